```python
import jax, jax.numpy as jnp
from jax import lax
import numpy as np

D_MODEL = 1024
BATCH = 2
SEQ = 8192
DEPTH = 2

HEAD_DIM = 64
ROT_DIM = HEAD_DIM // 4
ROPE_THETA = 500000.0
NORM_EPS = 1e-6
NEG = -1e30
Q_BLOCK = 128

A_HEADS = 8
MOBA_BLOCK = 256
MOBA_TOPK = 3
B_HEADS = 8
IDX_HEADS = 4
IDX_DIM = 64
DSA_TOPK = 256
C_HEADS = 16
DILATED_PAIRS = ((128, 1), (512, 4), (2048, 16))
PEER_HEADS = 8
N_KEYS = 128
N_EXPERTS = N_KEYS * N_KEYS
PEER_TOPK = 16
PEER_QDIM = 256
PEER_TOKEN_BLOCK = 128
PLE_DIM = 256

N_EVEN = (DEPTH + 1) // 2
N_ODD = DEPTH // 2
A_WIDTH = A_HEADS * HEAD_DIM
B_WIDTH = B_HEADS * HEAD_DIM
AB_WIDTH = A_WIDTH + B_WIDTH
AB_IN = 3 * AB_WIDTH + IDX_HEADS * IDX_DIM + IDX_DIM + IDX_HEADS
C_WIDTH = C_HEADS * HEAD_DIM
C_IN = 3 * C_WIDTH

kernel_name = "hybrid_moba_dsa_dilated_peer_trunk"


def rmsnorm(x, g):
    xf = x.astype(jnp.float32)
    y = xf * lax.rsqrt(jnp.mean(xf * xf, axis=-1, keepdims=True) + NORM_EPS)
    return (y * g.astype(jnp.float32)).astype(x.dtype)


def rope_tables(positions):
    half = ROT_DIM // 2
    inv = ROPE_THETA ** (-jnp.arange(half, dtype=jnp.float32) / half)
    ang = positions.astype(jnp.float32)[..., None] * inv
    return jnp.cos(ang)[:, :, None, :], jnp.sin(ang)[:, :, None, :]


def partial_rope(x, cos, sin):
    half = ROT_DIM // 2
    xf = x.astype(jnp.float32)
    x1 = xf[..., :half]
    x2 = xf[..., half:ROT_DIM]
    out = jnp.concatenate([x1 * cos - x2 * sin, x2 * cos + x1 * sin, xf[..., ROT_DIM:]], axis=-1)
    return out.astype(x.dtype)


def qk_prep(q, k, gq, gk, cos, sin):
    return partial_rope(rmsnorm(q, gq), cos, sin), partial_rope(rmsnorm(k, gk), cos, sin)


def moba_attention(q, k, v):
    B, S, H, dh = q.shape
    scale = dh ** -0.5
    Sp = -(-S // MOBA_BLOCK) * MOBA_BLOCK
    pad = ((0, 0), (0, Sp - S), (0, 0), (0, 0))
    qp, kp, vp = [jnp.pad(a, pad).transpose(0, 2, 1, 3) for a in (q, k, v)]
    nb = Sp // MOBA_BLOCK
    qb = qp.reshape(B, H, nb, MOBA_BLOCK, dh)
    kb = kp.reshape(B, H, nb, MOBA_BLOCK, dh)
    vb = vp.reshape(B, H, nb, MOBA_BLOCK, dh)
    s_own = jnp.einsum('bhnqd,bhnkd->bhnqk', qb, kb).astype(jnp.float32) * scale
    causal = jnp.tril(jnp.ones((MOBA_BLOCK, MOBA_BLOCK), dtype=bool))
    s_own = jnp.where(causal, s_own, NEG)
    m_own = jnp.max(s_own, axis=-1)
    p_own = jnp.exp(s_own - m_own[..., None])
    l_own = jnp.sum(p_own, axis=-1)
    o_own = jnp.einsum('bhnqk,bhnkd->bhnqd', p_own, vb.astype(jnp.float32)) / l_own[..., None]
    o_own = o_own.reshape(B, H, Sp, dh)
    lse_own = (m_own + jnp.log(l_own)).reshape(B, H, Sp)
    n_sel = min(MOBA_TOPK, nb - 1)
    if n_sel == 0:
        out = o_own
    else:
        kmean = jnp.mean(kb.astype(jnp.float32), axis=3)
        gate = jnp.einsum('bhsd,bhnd->bhsn', qp.astype(jnp.float32), kmean)
        qblk = jnp.arange(Sp) // MOBA_BLOCK
        past = jnp.arange(nb)[None, :] < qblk[:, None]
        gate = jnp.where(past, gate, NEG)
        _, gidx = lax.top_k(gate, n_sel)
        gvalid = gidx < qblk[:, None]
        nq = Sp // Q_BLOCK
        to_chunks = lambda a: jnp.moveaxis(a.reshape(B, H, nq, Q_BLOCK, *a.shape[3:]), 2, 0)
        bi = jnp.arange(B)[:, None, None, None]
        hi = jnp.arange(H)[None, :, None, None]

        def sel_step(args):
            qc, ic, vc = args
            kg = kb[bi, hi, ic]
            vg = vb[bi, hi, ic]
            s = jnp.einsum('bhqd,bhqnkd->bhqnk', qc, kg).astype(jnp.float32) * scale
            s = jnp.where(vc[..., None], s, NEG).reshape(B, H, Q_BLOCK, n_sel * MOBA_BLOCK)
            m = jnp.max(s, axis=-1)
            pr = jnp.exp(s - m[..., None])
            l = jnp.sum(pr, axis=-1)
            o = jnp.einsum('bhqj,bhqjd->bhqd', pr,
                           vg.reshape(B, H, Q_BLOCK, n_sel * MOBA_BLOCK, dh).astype(jnp.float32)) / l[..., None]
            return o, m + jnp.log(l)

        o_sel, lse_sel = lax.map(sel_step, (to_chunks(qp), to_chunks(gidx), to_chunks(gvalid)))
        o_sel = jnp.moveaxis(o_sel, 0, 2).reshape(B, H, Sp, dh)
        lse_sel = jnp.moveaxis(lse_sel, 0, 2).reshape(B, H, Sp)
        mx = jnp.maximum(lse_own, lse_sel)
        w1 = jnp.exp(lse_own - mx)
        w2 = jnp.exp(lse_sel - mx)
        out = (o_own * w1[..., None] + o_sel * w2[..., None]) / (w1 + w2)[..., None]
    return out.transpose(0, 2, 1, 3)[:, :S].astype(q.dtype)


def dsa_attention(q, k, v, q_idx, k_idx, w_idx):
    B, S, H, dh = q.shape
    n_sel = min(DSA_TOPK, S // 4)
    nq = S // Q_BLOCK
    scale = dh ** -0.5
    kpos = jnp.arange(S)
    bi = jnp.arange(B)[:, None, None]
    to_chunks = lambda a: jnp.moveaxis(a.reshape(B, nq, Q_BLOCK, *a.shape[2:]), 1, 0)

    def step(args):
        qc, qic, wc, t0 = args
        tpos = t0 + jnp.arange(Q_BLOCK)
        logits = jnp.einsum('bqhd,bsd->bqhs', qic, k_idx).astype(jnp.float32)
        score = jnp.einsum('bqh,bqhs->bqs', wc.astype(jnp.float32), jax.nn.relu(logits))
        adm = kpos[None, :] <= tpos[:, None]
        score = jnp.where(adm, score, NEG)
        _, sidx = lax.top_k(score, n_sel)
        valid = sidx <= tpos[None, :, None]
        kg = k[bi, sidx]
        vg = v[bi, sidx]
        s = jnp.einsum('bqhd,bqnhd->bqhn', qc, kg).astype(jnp.float32) * scale
        s = jnp.where(valid[:, :, None, :], s, NEG)
        pr = jax.nn.softmax(s, axis=-1)
        return jnp.einsum('bqhn,bqnhd->bqhd', pr, vg.astype(jnp.float32)).astype(q.dtype)

    t0s = jnp.arange(nq, dtype=jnp.int32) * Q_BLOCK
    o = lax.map(step, (to_chunks(q), to_chunks(q_idx), to_chunks(w_idx), t0s))
    return jnp.moveaxis(o, 0, 1).reshape(B, S, H, dh)


def dilated_branch(q, k, v, window, dil):
    B, S, H, dh = q.shape
    band = window // dil
    scale = dh ** -0.5
    span = dil * band
    Sp = -(-S // span) * span
    M = Sp // dil
    nb = M // band

    def split(a):
        a = jnp.pad(a, ((0, 0), (0, Sp - S), (0, 0), (0, 0)))
        a = a.reshape(B, M, dil, H, dh).transpose(0, 2, 3, 1, 4)
        return a.reshape(B, dil, H, nb, band, dh)

    def with_prev(a):
        prev = jnp.pad(a, ((0, 0), (0, 0), (0, 0), (1, 0), (0, 0), (0, 0)))[:, :, :, :-1]
        return jnp.concatenate([prev, a], axis=4)

    qb = split(q)
    kk = with_prev(split(k))
    vv = with_prev(split(v))
    s = jnp.einsum('brhnqd,brhnkd->brhnqk', qb, kk).astype(jnp.float32) * scale
    i = jnp.arange(band)[:, None]
    j = jnp.arange(2 * band)[None, :]
    dist = i + band - j
    in_band = (dist >= 0) & (dist <= band)
    mask = in_band[None] & ((jnp.arange(nb)[:, None, None] > 0) | (j[None] >= band))
    s = jnp.where(mask, s, NEG)
    m = jnp.max(s, axis=-1)
    pr = jnp.exp(s - m[..., None])
    l = jnp.sum(pr, axis=-1)
    o = jnp.einsum('brhnqk,brhnkd->brhnqd', pr, vv.astype(jnp.float32)) / l[..., None]
    lse = m + jnp.log(l)
    o = o.reshape(B, dil, H, M, dh).transpose(0, 3, 1, 2, 4).reshape(B, Sp, H, dh)[:, :S]
    lse = lse.reshape(B, dil, H, M).transpose(0, 3, 1, 2).reshape(B, Sp, H)[:, :S]
    return o, lse


def dilated_attention(q, k, v):
    outs, lses = [], []
    for window, dil in DILATED_PAIRS:
        o, lse = dilated_branch(q, k, v, window, dil)
        outs.append(o)
        lses.append(lse)
    wts = jax.nn.softmax(jnp.stack(lses, axis=0), axis=0)
    return jnp.einsum('gbsh,gbshd->bshd', wts, jnp.stack(outs, axis=0)).astype(q.dtype)


def ab_mixer(xn, cos, sin, w_in, w_out, a_q_norm, a_k_norm, b_q_norm, b_k_norm):
    B, S, _ = xn.shape
    sizes = [A_WIDTH, A_WIDTH, A_WIDTH, B_WIDTH, B_WIDTH, B_WIDTH, IDX_HEADS * IDX_DIM, IDX_DIM, IDX_HEADS]
    cuts = np.cumsum(sizes)[:-1].tolist()
    qa, ka, va, qb, kb, vb, qi, ki, wi = jnp.split(xn @ w_in, cuts, axis=-1)
    heads = lambda a, n: a.reshape(B, S, n, -1)
    qa, ka = qk_prep(heads(qa, A_HEADS), heads(ka, A_HEADS), a_q_norm, a_k_norm, cos, sin)
    oa = moba_attention(qa, ka, heads(va, A_HEADS))
    qb, kb = qk_prep(heads(qb, B_HEADS), heads(kb, B_HEADS), b_q_norm, b_k_norm, cos, sin)
    qi = partial_rope(heads(qi, IDX_HEADS), cos, sin)
    ki = partial_rope(ki[:, :, None, :], cos, sin)[:, :, 0]
    ob = dsa_attention(qb, kb, heads(vb, B_HEADS), qi, ki, wi)
    o = jnp.concatenate([oa.reshape(B, S, A_WIDTH), ob.reshape(B, S, B_WIDTH)], axis=-1)
    return o @ w_out


def c_mixer(xn, cos, sin, w_in, w_out, q_norm, k_norm):
    B, S, _ = xn.shape
    q, k, v = jnp.split(xn @ w_in, 3, axis=-1)
    heads = lambda a: a.reshape(B, S, C_HEADS, HEAD_DIM)
    q, k = qk_prep(heads(q), heads(k), q_norm, k_norm, cos, sin)
    o = dilated_attention(q, k, heads(v))
    return o.reshape(B, S, C_WIDTH) @ w_out


def peer_ffn(xn, w_query, sub_keys, expert_u, expert_v):
    B, S, D = xn.shape
    T = B * S
    xt = xn.reshape(T // PEER_TOKEN_BLOCK, PEER_TOKEN_BLOCK, D)

    def step(xc):
        tc = xc.shape[0]
        qry = (xc @ w_query).reshape(tc, PEER_HEADS, 2, PEER_QDIM // 2)
        s = jnp.einsum('thcd,hckd->thck', qry, sub_keys).astype(jnp.float32)
        sv, si = lax.top_k(s, PEER_TOPK)
        cand = (sv[:, :, 0, :, None] + sv[:, :, 1, None, :]).reshape(tc, PEER_HEADS, PEER_TOPK * PEER_TOPK)
        cidx = (si[:, :, 0, :, None] * N_KEYS + si[:, :, 1, None, :]).reshape(tc, PEER_HEADS, PEER_TOPK * PEER_TOPK)
        cv, ci = lax.top_k(cand, PEER_TOPK)
        eidx = jnp.take_along_axis(cidx, ci, axis=-1)
        g = jax.nn.softmax(cv, axis=-1)
        u = expert_u[eidx]
        a = jax.nn.gelu(jnp.einsum('td,thkd->thk', xc, u).astype(jnp.float32))
        ve = expert_v[eidx]
        return jnp.einsum('thk,thkd->td', (g * a).astype(ve.dtype), ve).astype(xc.dtype)

    return lax.map(step, xt).reshape(B, S, D)


def setup_inputs(seed: int = 0) -> dict:
    key = jax.random.key(seed)
    ks = jax.random.split(key, 24)
    f32 = jnp.float32
    nrm = lambda k, shape, sc: jax.random.normal(k, shape, f32) * sc
    gain = lambda k, shape: 1.0 + 0.05 * jax.random.normal(k, shape, f32)
    return {
        "x": nrm(ks[0], (BATCH, SEQ, D_MODEL), 1.0),
        "p": nrm(ks[1], (DEPTH, BATCH, SEQ, PLE_DIM), 1.0),
        "positions": jnp.broadcast_to(jnp.arange(SEQ, dtype=jnp.int32), (BATCH, SEQ)),
        "attn_norm": gain(ks[2], (DEPTH, D_MODEL)),
        "ffn_norm": gain(ks[3], (DEPTH, D_MODEL)),
        "ple_norm": gain(ks[4], (DEPTH, D_MODEL)),
        "ab_w_in": nrm(ks[5], (N_EVEN, D_MODEL, AB_IN), D_MODEL ** -0.5),
        "ab_w_out": nrm(ks[6], (N_EVEN, AB_WIDTH, D_MODEL), AB_WIDTH ** -0.5),
        "a_q_norm": gain(ks[7], (N_EVEN, HEAD_DIM)),
        "a_k_norm": gain(ks[8], (N_EVEN, HEAD_DIM)),
        "b_q_norm": gain(ks[9], (N_EVEN, HEAD_DIM)),
        "b_k_norm": gain(ks[10], (N_EVEN, HEAD_DIM)),
        "c_w_in": nrm(ks[11], (N_ODD, D_MODEL, C_IN), D_MODEL ** -0.5),
        "c_w_out": nrm(ks[12], (N_ODD, C_WIDTH, D_MODEL), C_WIDTH ** -0.5),
        "c_q_norm": gain(ks[13], (N_ODD, HEAD_DIM)),
        "c_k_norm": gain(ks[14], (N_ODD, HEAD_DIM)),
        "peer_w_query": nrm(ks[15], (DEPTH, D_MODEL, PEER_HEADS * PEER_QDIM), D_MODEL ** -0.5),
        "peer_sub_keys": nrm(ks[16], (DEPTH, PEER_HEADS, 2, N_KEYS, PEER_QDIM // 2), (PEER_QDIM // 2) ** -0.5),
        "peer_u": nrm(ks[17], (DEPTH, N_EXPERTS, D_MODEL), D_MODEL ** -0.5),
        "peer_v": nrm(ks[18], (DEPTH, N_EXPERTS, D_MODEL), PEER_HEADS ** -0.5),
        "ple_w_gate": nrm(ks[19], (DEPTH, D_MODEL, D_MODEL), D_MODEL ** -0.5),
        "ple_w_up": nrm(ks[20], (DEPTH, PLE_DIM, D_MODEL), PLE_DIM ** -0.5),
    }


def reference(x, p, positions, attn_norm, ffn_norm, ple_norm, ab_w_in, ab_w_out,
              a_q_norm, a_k_norm, b_q_norm, b_k_norm, c_w_in, c_w_out, c_q_norm, c_k_norm,
              peer_w_query, peer_sub_keys, peer_u, peer_v, ple_w_gate, ple_w_up):
    cos, sin = rope_tables(positions)
    h = x
    for i in range(DEPTH):
        xn = rmsnorm(h, attn_norm[i])
        if i % 2 == 0:
            j = i // 2
            h = h + ab_mixer(xn, cos, sin, ab_w_in[j], ab_w_out[j],
                             a_q_norm[j], a_k_norm[j], b_q_norm[j], b_k_norm[j])
        else:
            j = i // 2
            h = h + c_mixer(xn, cos, sin, c_w_in[j], c_w_out[j], c_q_norm[j], c_k_norm[j])
        h = h + peer_ffn(rmsnorm(h, ffn_norm[i]), peer_w_query[i], peer_sub_keys[i], peer_u[i], peer_v[i])
        gate = jax.nn.sigmoid(rmsnorm(h, ple_norm[i]) @ ple_w_gate[i])
        h = h + (p[i] @ ple_w_up[i]) * gate
    return h
```

```python
import functools
import math

import numpy as np
import jax
import jax.numpy as jnp
from jax import lax
from jax.experimental import pallas as pl
from jax.experimental.pallas import tpu as pltpu

F32 = jnp.float32
BF16 = jnp.bfloat16

D_MODEL = 1024
HEAD_DIM = 64
ROT_HALF = 8
ROPE_THETA = 500000.0
NORM_EPS = 1e-6
NEG = -1e30
LANES = 128

A_HEADS = 8
B_HEADS = 8
C_HEADS = 16
IDX_HEADS = 4
MOBA_BLOCK = 256
MOBA_TOPK = 3
DSA_TOPK = 256
DILATED_PAIRS = ((128, 1), (512, 4), (2048, 16))
PEER_HEADS = 8
N_KEYS = 128
PEER_TOPK = 16

V7X_VMEM_BYTES = 64 * 1024 * 1024


def _cparams(sem, vmem_mb):
    assert vmem_mb * 2**20 < V7X_VMEM_BYTES
    return pltpu.CompilerParams(dimension_semantics=sem, vmem_limit_bytes=vmem_mb * 2**20)


def _rms(x, g):
    ms = jnp.mean(x * x, axis=-1, keepdims=True)
    return x * lax.rsqrt(ms + NORM_EPS) * g


def _nt_dot(a, b):
    return lax.dot_general(a, b, (((1,), (1,)), ((), ())), preferred_element_type=F32)


def _split_bf16(x):
    hi = x.astype(BF16)
    lo = (x - hi.astype(F32)).astype(BF16)
    return hi, lo


def _proj_kernel(x_ref, g_ref, w_ref, gains_ref, cos_ref, sa_ref, sb_ref, y_ref, aux_ref, *, kinds, group):
    xn = _rms(x_ref[...], g_ref[...]).astype(BF16)
    row = lax.broadcasted_iota(jnp.int32, (LANES, LANES), 0) // HEAD_DIM
    col = lax.broadcasted_iota(jnp.int32, (LANES, LANES), 1) // HEAD_DIM
    head_avg = jnp.where(row == col, 1.0 / HEAD_DIM, 0.0).astype(BF16)
    n_chunks = len(kinds)
    for g0 in range(0, n_chunks, group):
        g1 = min(g0 + group, n_chunks)
        yg = jnp.dot(xn, w_ref[:, g0 * LANES:g1 * LANES], preferred_element_type=F32)
        for c in range(g0, g1):
            y = yg[:, (c - g0) * LANES:(c - g0 + 1) * LANES]
            kind = kinds[c]
            if "n" in kind:
                hi, lo = _split_bf16(y * y)
                msq = (jnp.dot(hi, head_avg, preferred_element_type=F32)
                       + jnp.dot(lo, head_avg, preferred_element_type=F32))
                y = y * lax.rsqrt(msq + NORM_EPS) * gains_ref[c:c + 1, :]
            if "r" in kind:
                y = (y * cos_ref[...] + pltpu.roll(y, LANES - ROT_HALF, 1) * sa_ref[...]
                     + pltpu.roll(y, ROT_HALF, 1) * sb_ref[...])
            if "a" in kind:
                aux_ref[...] = y
            y_ref[c] = y.astype(BF16)


def _project(h, g, w_bf16, gains, rope, kinds, *, tm=512, group=4):
    T, D = h.shape
    n_chunks = len(kinds)
    assert w_bf16.shape == (D, n_chunks * LANES) and T % tm == 0
    cos, sa, sb = rope
    tok = lambda i: (i, 0)
    fixed = lambda i: (0, 0)
    return pl.pallas_call(
        functools.partial(_proj_kernel, kinds=tuple(kinds), group=group),
        grid=(T // tm,),
        in_specs=[pl.BlockSpec((tm, D), tok), pl.BlockSpec((1, D), fixed),
                  pl.BlockSpec((D, n_chunks * LANES), fixed), pl.BlockSpec((n_chunks, LANES), fixed),
                  pl.BlockSpec((tm, LANES), tok), pl.BlockSpec((tm, LANES), tok), pl.BlockSpec((tm, LANES), tok)],
        out_specs=[pl.BlockSpec((n_chunks, tm, LANES), lambda i: (0, i, 0)), pl.BlockSpec((tm, LANES), tok)],
        out_shape=[jax.ShapeDtypeStruct((n_chunks, T, LANES), BF16), jax.ShapeDtypeStruct((T, LANES), F32)],
        compiler_params=_cparams(("parallel",), 48),
        name="proj",
    )(h, g.reshape(1, D), w_bf16, gains, cos, sa, sb)


def _rope_tables(positions):
    B, S = positions.shape
    inv = ROPE_THETA ** (-jnp.arange(ROT_HALF, dtype=F32) / ROT_HALF)
    ang = positions.astype(F32)[..., None] * inv
    cos, sin = jnp.cos(ang), jnp.sin(ang)
    rest = HEAD_DIM - 2 * ROT_HALF
    one, zero, z8 = jnp.ones((B, S, rest), F32), jnp.zeros((B, S, rest), F32), jnp.zeros((B, S, ROT_HALF), F32)
    tile = lambda parts: jnp.tile(jnp.concatenate(parts, axis=-1), (1, 1, LANES // HEAD_DIM)).reshape(B * S, LANES)
    return tile([cos, cos, one]), tile([-sin, z8, zero]), tile([z8, sin, zero])


def _kmean_kernel(k_ref, o_ref):
    n = pl.program_id(1)

    @pl.when(n == 0)
    def _():
        o_ref[...] = jnp.zeros_like(o_ref)

    o_ref[:, pl.ds(n, 1), :] = jnp.mean(k_ref[...].astype(F32), axis=1, keepdims=True)


def _block_means(y, k_off, pairs, B, S):
    nb = S // MOBA_BLOCK
    assert nb <= LANES and k_off % pairs == 0
    return pl.pallas_call(
        _kmean_kernel,
        grid=(B, nb),
        in_specs=[pl.BlockSpec((pairs, MOBA_BLOCK, LANES), lambda b, n: (k_off // pairs, b * nb + n, 0))],
        out_specs=pl.BlockSpec((pairs, LANES, LANES), lambda b, n: (0, b, 0)),
        out_shape=jax.ShapeDtypeStruct((pairs, B * LANES, LANES), F32),
        compiler_params=_cparams(("parallel", "arbitrary"), 16),
        name="moba_kmean",
    )(y)


def _flash_kernel(*refs, mode, pairs, tq, tk, n_steps, back):
    if mode == "moba":
        q_ref, k_ref, v_ref, km_ref, o_ref, m_scr, l_scr, acc_scr, bias_scr, sel_scr = refs
    elif mode == "dsa":
        q_ref, k_ref, v_ref, b_ref, o_ref, m_scr, l_scr, acc_scr, bias_scr = refs
    else:
        q_ref, k_ref, v_ref, o_ref, m_scr, l_scr, acc_scr, bias_scr = refs
    i = pl.program_id(1)
    j = pl.program_id(2)
    lane = lax.broadcasted_iota(jnp.int32, (1, LANES), 1)
    halves = (lane < HEAD_DIM, lane >= HEAD_DIM)
    if mode == "dilated":
        kt = i * (tq // tk) - back + j
        active = kt >= 0
    else:
        kt = j
        active = j * tk < (i + 1) * tq
    tpos = i * tq + lax.broadcasted_iota(jnp.int32, (tq, 1), 0)

    @pl.when(j == 0)
    def _():
        m_scr[...] = jnp.full_like(m_scr, NEG)
        l_scr[...] = jnp.zeros_like(l_scr)
        acc_scr[...] = jnp.zeros_like(acc_scr)
        if mode == "moba":
            blk = lax.broadcasted_iota(jnp.int32, (tq, LANES), 1)
            past = blk < tpos // MOBA_BLOCK
            for c in range(pairs):
                km_hi, km_lo = _split_bf16(km_ref[c])
                for half in range(2):
                    qm = jnp.where(halves[half], q_ref[c], jnp.zeros((), BF16))
                    gate = jnp.where(past, _nt_dot(qm, km_hi) + _nt_dot(qm, km_lo), NEG)
                    rest, thr = gate, None
                    for _ in range(MOBA_TOPK):
                        thr = jnp.max(rest, axis=1, keepdims=True)
                        rest = jnp.where(rest >= thr, NEG, rest)
                    sel_scr[2 * c + half] = jnp.where((gate >= thr) & past, 1.0, 0.0)

    @pl.when(active)
    def _():
        kpos = kt * tk + lax.broadcasted_iota(jnp.int32, (tq, tk), 1)
        if mode == "moba":
            own = (kpos <= tpos) & (kpos // MOBA_BLOCK == tpos // MOBA_BLOCK)
            bias_scr[...] = jnp.where(own, 0.0, -jnp.inf)
        elif mode == "dsa":
            bias_scr[...] = b_ref[0, 0].astype(F32)
        else:
            d = tpos - kpos
            mult = jnp.zeros((tq, tk), F32)
            for window, dil in DILATED_PAIRS:
                hit = (d >= 0) & (d <= window) & ((d & (dil - 1)) == 0)
                mult = mult + jnp.where(hit, 1.0, 0.0)
            bias_scr[...] = jnp.where(mult > 0.0, jnp.log(jnp.maximum(mult, 1.0)), -jnp.inf)

        def pair_body(c, carry):
            qp, kp, vp = q_ref[c], k_ref[c], v_ref[c]
            for half in range(2):
                h = 2 * c + half
                qm = jnp.where(halves[half], qp, jnp.zeros((), BF16))
                if mode == "moba":
                    blk = lax.broadcasted_iota(jnp.int32, (tq, LANES), 1)
                    picked = jnp.sum(jnp.where(blk == kt * tk // MOBA_BLOCK, sel_scr[h], 0.0),
                                     axis=1, keepdims=True) > 0.0
                    bias = jnp.where(picked, 0.0, bias_scr[...])
                else:
                    bias = bias_scr[...]
                s = _nt_dot(qm, kp) + bias
                m_old = m_scr[h]
                m_new = jnp.maximum(m_old, jnp.max(s, axis=1, keepdims=True))
                alpha = jnp.exp(m_old - m_new)
                p = jnp.exp(s - m_new)
                l_scr[h] = alpha * l_scr[h] + jnp.sum(p, axis=1, keepdims=True)
                acc_scr[h] = alpha * acc_scr[h] + jnp.dot(p.astype(BF16), vp, preferred_element_type=F32)
                m_scr[h] = m_new
            return carry

        lax.fori_loop(0, pairs, pair_body, 0)

    @pl.when(j == n_steps - 1)
    def _():
        for c in range(pairs):
            lo = acc_scr[2 * c] / l_scr[2 * c]
            hi = acc_scr[2 * c + 1] / l_scr[2 * c + 1]
            o_ref[c] = jnp.where(halves[0], lo, hi).astype(BF16)


def _flash(mode, q_arr, k_arr, v_arr, q_off, k_off, v_off, pairs, B, S, *, tq, tk, extra=None):
    nq, nk = S // tq, S // tk
    assert S % tq == 0 and S % tk == 0 and tq % tk == 0
    assert q_off % pairs == 0 and k_off % pairs == 0 and v_off % pairs == 0
    if mode == "dilated":
        max_back = max(w for w, _ in DILATED_PAIRS)
        back = -(-max_back // tk)
        n_steps = back + tq // tk
        kv_tile = lambda i, j: jnp.maximum(i * (tq // tk) - back + j, 0)
    else:
        back = 0
        n_steps = nk
        kv_tile = lambda i, j: jnp.minimum(j, ((i + 1) * tq - 1) // tk)
    in_specs = [pl.BlockSpec((pairs, tq, LANES), lambda b, i, j: (q_off // pairs, b * nq + i, 0)),
                pl.BlockSpec((pairs, tk, LANES), lambda b, i, j: (k_off // pairs, b * nk + kv_tile(i, j), 0)),
                pl.BlockSpec((pairs, tk, LANES), lambda b, i, j: (v_off // pairs, b * nk + kv_tile(i, j), 0))]
    args = [q_arr, k_arr, v_arr]
    scratch = [pltpu.VMEM((2 * pairs, tq, 1), F32), pltpu.VMEM((2 * pairs, tq, 1), F32),
               pltpu.VMEM((2 * pairs, tq, LANES), F32), pltpu.VMEM((tq, tk), F32)]
    if mode == "moba":
        assert tk == MOBA_BLOCK
        in_specs.append(pl.BlockSpec((pairs, LANES, LANES), lambda b, i, j: (0, b, 0)))
        args.append(extra)
        scratch.append(pltpu.VMEM((2 * pairs, tq, LANES), F32))
    elif mode == "dsa":
        in_specs.append(pl.BlockSpec((1, 1, tq, tk), lambda b, i, j: (b, kv_tile(i, j), i, 0)))
        args.append(extra)
    return pl.pallas_call(
        functools.partial(_flash_kernel, mode=mode, pairs=pairs, tq=tq, tk=tk, n_steps=n_steps, back=back),
        grid=(B, nq, n_steps),
        in_specs=in_specs,
        out_specs=pl.BlockSpec((pairs, tq, LANES), lambda b, i, j: (0, b * nq + i, 0)),
        out_shape=jax.ShapeDtypeStruct((pairs, B * S, LANES), BF16),
        scratch_shapes=scratch,
        compiler_params=_cparams(("parallel", "parallel", "arbitrary"), 48),
        name="flash_" + mode,
    )(*args)


def _dsa_select_kernel(qi_ref, ki_ref, w_ref, o_ref, key_scr, *, tq, ck, n_chunks, topk, idx_bits):
    i = pl.program_id(1)
    lane = lax.broadcasted_iota(jnp.int32, (1, LANES), 1)
    halves = (lane < HEAD_DIM, lane >= HEAD_DIM)
    tpos = i * tq + lax.broadcasted_iota(jnp.int32, (tq, 1), 0)
    n_live = ((i + 1) * tq + ck - 1) // ck
    w = w_ref[...]

    def score_chunk(c, carry):
        kc = ki_ref[0, pl.ds(pl.multiple_of(c * ck, ck), ck), :]
        score = jnp.zeros((tq, ck), F32)
        for h in range(IDX_HEADS):
            qm = jnp.where(halves[h % 2], qi_ref[h // 2], jnp.zeros((), BF16))
            score = score + w[:, h:h + 1] * jnp.maximum(_nt_dot(qm, kc), 0.0)
        kpos = c * ck + lax.broadcasted_iota(jnp.int32, (tq, ck), 1)
        score = jnp.where(kpos <= tpos, score + 0.0, NEG)
        bits = pltpu.bitcast(score, jnp.int32)
        key_scr[c] = jnp.where(bits < 0, bits ^ jnp.int32(0x7FFFFFFF), bits)
        return carry

    lax.fori_loop(0, n_live, score_chunk, 0)

    def count(pred):
        def body(c, acc):
            hit = jnp.where(pred(key_scr[c], c), 1.0, 0.0)
            for s in range(ck // LANES):
                acc = acc + hit[:, s * LANES:(s + 1) * LANES]
            return acc
        acc = lax.fori_loop(0, n_live, body, jnp.zeros((tq, LANES), F32))
        return jnp.sum(acc, axis=1, keepdims=True)

    def value_bit(it, lo):
        cand = lo + jnp.left_shift(jnp.int32(1), 31 - it)
        cnt = count(lambda kc, c: kc >= cand)
        return jnp.where(cnt >= topk, cand, lo)

    thr = lax.fori_loop(0, 32, value_bit, jnp.full((tq, 1), -2**31, jnp.int32))
    need = topk - count(lambda kc, c: kc > thr)

    def index_bit(it, lo):
        cand = lo + jnp.left_shift(jnp.int32(1), idx_bits - 1 - it)
        def pred(kc, c):
            kpos = c * ck + lax.broadcasted_iota(jnp.int32, (tq, ck), 1)
            return (kc == thr) & (kpos < cand)
        return jnp.where(count(pred) < need, cand, lo)

    last = lax.fori_loop(0, idx_bits, index_bit, jnp.zeros((tq, 1), jnp.int32))

    def emit(c, carry):
        kc = key_scr[c]
        kpos = c * ck + lax.broadcasted_iota(jnp.int32, (tq, ck), 1)
        take = ((kc > thr) | ((kc == thr) & (kpos <= last))) & (kpos <= tpos)
        o_ref[0, c] = jnp.where(take, 0.0, -jnp.inf).astype(BF16)
        return carry

    lax.fori_loop(0, n_live, emit, 0)

    def blank(c, carry):
        o_ref[0, c] = jnp.full((tq, ck), -jnp.inf, BF16)
        return carry

    lax.fori_loop(n_live, n_chunks, blank, 0)


def _dsa_select(y, aux, qi_off, ki_off, B, S, *, tq=256, ck=512):
    n_chunks = S // ck
    idx_bits = int(math.log2(S))
    assert 2**idx_bits == S and S % ck == 0 and S % tq == 0 and qi_off % 2 == 0
    nq = S // tq
    topk = min(DSA_TOPK, S // 4)
    return pl.pallas_call(
        functools.partial(_dsa_select_kernel, tq=tq, ck=ck, n_chunks=n_chunks, topk=topk, idx_bits=idx_bits),
        grid=(B, nq),
        in_specs=[pl.BlockSpec((2, tq, LANES), lambda b, i: (qi_off // 2, b * nq + i, 0)),
                  pl.BlockSpec((1, S, LANES), lambda b, i: (ki_off, b, 0)),
                  pl.BlockSpec((tq, LANES), lambda b, i: (b * nq + i, 0))],
        out_specs=pl.BlockSpec((1, n_chunks, tq, ck), lambda b, i: (b, 0, i, 0)),
        out_shape=jax.ShapeDtypeStruct((B, n_chunks, S, ck), BF16),
        scratch_shapes=[pltpu.VMEM((n_chunks, tq, ck), jnp.int32)],
        compiler_params=_cparams(("parallel", "parallel"), 48),
        name="dsa_select",
    )(y, y, aux)


def _out_proj_kernel(*refs, n_in):
    o_refs, w_ref, h_ref, out_ref = refs[:n_in], refs[n_in], refs[n_in + 1], refs[n_in + 2]
    slabs = [o[c] for o in o_refs for c in range(o.shape[0])]
    o = jnp.concatenate(slabs, axis=-1)
    out_ref[...] = h_ref[...] + jnp.dot(o, w_ref[...], preferred_element_type=F32)


def _out_proj(os_, w_bf16, h, *, tm=512):
    T, D = h.shape
    width = sum(o.shape[0] for o in os_) * LANES
    assert w_bf16.shape == (width, D)
    in_specs = [pl.BlockSpec((o.shape[0], tm, LANES), lambda i: (0, i, 0)) for o in os_]
    in_specs += [pl.BlockSpec((width, D), lambda i: (0, 0)), pl.BlockSpec((tm, D), lambda i: (i, 0))]
    return pl.pallas_call(
        functools.partial(_out_proj_kernel, n_in=len(os_)),
        grid=(T // tm,),
        in_specs=in_specs,
        out_specs=pl.BlockSpec((tm, D), lambda i: (i, 0)),
        out_shape=jax.ShapeDtypeStruct((T, D), F32),
        compiler_params=_cparams(("parallel",), 32),
        name="out_proj",
    )(*os_, w_bf16, h)


def _top_values(s, k):
    out = []
    for _ in range(k):
        cur = jnp.max(s, axis=0, keepdims=True)
        out.append(cur)
        s = jnp.where(s >= cur, -jnp.inf, s)
    return jnp.concatenate(out, axis=0)


def _pair_candidates(t1, t2):
    k = PEER_TOPK
    parts = [t1[0:1] + t2]
    parts += [t1[a:a + 1] + t2[0:k // 2] for a in range(1, k // 2)]
    parts.append(t1[k // 2:k] + t2[0:1])
    return jnp.concatenate(parts, axis=0)


def _kth_largest(c, k):
    left = jnp.full((1, c.shape[1]), float(k), F32)
    tau = jnp.zeros((1, c.shape[1]), F32)
    for _ in range(k):
        cur = jnp.max(c, axis=0, keepdims=True)
        hit = c >= cur
        tau = jnp.where(left > 0.0, cur, tau)
        left = left - jnp.sum(jnp.where(hit, 1.0, 0.0), axis=0, keepdims=True)
        c = jnp.where(hit, -jnp.inf, c)
    return tau


def _peer_route_kernel(h_ref, g_ref, wq_ref, sk_ref, xn_ref, s1_ref, s2_ref, tau_ref):
    xn = _rms(h_ref[...], g_ref[...]).astype(BF16)
    xn_ref[...] = xn
    taus = []
    for hd in range(PEER_HEADS):
        q = jnp.dot(xn, wq_ref[:, hd * 2 * N_KEYS:(hd + 1) * 2 * N_KEYS], preferred_element_type=F32).astype(BF16)
        s1 = _nt_dot(sk_ref[2 * hd], q[:, :N_KEYS])
        s2 = _nt_dot(sk_ref[2 * hd + 1], q[:, N_KEYS:])
        s1 = s1 - jnp.max(s1, axis=0, keepdims=True)
        s2 = s2 - jnp.max(s2, axis=0, keepdims=True)
        t1, t2 = _top_values(s1, PEER_TOPK), _top_values(s2, PEER_TOPK)
        cand = _pair_candidates(t1, t2)
        tau0 = _kth_largest(cand, PEER_TOPK)
        log_z = jnp.log(jnp.sum(jnp.where(cand >= tau0, jnp.exp(cand), 0.0), axis=0, keepdims=True))
        taus.append(_kth_largest(_pair_candidates(t1 - log_z, t2), PEER_TOPK))
        s1_ref[hd] = s1 - log_z
        s2_ref[hd] = s2
    tau_ref[...] = jnp.concatenate(taus, axis=0)


def _peer_route(h, g, wq_bf16, sk_bf16, *, tm=256):
    T, D = h.shape
    nq = PEER_HEADS * 2 * N_KEYS
    assert wq_bf16.shape == (D, nq) and sk_bf16.shape == (2 * PEER_HEADS, N_KEYS, N_KEYS)
    return pl.pallas_call(
        _peer_route_kernel,
        grid=(T // tm,),
        in_specs=[pl.BlockSpec((tm, D), lambda i: (i, 0)), pl.BlockSpec((1, D), lambda i: (0, 0)),
                  pl.BlockSpec((D, nq), lambda i: (0, 0)),
                  pl.BlockSpec((2 * PEER_HEADS, N_KEYS, N_KEYS), lambda i: (0, 0, 0))],
        out_specs=[pl.BlockSpec((tm, D), lambda i: (i, 0)),
                   pl.BlockSpec((PEER_HEADS, N_KEYS, tm), lambda i: (0, 0, i)),
                   pl.BlockSpec((PEER_HEADS, N_KEYS, tm), lambda i: (0, 0, i)),
                   pl.BlockSpec((PEER_HEADS, tm), lambda i: (0, i))],
        out_shape=[jax.ShapeDtypeStruct((T, D), BF16), jax.ShapeDtypeStruct((PEER_HEADS, N_KEYS, T), F32),
                   jax.ShapeDtypeStruct((PEER_HEADS, N_KEYS, T), F32), jax.ShapeDtypeStruct((PEER_HEADS, T), F32)],
        compiler_params=_cparams(("parallel",), 40),
        name="peer_route",
    )(h, g.reshape(1, D), wq_bf16, sk_bf16)


def _gelu_tanh(x):
    return 0.5 * x * (1.0 + jnp.tanh(math.sqrt(2.0 / math.pi) * (x + 0.044715 * (x * x * x))))


def _peer_expert_kernel(xn_ref, u_ref, vt_ref, s1_ref, s2_ref, tau_ref, h_ref, out_ref, acc_scr, a_scr, *, eb, tq):
    e = pl.program_id(1)
    rows = eb // N_KEYS

    @pl.when(e == 0)
    def _():
        acc_scr[...] = jnp.zeros_like(acc_scr)

    act = _gelu_tanh(_nt_dot(u_ref[...], xn_ref[...]))
    row0 = pl.multiple_of(e * rows, rows)
    for r in range(rows):
        for t0 in range(0, tq, LANES):
            w = jnp.zeros((N_KEYS, LANES), F32)
            for hd in range(PEER_HEADS):
                s1_rows = s1_ref[hd, pl.ds(row0, rows), t0:t0 + LANES]
                c = s2_ref[hd, :, t0:t0 + LANES] + s1_rows[r:r + 1, :]
                w = w + jnp.where(c >= tau_ref[hd:hd + 1, t0:t0 + LANES], jnp.exp(c), 0.0)
            a_scr[r * N_KEYS:(r + 1) * N_KEYS, t0:t0 + LANES] = (
                act[r * N_KEYS:(r + 1) * N_KEYS, t0:t0 + LANES] * w).astype(BF16)
    acc_scr[...] += jnp.dot(vt_ref[...], a_scr[...], preferred_element_type=F32)

    @pl.when(e == pl.num_programs(1) - 1)
    def _():
        out_ref[...] = h_ref[...] + acc_scr[...].T


def _peer_experts(xn, u_bf16, vt_bf16, s1, s2, tau, h, *, tq=512, eb=1024):
    T, D = h.shape
    E = u_bf16.shape[0]
    assert E == N_KEYS * N_KEYS and E % eb == 0 and eb % (8 * N_KEYS) == 0 and T % tq == 0
    return pl.pallas_call(
        functools.partial(_peer_expert_kernel, eb=eb, tq=tq),
        grid=(T // tq, E // eb),
        in_specs=[pl.BlockSpec((tq, D), lambda i, e: (i, 0)),
                  pl.BlockSpec((eb, D), lambda i, e: (e, 0)),
                  pl.BlockSpec((D, eb), lambda i, e: (0, e)),
                  pl.BlockSpec((PEER_HEADS, N_KEYS, tq), lambda i, e: (0, 0, i)),
                  pl.BlockSpec((PEER_HEADS, N_KEYS, tq), lambda i, e: (0, 0, i)),
                  pl.BlockSpec((PEER_HEADS, tq), lambda i, e: (0, i)),
                  pl.BlockSpec((tq, D), lambda i, e: (i, 0))],
        out_specs=pl.BlockSpec((tq, D), lambda i, e: (i, 0)),
        out_shape=jax.ShapeDtypeStruct((T, D), F32),
        scratch_shapes=[pltpu.VMEM((D, tq), F32), pltpu.VMEM((eb, tq), BF16)],
        compiler_params=_cparams(("parallel", "arbitrary"), 48),
        name="peer_experts",
    )(xn, u_bf16, vt_bf16, s1, s2, tau, h)


def _ple_kernel(h_ref, g_ref, p_ref, wg_ref, wu_ref, out_ref):
    h = h_ref[...]
    gate = jax.nn.sigmoid(jnp.dot(_rms(h, g_ref[...]).astype(BF16), wg_ref[...], preferred_element_type=F32))
    up = jnp.dot(p_ref[...].astype(BF16), wu_ref[...], preferred_element_type=F32)
    out_ref[...] = h + up * gate


def _ple(h, g, p, wg_bf16, wu_bf16, *, tm=512):
    T, D = h.shape
    P = p.shape[1]
    return pl.pallas_call(
        _ple_kernel,
        grid=(T // tm,),
        in_specs=[pl.BlockSpec((tm, D), lambda i: (i, 0)), pl.BlockSpec((1, D), lambda i: (0, 0)),
                  pl.BlockSpec((tm, P), lambda i: (i, 0)), pl.BlockSpec((D, D), lambda i: (0, 0)),
                  pl.BlockSpec((P, D), lambda i: (0, 0))],
        out_specs=pl.BlockSpec((tm, D), lambda i: (i, 0)),
        out_shape=jax.ShapeDtypeStruct((T, D), F32),
        compiler_params=_cparams(("parallel",), 32),
        name="ple",
    )(h, g.reshape(1, D), p, wg_bf16, wu_bf16)


def _tile_gain(g, scale=1.0):
    return jnp.tile(g.astype(F32) * scale, LANES // HEAD_DIM)


def _ab_mixer(h, rope, B, S, g, w_in, w_out, a_q, a_k, b_q, b_k):
    A, Bw = A_HEADS * HEAD_DIM, B_HEADS * HEAD_DIM
    cuts = np.cumsum([A, A, A, Bw, Bw, Bw, IDX_HEADS * HEAD_DIM, HEAD_DIM, IDX_HEADS])
    base, ki0, wi0 = w_in[:, :cuts[6]], w_in[:, cuts[6]:cuts[7]], w_in[:, cuts[7]:cuts[8]]
    w_pad = jnp.concatenate([base, ki0, ki0, wi0, jnp.zeros((D_MODEL, LANES - IDX_HEADS), F32)], axis=1).astype(BF16)
    scale = HEAD_DIM ** -0.5
    kinds = ["nr"] * 8 + ["p"] * 4 + ["nr"] * 8 + ["p"] * 4 + ["r"] * 3 + ["pa"]
    one = jnp.ones((LANES,), F32)
    gains = jnp.stack([_tile_gain(a_q, scale)] * 4 + [_tile_gain(a_k)] * 4 + [one] * 4
                      + [_tile_gain(b_q, scale)] * 4 + [_tile_gain(b_k)] * 4 + [one] * 8)
    y, aux = _project(h, g, w_pad, gains, rope, kinds)
    kmean = _block_means(y, 4, 4, B, S)
    oa = _flash("moba", y, y, y, 0, 4, 8, 4, B, S, tq=512, tk=MOBA_BLOCK, extra=kmean)
    bias = _dsa_select(y, aux, 24, 26, B, S)
    ob = _flash("dsa", y, y, y, 12, 16, 20, 4, B, S, tq=512, tk=512, extra=bias)
    return _out_proj([oa, ob], w_out.astype(BF16), h)


def _c_mixer(h, rope, B, S, g, w_in, w_out, q_norm, k_norm):
    scale = HEAD_DIM ** -0.5
    kinds = ["nr"] * 16 + ["p"] * 7 + ["pa"]
    one = jnp.ones((LANES,), F32)
    gains = jnp.stack([_tile_gain(q_norm, scale)] * 8 + [_tile_gain(k_norm)] * 8 + [one] * 8)
    y, _ = _project(h, g, w_in.astype(BF16), gains, rope, kinds)
    o = _flash("dilated", y, y, y, 0, 8, 16, 8, B, S, tq=512, tk=512)
    return _out_proj([o], w_out.astype(BF16), h)


def _peer(h, g, w_query, sub_keys, expert_u, expert_v):
    sk = sub_keys.reshape(2 * PEER_HEADS, N_KEYS, N_KEYS).astype(BF16)
    xn, s1, s2, tau = _peer_route(h, g, w_query.astype(BF16), sk)
    return _peer_experts(xn, expert_u.astype(BF16), expert_v.T.astype(BF16), s1, s2, tau, h)


def kernel(x, p, positions, attn_norm, ffn_norm, ple_norm, ab_w_in, ab_w_out, a_q_norm, a_k_norm, b_q_norm,
           b_k_norm, c_w_in, c_w_out, c_q_norm, c_k_norm, peer_w_query, peer_sub_keys, peer_u, peer_v,
           ple_w_gate, ple_w_up):
    B, S, D = x.shape
    depth = p.shape[0]
    rope = _rope_tables(positions)
    h = x.reshape(B * S, D)
    for i in range(depth):
        j = i // 2
        if i % 2 == 0:
            h = _ab_mixer(h, rope, B, S, attn_norm[i], ab_w_in[j], ab_w_out[j],
                          a_q_norm[j], a_k_norm[j], b_q_norm[j], b_k_norm[j])
        else:
            h = _c_mixer(h, rope, B, S, attn_norm[i], c_w_in[j], c_w_out[j], c_q_norm[j], c_k_norm[j])
        h = _peer(h, ffn_norm[i], peer_w_query[i], peer_sub_keys[i], peer_u[i], peer_v[i])
        h = _ple(h, ple_norm[i], p[i].reshape(B * S, -1), ple_w_gate[i].astype(BF16), ple_w_up[i].astype(BF16))
    return h.reshape(B, S, D)
```

```python
import functools
import math

import numpy as np
import jax
import jax.numpy as jnp
from jax import lax
from jax.experimental import pallas as pl
from jax.experimental.pallas import tpu as pltpu

F32 = jnp.float32
BF16 = jnp.bfloat16

D_MODEL = 1024
HEAD_DIM = 64
ROT_HALF = 8
ROPE_THETA = 500000.0
NORM_EPS = 1e-6
NEG = -1e30
LOG2E = math.log2(math.e)
LANES = 128

A_HEADS = 8
B_HEADS = 8
C_HEADS = 16
IDX_HEADS = 4
MOBA_BLOCK = 256
MOBA_TOPK = 3
DSA_TOPK = 256
DILATED_PAIRS = ((128, 1), (512, 4), (2048, 16))
PEER_HEADS = 8
N_KEYS = 128
PEER_TOPK = 16

V7X_VMEM_BYTES = 64 * 1024 * 1024


def _cparams(sem, vmem_mb):
    assert vmem_mb * 2**20 < V7X_VMEM_BYTES
    return pltpu.CompilerParams(dimension_semantics=sem, vmem_limit_bytes=vmem_mb * 2**20)


def _rms(x, g):
    ms = jnp.mean(x * x, axis=-1, keepdims=True)
    return x * lax.rsqrt(ms + NORM_EPS) * g


def _nt_dot(a, b):
    return lax.dot_general(a, b, (((1,), (1,)), ((), ())), preferred_element_type=F32)


def _split_bf16(x):
    hi = x.astype(BF16)
    lo = (x - hi.astype(F32)).astype(BF16)
    return hi, lo


def _proj_kernel(x_ref, g_ref, w_ref, gains_ref, cos_ref, sa_ref, sb_ref, y_ref, aux_ref, *, kinds, group):
    xn = _rms(x_ref[...], g_ref[...]).astype(BF16)
    row = lax.broadcasted_iota(jnp.int32, (LANES, LANES), 0) // HEAD_DIM
    col = lax.broadcasted_iota(jnp.int32, (LANES, LANES), 1) // HEAD_DIM
    head_avg = jnp.where(row == col, 1.0 / HEAD_DIM, 0.0).astype(BF16)
    n_chunks = len(kinds)
    for g0 in range(0, n_chunks, group):
        g1 = min(g0 + group, n_chunks)
        yg = jnp.dot(xn, w_ref[:, g0 * LANES:g1 * LANES], preferred_element_type=F32)
        for c in range(g0, g1):
            y = yg[:, (c - g0) * LANES:(c - g0 + 1) * LANES]
            kind = kinds[c]
            if "n" in kind:
                hi, lo = _split_bf16(y * y)
                msq = (jnp.dot(hi, head_avg, preferred_element_type=F32)
                       + jnp.dot(lo, head_avg, preferred_element_type=F32))
                y = y * lax.rsqrt(msq + NORM_EPS) * gains_ref[c:c + 1, :]
            if "r" in kind:
                y = (y * cos_ref[...] + pltpu.roll(y, LANES - ROT_HALF, 1) * sa_ref[...]
                     + pltpu.roll(y, ROT_HALF, 1) * sb_ref[...])
            if "a" in kind:
                aux_ref[...] = y
            y_ref[c] = y.astype(BF16)


def _project(h, g, w_bf16, gains, rope, kinds, *, tm=512, group=4):
    T, D = h.shape
    n_chunks = len(kinds)
    assert w_bf16.shape == (D, n_chunks * LANES) and T % tm == 0
    cos, sa, sb = rope
    tok = lambda i: (i, 0)
    fixed = lambda i: (0, 0)
    return pl.pallas_call(
        functools.partial(_proj_kernel, kinds=tuple(kinds), group=group),
        grid=(T // tm,),
        in_specs=[pl.BlockSpec((tm, D), tok), pl.BlockSpec((1, D), fixed),
                  pl.BlockSpec((D, n_chunks * LANES), fixed), pl.BlockSpec((n_chunks, LANES), fixed),
                  pl.BlockSpec((tm, LANES), tok), pl.BlockSpec((tm, LANES), tok), pl.BlockSpec((tm, LANES), tok)],
        out_specs=[pl.BlockSpec((n_chunks, tm, LANES), lambda i: (0, i, 0)), pl.BlockSpec((tm, LANES), tok)],
        out_shape=[jax.ShapeDtypeStruct((n_chunks, T, LANES), BF16), jax.ShapeDtypeStruct((T, LANES), F32)],
        compiler_params=_cparams(("parallel",), 48),
        name="proj",
    )(h, g.reshape(1, D), w_bf16, gains, cos, sa, sb)


def _rope_tables(positions):
    B, S = positions.shape
    inv = ROPE_THETA ** (-jnp.arange(ROT_HALF, dtype=F32) / ROT_HALF)
    ang = positions.astype(F32)[..., None] * inv
    cos, sin = jnp.cos(ang), jnp.sin(ang)
    rest = HEAD_DIM - 2 * ROT_HALF
    one, zero, z8 = jnp.ones((B, S, rest), F32), jnp.zeros((B, S, rest), F32), jnp.zeros((B, S, ROT_HALF), F32)
    tile = lambda parts: jnp.tile(jnp.concatenate(parts, axis=-1), (1, 1, LANES // HEAD_DIM)).reshape(B * S, LANES)
    return tile([cos, cos, one]), tile([-sin, z8, zero]), tile([z8, sin, zero])


def _kmean_kernel(k_ref, o_ref):
    n = pl.program_id(1)

    @pl.when(n == 0)
    def _():
        o_ref[...] = jnp.zeros_like(o_ref)

    o_ref[:, pl.ds(n, 1), :] = jnp.mean(k_ref[...].astype(F32), axis=1, keepdims=True)


def _block_means(y, k_off, pairs, B, S):
    nb = S // MOBA_BLOCK
    assert nb <= LANES and k_off % pairs == 0
    return pl.pallas_call(
        _kmean_kernel,
        grid=(B, nb),
        in_specs=[pl.BlockSpec((pairs, MOBA_BLOCK, LANES), lambda b, n: (k_off // pairs, b * nb + n, 0))],
        out_specs=pl.BlockSpec((pairs, LANES, LANES), lambda b, n: (0, b, 0)),
        out_shape=jax.ShapeDtypeStruct((pairs, B * LANES, LANES), F32),
        compiler_params=_cparams(("parallel", "arbitrary"), 16),
        name="moba_kmean",
    )(y)


MASK_BIG = 32768.0


def _flash_kernel(*refs, mode, pairs, tq, tk, n_steps, back):
    q_ref, k_ref, v_ref, x_ref, o_ref, m_scr, acc_scr, qa_scr = refs[:8]
    s_scr, p_scr, alpha_scr = refs[8:10], refs[10:12], refs[12:14]
    if mode == "dilated":
        b_ref = x_ref
    elif mode == "dsa":
        b_ref, bias_scr = x_ref, refs[14]
    else:
        km_ref, bias_scr, oh_scr = x_ref, refs[14], refs[15]
    i = pl.program_id(1)
    j = pl.program_id(2)
    lane = lax.broadcasted_iota(jnp.int32, (1, LANES), 1)
    halves = (lane < HEAD_DIM, lane >= HEAD_DIM)
    if mode == "dilated":
        kt = i * (tq // tk) - back + j
        active = kt >= 0
    else:
        kt = j
        active = j * tk < (i + 1) * tq
    tpos = i * tq + lax.broadcasted_iota(jnp.int32, (tq, 1), 0)

    @pl.when(j == 0)
    def _():
        m_scr[...] = jnp.full_like(m_scr, NEG)
        acc_scr[...] = jnp.zeros_like(acc_scr)
        blk = lax.broadcasted_iota(jnp.int32, (tq, LANES), 1)
        qblk = tpos // MOBA_BLOCK
        for c in range(pairs):
            if mode == "moba":
                km_hi, km_lo = _split_bf16(km_ref[c])
            for half in range(2):
                qm = jnp.where(halves[half], q_ref[c], jnp.zeros((), BF16))
                if mode == "moba":
                    gate = jnp.where(blk < qblk, _nt_dot(qm, km_hi) + _nt_dot(qm, km_lo), NEG)
                    rest, thr = gate, None
                    for _ in range(MOBA_TOPK):
                        thr = jnp.max(rest, axis=1, keepdims=True)
                        rest = jnp.where(rest >= thr, NEG, rest)
                    attended = ((gate >= thr) & (blk < qblk)) | (blk == qblk)
                    feat = jnp.where(attended, 0.0, -MASK_BIG)
                    if half == 0:
                        feat = pltpu.roll(feat, HEAD_DIM, 1)
                    qm = jnp.where(halves[half], qm, feat.astype(BF16))
                qa_scr[2 * c + half] = qm

    def attend(bias):
        n_heads = 2 * pairs
        if mode == "moba":
            kblk = (kt * tk + lax.broadcasted_iota(jnp.int32, (tk, 1), 0)) // MOBA_BLOCK
            oh_scr[0] = jnp.where(lane == kblk + HEAD_DIM, 1.0, 0.0).astype(BF16)
            oh_scr[1] = jnp.where(lane == kblk, 1.0, 0.0).astype(BF16)

        def scores(h, par):
            kp = k_ref[h // 2]
            if mode == "moba":
                kp = jnp.where(halves[par], kp, oh_scr[par])
            s_scr[par][...] = _nt_dot(qa_scr[h], kp)

        def softmax(h, par):
            s = s_scr[par][...]
            if bias is not None:
                s = s + bias[...]
            m_old = m_scr[h]
            m_new = jnp.maximum(m_old, jnp.max(s, axis=1, keepdims=True))
            p_scr[par][...] = jnp.exp2(s - m_new).astype(BF16)
            alpha_scr[par][...] = jnp.exp2(m_old - m_new)
            m_scr[h] = m_new

        def values(h, par):
            va = jnp.where(halves[par], v_ref[h // 2], jnp.ones((), BF16))
            acc_scr[h] = alpha_scr[par][...] * acc_scr[h] + jnp.dot(p_scr[par][...], va, preferred_element_type=F32)

        scores(0, 0)
        scores(1, 1)
        softmax(0, 0)

        def body(c, carry):
            scores(2 * c + 2, 0)
            softmax(2 * c + 1, 1)
            values(2 * c, 0)
            scores(2 * c + 3, 1)
            softmax(2 * c + 2, 0)
            values(2 * c + 1, 1)
            return carry

        lax.fori_loop(0, pairs - 1, body, 0)
        softmax(n_heads - 1, 1)
        values(n_heads - 2, 0)
        values(n_heads - 1, 1)

    if mode == "moba":
        on_diagonal = (kt + 1) * tk > i * tq

        @pl.when(active & on_diagonal)
        def _():
            kpos = kt * tk + lax.broadcasted_iota(jnp.int32, (tq, tk), 1)
            bias_scr[...] = jnp.where(kpos <= tpos, 0.0, -jnp.inf)
            attend(bias_scr)

        @pl.when(active & jnp.logical_not(on_diagonal))
        def _():
            attend(None)
    elif mode == "dsa":
        @pl.when(active)
        def _():
            bias_scr[...] = b_ref[0, 0].astype(F32)
            attend(bias_scr)
    else:
        @pl.when(active)
        def _():
            attend(b_ref.at[0])

    @pl.when(j == n_steps - 1)
    def _():
        for c in range(pairs):
            a0, a1 = acc_scr[2 * c], acc_scr[2 * c + 1]
            num = jnp.where(halves[0], a0, a1)
            den = pltpu.roll(jnp.where(halves[0], a1, a0), HEAD_DIM, 1)
            o_ref[c] = (num / den).astype(BF16)


def _dilated_bias_table(tq, tk, back, n_steps):
    r = np.arange(tq)[:, None]
    c = np.arange(tk)[None, :]
    tabs = []
    for j in range(n_steps):
        d = (back - j) * tk + r - c
        mult = sum(((d >= 0) & (d <= w) & (d % dil == 0)).astype(np.float64) for w, dil in DILATED_PAIRS)
        tabs.append(np.where(mult > 0, np.log2(np.maximum(mult, 1.0)), -np.inf))
    return jnp.asarray(np.stack(tabs), F32)


def _flash(mode, q_arr, k_arr, v_arr, q_off, k_off, v_off, pairs, B, S, *, tq, tk, extra=None):
    nq, nk = S // tq, S // tk
    assert S % tq == 0 and S % tk == 0 and tq % tk == 0
    assert q_off % pairs == 0 and k_off % pairs == 0 and v_off % pairs == 0
    if mode == "dilated":
        max_back = max(w for w, _ in DILATED_PAIRS)
        back = -(-max_back // tk)
        n_steps = back + tq // tk
        kv_tile = lambda i, j: jnp.maximum(i * (tq // tk) - back + j, 0)
    else:
        back = 0
        n_steps = nk
        kv_tile = lambda i, j: jnp.minimum(j, ((i + 1) * tq - 1) // tk)
    in_specs = [pl.BlockSpec((pairs, tq, LANES), lambda b, i, j: (q_off // pairs, b * nq + i, 0)),
                pl.BlockSpec((pairs, tk, LANES), lambda b, i, j: (k_off // pairs, b * nk + kv_tile(i, j), 0)),
                pl.BlockSpec((pairs, tk, LANES), lambda b, i, j: (v_off // pairs, b * nk + kv_tile(i, j), 0))]
    args = [q_arr, k_arr, v_arr]
    scratch = [pltpu.VMEM((2 * pairs, tq, 1), F32), pltpu.VMEM((2 * pairs, tq, LANES), F32),
               pltpu.VMEM((2 * pairs, tq, LANES), BF16), pltpu.VMEM((tq, tk), F32), pltpu.VMEM((tq, tk), F32),
               pltpu.VMEM((tq, tk), BF16), pltpu.VMEM((tq, tk), BF16),
               pltpu.VMEM((tq, 1), F32), pltpu.VMEM((tq, 1), F32)]
    if mode == "moba":
        assert tk % MOBA_BLOCK == 0 and S // MOBA_BLOCK <= HEAD_DIM
        in_specs.append(pl.BlockSpec((pairs, LANES, LANES), lambda b, i, j: (0, b, 0)))
        args.append(extra)
        scratch += [pltpu.VMEM((tq, tk), F32), pltpu.VMEM((2, tk, LANES), BF16)]
    elif mode == "dsa":
        in_specs.append(pl.BlockSpec((1, 1, tq, tk), lambda b, i, j: (b, kv_tile(i, j), i, 0)))
        args.append(extra)
        scratch.append(pltpu.VMEM((tq, tk), F32))
    else:
        in_specs.append(pl.BlockSpec((1, tq, tk), lambda b, i, j: (j, 0, 0)))
        args.append(_dilated_bias_table(tq, tk, back, n_steps))
    return pl.pallas_call(
        functools.partial(_flash_kernel, mode=mode, pairs=pairs, tq=tq, tk=tk, n_steps=n_steps, back=back),
        grid=(B, nq, n_steps),
        in_specs=in_specs,
        out_specs=pl.BlockSpec((pairs, tq, LANES), lambda b, i, j: (0, b * nq + i, 0)),
        out_shape=jax.ShapeDtypeStruct((pairs, B * S, LANES), BF16),
        scratch_shapes=scratch,
        compiler_params=_cparams(("parallel", "parallel", "arbitrary"), 48),
        name="flash_" + mode,
    )(*args)


def _dsa_select_strip(key_scr, o_ref, tpos, n_live, *, r0, rs, ck, topk, idx_bits):
    def count(pred):
        def body(c, acc):
            hit = jnp.where(pred(key_scr[c, r0:r0 + rs, :], c), 1.0, 0.0)
            for s in range(ck // LANES):
                acc = acc + hit[:, s * LANES:(s + 1) * LANES]
            return acc
        acc = lax.fori_loop(0, n_live, body, jnp.zeros((rs, LANES), F32))
        return jnp.sum(acc, axis=1, keepdims=True)

    def value_bit(it, lo):
        cand = lo + jnp.left_shift(jnp.int32(1), 31 - it)
        return jnp.where(count(lambda kc, c: kc >= cand) >= topk, cand, lo)

    thr = lax.fori_loop(0, 32, value_bit, jnp.full((rs, 1), -2**31, jnp.int32))
    need = topk - count(lambda kc, c: kc > thr)
    n_equal = count(lambda kc, c: kc == thr)

    def index_bit(it, lo):
        cand = lo + jnp.left_shift(jnp.int32(1), idx_bits - 1 - it)
        def pred(kc, c):
            kpos = c * ck + lax.broadcasted_iota(jnp.int32, (rs, ck), 1)
            return (kc == thr) & (kpos < cand)
        return jnp.where(count(pred) < need, cand, lo)

    last = lax.cond(jnp.max(n_equal - need) > 0.0,
                    lambda: lax.fori_loop(0, idx_bits, index_bit, jnp.zeros((rs, 1), jnp.int32)),
                    lambda: jnp.full((rs, 1), 2**idx_bits, jnp.int32))

    def emit(c, carry):
        kc = key_scr[c, r0:r0 + rs, :]
        kpos = c * ck + lax.broadcasted_iota(jnp.int32, (rs, ck), 1)
        take = ((kc > thr) | ((kc == thr) & (kpos <= last))) & (kpos <= tpos)
        o_ref[0, c, r0:r0 + rs, :] = jnp.where(take, 0.0, -jnp.inf).astype(BF16)
        return carry

    lax.fori_loop(0, n_live, emit, 0)


def _dsa_select_kernel(qi_ref, ki_ref, w_ref, o_ref, key_scr, *, tq, ck, rs, n_chunks, topk, idx_bits):
    i = pl.program_id(1)
    lane = lax.broadcasted_iota(jnp.int32, (1, LANES), 1)
    halves = (lane < HEAD_DIM, lane >= HEAD_DIM)
    tpos = i * tq + lax.broadcasted_iota(jnp.int32, (tq, 1), 0)
    n_live = ((i + 1) * tq + ck - 1) // ck
    w = w_ref[...]

    def score_chunk(c, carry):
        kc = ki_ref[0, pl.ds(pl.multiple_of(c * ck, ck), ck), :]
        score = jnp.zeros((tq, ck), F32)
        for h in range(IDX_HEADS):
            qm = jnp.where(halves[h % 2], qi_ref[h // 2], jnp.zeros((), BF16))
            score = score + w[:, h:h + 1] * jnp.maximum(_nt_dot(qm, kc), 0.0)
        kpos = c * ck + lax.broadcasted_iota(jnp.int32, (tq, ck), 1)
        score = jnp.where(kpos <= tpos, score + 0.0, NEG)
        bits = pltpu.bitcast(score, jnp.int32)
        key_scr[c] = jnp.where(bits < 0, bits ^ jnp.int32(0x7FFFFFFF), bits)
        return carry

    lax.fori_loop(0, n_live, score_chunk, 0)

    for r0 in range(0, tq, rs):
        _dsa_select_strip(key_scr, o_ref, tpos[r0:r0 + rs], n_live, r0=r0, rs=rs, ck=ck, topk=topk, idx_bits=idx_bits)

    def blank(c, carry):
        o_ref[0, c] = jnp.full((tq, ck), -jnp.inf, BF16)
        return carry

    lax.fori_loop(n_live, n_chunks, blank, 0)


def _dsa_select(y, aux, qi_off, ki_off, B, S, *, tq=256, ck=512, rs=128):
    n_chunks = S // ck
    idx_bits = int(math.log2(S))
    assert 2**idx_bits == S and S % ck == 0 and S % tq == 0 and tq % rs == 0 and qi_off % 2 == 0
    nq = S // tq
    topk = min(DSA_TOPK, S // 4)
    return pl.pallas_call(
        functools.partial(_dsa_select_kernel, tq=tq, ck=ck, rs=rs, n_chunks=n_chunks, topk=topk, idx_bits=idx_bits),
        grid=(B, nq),
        in_specs=[pl.BlockSpec((2, tq, LANES), lambda b, i: (qi_off // 2, b * nq + i, 0)),
                  pl.BlockSpec((1, S, LANES), lambda b, i: (ki_off, b, 0)),
                  pl.BlockSpec((tq, LANES), lambda b, i: (b * nq + i, 0))],
        out_specs=pl.BlockSpec((1, n_chunks, tq, ck), lambda b, i: (b, 0, i, 0)),
        out_shape=jax.ShapeDtypeStruct((B, n_chunks, S, ck), BF16),
        scratch_shapes=[pltpu.VMEM((n_chunks, tq, ck), jnp.int32)],
        compiler_params=_cparams(("parallel", "parallel"), 48),
        name="dsa_select",
    )(y, y, aux)


def _out_proj_kernel(*refs, n_in):
    o_refs, w_ref, h_ref, out_ref = refs[:n_in], refs[n_in], refs[n_in + 1], refs[n_in + 2]
    slabs = [o[c] for o in o_refs for c in range(o.shape[0])]
    o = jnp.concatenate(slabs, axis=-1)
    out_ref[...] = h_ref[...] + jnp.dot(o, w_ref[...], preferred_element_type=F32)


def _out_proj(os_, w_bf16, h, *, tm=512):
    T, D = h.shape
    width = sum(o.shape[0] for o in os_) * LANES
    assert w_bf16.shape == (width, D)
    in_specs = [pl.BlockSpec((o.shape[0], tm, LANES), lambda i: (0, i, 0)) for o in os_]
    in_specs += [pl.BlockSpec((width, D), lambda i: (0, 0)), pl.BlockSpec((tm, D), lambda i: (i, 0))]
    return pl.pallas_call(
        functools.partial(_out_proj_kernel, n_in=len(os_)),
        grid=(T // tm,),
        in_specs=in_specs,
        out_specs=pl.BlockSpec((tm, D), lambda i: (i, 0)),
        out_shape=jax.ShapeDtypeStruct((T, D), F32),
        compiler_params=_cparams(("parallel",), 32),
        name="out_proj",
    )(*os_, w_bf16, h)


def _top_values(s, k):
    out = []
    for _ in range(k):
        cur = jnp.max(s, axis=0, keepdims=True)
        out.append(cur)
        s = jnp.where(s >= cur, -jnp.inf, s)
    return jnp.concatenate(out, axis=0)


def _pair_candidates(t1, t2):
    k = PEER_TOPK
    parts = [t1[0:1] + t2]
    parts += [t1[a:a + 1] + t2[0:k // 2] for a in range(1, k // 2)]
    parts.append(t1[k // 2:k] + t2[0:1])
    return jnp.concatenate(parts, axis=0)


def _kth_largest(c, k):
    left = jnp.full((1, c.shape[1]), float(k), F32)
    tau = jnp.zeros((1, c.shape[1]), F32)
    for _ in range(k):
        cur = jnp.max(c, axis=0, keepdims=True)
        hit = c >= cur
        tau = jnp.where(left > 0.0, cur, tau)
        left = left - jnp.sum(jnp.where(hit, 1.0, 0.0), axis=0, keepdims=True)
        c = jnp.where(hit, -jnp.inf, c)
    return tau


def _peer_route_kernel(h_ref, g_ref, wq_ref, sk_ref, xn_ref, s1_ref, s2_ref, tau_ref):
    xn = _rms(h_ref[...], g_ref[...]).astype(BF16)
    xn_ref[...] = xn
    taus = []
    for hd in range(PEER_HEADS):
        q = jnp.dot(xn, wq_ref[:, hd * 2 * N_KEYS:(hd + 1) * 2 * N_KEYS], preferred_element_type=F32).astype(BF16)
        s1 = _nt_dot(sk_ref[2 * hd], q[:, :N_KEYS])
        s2 = _nt_dot(sk_ref[2 * hd + 1], q[:, N_KEYS:])
        s1 = s1 - jnp.max(s1, axis=0, keepdims=True)
        s2 = s2 - jnp.max(s2, axis=0, keepdims=True)
        t1, t2 = _top_values(s1, PEER_TOPK), _top_values(s2, PEER_TOPK)
        cand = _pair_candidates(t1, t2)
        tau0 = _kth_largest(cand, PEER_TOPK)
        log_z = jnp.log(jnp.sum(jnp.where(cand >= tau0, jnp.exp(cand), 0.0), axis=0, keepdims=True))
        taus.append(_kth_largest(_pair_candidates((t1 - log_z) * LOG2E, t2 * LOG2E), PEER_TOPK))
        s1_ref[hd] = (s1 - log_z) * LOG2E
        s2_ref[hd] = s2 * LOG2E
    tau_ref[...] = jnp.concatenate(taus, axis=0)


def _peer_route(h, g, wq_bf16, sk_bf16, *, tm=256):
    T, D = h.shape
    nq = PEER_HEADS * 2 * N_KEYS
    assert wq_bf16.shape == (D, nq) and sk_bf16.shape == (2 * PEER_HEADS, N_KEYS, N_KEYS)
    return pl.pallas_call(
        _peer_route_kernel,
        grid=(T // tm,),
        in_specs=[pl.BlockSpec((tm, D), lambda i: (i, 0)), pl.BlockSpec((1, D), lambda i: (0, 0)),
                  pl.BlockSpec((D, nq), lambda i: (0, 0)),
                  pl.BlockSpec((2 * PEER_HEADS, N_KEYS, N_KEYS), lambda i: (0, 0, 0))],
        out_specs=[pl.BlockSpec((tm, D), lambda i: (i, 0)),
                   pl.BlockSpec((PEER_HEADS, N_KEYS, tm), lambda i: (0, 0, i)),
                   pl.BlockSpec((PEER_HEADS, N_KEYS, tm), lambda i: (0, 0, i)),
                   pl.BlockSpec((PEER_HEADS, tm), lambda i: (0, i))],
        out_shape=[jax.ShapeDtypeStruct((T, D), BF16), jax.ShapeDtypeStruct((PEER_HEADS, N_KEYS, T), F32),
                   jax.ShapeDtypeStruct((PEER_HEADS, N_KEYS, T), F32), jax.ShapeDtypeStruct((PEER_HEADS, T), F32)],
        compiler_params=_cparams(("parallel",), 40),
        name="peer_route",
    )(h, g.reshape(1, D), wq_bf16, sk_bf16)


def _gelu_tanh(x):
    return 0.5 * x * (1.0 + jnp.tanh(math.sqrt(2.0 / math.pi) * (x + 0.044715 * (x * x * x))))


def _peer_expert_kernel(xn_ref, u_ref, vt_ref, s1_ref, s2_ref, tau_ref, h_ref, out_ref, acc_scr, a_scr, *, eb, tq):
    e = pl.program_id(1)
    rows = eb // N_KEYS

    @pl.when(e == 0)
    def _():
        acc_scr[...] = jnp.zeros_like(acc_scr)

    act = _gelu_tanh(_nt_dot(u_ref[...], xn_ref[...]))
    row0 = pl.multiple_of(e * rows, rows)
    for r in range(rows):
        for t0 in range(0, tq, LANES):
            w = jnp.zeros((N_KEYS, LANES), F32)
            for hd in range(PEER_HEADS):
                s1_rows = s1_ref[hd, pl.ds(row0, rows), t0:t0 + LANES]
                c = s2_ref[hd, :, t0:t0 + LANES] + s1_rows[r:r + 1, :]
                w = w + jnp.where(c >= tau_ref[hd:hd + 1, t0:t0 + LANES], jnp.exp2(c), 0.0)
            a_scr[r * N_KEYS:(r + 1) * N_KEYS, t0:t0 + LANES] = (
                act[r * N_KEYS:(r + 1) * N_KEYS, t0:t0 + LANES] * w).astype(BF16)
    acc_scr[...] += jnp.dot(vt_ref[...], a_scr[...], preferred_element_type=F32)

    @pl.when(e == pl.num_programs(1) - 1)
    def _():
        out_ref[...] = h_ref[...] + acc_scr[...].T


def _peer_experts(xn, u_bf16, vt_bf16, s1, s2, tau, h, *, tq=512, eb=2048):
    T, D = h.shape
    E = u_bf16.shape[0]
    assert E == N_KEYS * N_KEYS and E % eb == 0 and eb % (8 * N_KEYS) == 0 and T % tq == 0
    return pl.pallas_call(
        functools.partial(_peer_expert_kernel, eb=eb, tq=tq),
        grid=(T // tq, E // eb),
        in_specs=[pl.BlockSpec((tq, D), lambda i, e: (i, 0)),
                  pl.BlockSpec((eb, D), lambda i, e: (e, 0)),
                  pl.BlockSpec((D, eb), lambda i, e: (0, e)),
                  pl.BlockSpec((PEER_HEADS, N_KEYS, tq), lambda i, e: (0, 0, i)),
                  pl.BlockSpec((PEER_HEADS, N_KEYS, tq), lambda i, e: (0, 0, i)),
                  pl.BlockSpec((PEER_HEADS, tq), lambda i, e: (0, i)),
                  pl.BlockSpec((tq, D), lambda i, e: (i, 0))],
        out_specs=pl.BlockSpec((tq, D), lambda i, e: (i, 0)),
        out_shape=jax.ShapeDtypeStruct((T, D), F32),
        scratch_shapes=[pltpu.VMEM((D, tq), F32), pltpu.VMEM((eb, tq), BF16)],
        compiler_params=_cparams(("parallel", "arbitrary"), 56),
        name="peer_experts",
    )(xn, u_bf16, vt_bf16, s1, s2, tau, h)


def _ple_kernel(h_ref, g_ref, p_ref, wg_ref, wu_ref, out_ref):
    h = h_ref[...]
    gate = jax.nn.sigmoid(jnp.dot(_rms(h, g_ref[...]).astype(BF16), wg_ref[...], preferred_element_type=F32))
    up = jnp.dot(p_ref[...].astype(BF16), wu_ref[...], preferred_element_type=F32)
    out_ref[...] = h + up * gate


def _ple(h, g, p, wg_bf16, wu_bf16, *, tm=512):
    T, D = h.shape
    P = p.shape[1]
    return pl.pallas_call(
        _ple_kernel,
        grid=(T // tm,),
        in_specs=[pl.BlockSpec((tm, D), lambda i: (i, 0)), pl.BlockSpec((1, D), lambda i: (0, 0)),
                  pl.BlockSpec((tm, P), lambda i: (i, 0)), pl.BlockSpec((D, D), lambda i: (0, 0)),
                  pl.BlockSpec((P, D), lambda i: (0, 0))],
        out_specs=pl.BlockSpec((tm, D), lambda i: (i, 0)),
        out_shape=jax.ShapeDtypeStruct((T, D), F32),
        compiler_params=_cparams(("parallel",), 32),
        name="ple",
    )(h, g.reshape(1, D), p, wg_bf16, wu_bf16)


def _tile_gain(g, scale=1.0):
    return jnp.tile(g.astype(F32) * scale, LANES // HEAD_DIM)


def _ab_mixer(h, rope, B, S, g, w_in, w_out, a_q, a_k, b_q, b_k):
    A, Bw = A_HEADS * HEAD_DIM, B_HEADS * HEAD_DIM
    cuts = np.cumsum([A, A, A, Bw, Bw, Bw, IDX_HEADS * HEAD_DIM, HEAD_DIM, IDX_HEADS])
    base, ki0, wi0 = w_in[:, :cuts[6]], w_in[:, cuts[6]:cuts[7]], w_in[:, cuts[7]:cuts[8]]
    w_pad = jnp.concatenate([base, ki0, ki0, wi0, jnp.zeros((D_MODEL, LANES - IDX_HEADS), F32)], axis=1).astype(BF16)
    scale = HEAD_DIM ** -0.5 * LOG2E
    kinds = ["nr"] * 8 + ["p"] * 4 + ["nr"] * 8 + ["p"] * 4 + ["r"] * 3 + ["pa"]
    one = jnp.ones((LANES,), F32)
    gains = jnp.stack([_tile_gain(a_q, scale)] * 4 + [_tile_gain(a_k)] * 4 + [one] * 4
                      + [_tile_gain(b_q, scale)] * 4 + [_tile_gain(b_k)] * 4 + [one] * 8)
    y, aux = _project(h, g, w_pad, gains, rope, kinds)
    kmean = _block_means(y, 4, 4, B, S)
    oa = _flash("moba", y, y, y, 0, 4, 8, 4, B, S, tq=512, tk=512, extra=kmean)
    bias = _dsa_select(y, aux, 24, 26, B, S)
    ob = _flash("dsa", y, y, y, 12, 16, 20, 4, B, S, tq=512, tk=512, extra=bias)
    return _out_proj([oa, ob], w_out.astype(BF16), h)


def _c_mixer(h, rope, B, S, g, w_in, w_out, q_norm, k_norm):
    scale = HEAD_DIM ** -0.5 * LOG2E
    kinds = ["nr"] * 16 + ["p"] * 7 + ["pa"]
    one = jnp.ones((LANES,), F32)
    gains = jnp.stack([_tile_gain(q_norm, scale)] * 8 + [_tile_gain(k_norm)] * 8 + [one] * 8)
    y, _ = _project(h, g, w_in.astype(BF16), gains, rope, kinds)
    o = _flash("dilated", y, y, y, 0, 8, 16, 8, B, S, tq=512, tk=512)
    return _out_proj([o], w_out.astype(BF16), h)


def _peer(h, g, w_query, sub_keys, expert_u, expert_v):
    sk = sub_keys.reshape(2 * PEER_HEADS, N_KEYS, N_KEYS).astype(BF16)
    xn, s1, s2, tau = _peer_route(h, g, w_query.astype(BF16), sk)
    return _peer_experts(xn, expert_u.astype(BF16), expert_v.T.astype(BF16), s1, s2, tau, h)


def kernel(x, p, positions, attn_norm, ffn_norm, ple_norm, ab_w_in, ab_w_out, a_q_norm, a_k_norm, b_q_norm,
           b_k_norm, c_w_in, c_w_out, c_q_norm, c_k_norm, peer_w_query, peer_sub_keys, peer_u, peer_v,
           ple_w_gate, ple_w_up):
    B, S, D = x.shape
    depth = p.shape[0]
    rope = _rope_tables(positions)
    h = x.reshape(B * S, D)
    for i in range(depth):
        j = i // 2
        if i % 2 == 0:
            h = _ab_mixer(h, rope, B, S, attn_norm[i], ab_w_in[j], ab_w_out[j],
                          a_q_norm[j], a_k_norm[j], b_q_norm[j], b_k_norm[j])
        else:
            h = _c_mixer(h, rope, B, S, attn_norm[i], c_w_in[j], c_w_out[j], c_q_norm[j], c_k_norm[j])
        h = _peer(h, ffn_norm[i], peer_w_query[i], peer_sub_keys[i], peer_u[i], peer_v[i])
        h = _ple(h, ple_norm[i], p[i].reshape(B * S, -1), ple_w_gate[i].astype(BF16), ple_w_up[i].astype(BF16))
    return h.reshape(B, S, D)
```

```python
import functools
import math

import numpy as np
import jax
import jax.numpy as jnp
from jax import lax
from jax.experimental import pallas as pl
from jax.experimental.pallas import tpu as pltpu

F32 = jnp.float32
BF16 = jnp.bfloat16

D_MODEL = 1024
HEAD_DIM = 64
ROT_HALF = 8
ROPE_THETA = 500000.0
NORM_EPS = 1e-6
NEG = -1e30
LOG2E = math.log2(math.e)
LANES = 128

A_HEADS = 8
B_HEADS = 8
C_HEADS = 16
IDX_HEADS = 4
MOBA_BLOCK = 256
MOBA_TOPK = 3
DSA_TOPK = 256
DILATED_PAIRS = ((128, 1), (512, 4), (2048, 16))
PEER_HEADS = 8
N_KEYS = 128
PEER_TOPK = 16

V7X_VMEM_BYTES = 64 * 1024 * 1024


def _cparams(sem, vmem_mb):
    assert vmem_mb * 2**20 < V7X_VMEM_BYTES
    return pltpu.CompilerParams(dimension_semantics=sem, vmem_limit_bytes=vmem_mb * 2**20)


def _rms(x, g):
    ms = jnp.mean(x * x, axis=-1, keepdims=True)
    return x * lax.rsqrt(ms + NORM_EPS) * g


def _nt_dot(a, b):
    return lax.dot_general(a, b, (((1,), (1,)), ((), ())), preferred_element_type=F32)


def _split_bf16(x):
    hi = x.astype(BF16)
    lo = (x - hi.astype(F32)).astype(BF16)
    return hi, lo


def _proj_kernel(x_ref, g_ref, w_ref, gains_ref, cos_ref, sa_ref, sb_ref, y_ref, aux_ref, *, kinds, group):
    xn = _rms(x_ref[...], g_ref[...]).astype(BF16)
    row = lax.broadcasted_iota(jnp.int32, (LANES, LANES), 0) // HEAD_DIM
    col = lax.broadcasted_iota(jnp.int32, (LANES, LANES), 1) // HEAD_DIM
    head_avg = jnp.where(row == col, 1.0 / HEAD_DIM, 0.0).astype(BF16)
    n_chunks = len(kinds)
    for g0 in range(0, n_chunks, group):
        g1 = min(g0 + group, n_chunks)
        yg = jnp.dot(xn, w_ref[:, g0 * LANES:g1 * LANES], preferred_element_type=F32)
        for c in range(g0, g1):
            y = yg[:, (c - g0) * LANES:(c - g0 + 1) * LANES]
            kind = kinds[c]
            if "n" in kind:
                hi, lo = _split_bf16(y * y)
                msq = (jnp.dot(hi, head_avg, preferred_element_type=F32)
                       + jnp.dot(lo, head_avg, preferred_element_type=F32))
                y = y * lax.rsqrt(msq + NORM_EPS) * gains_ref[c:c + 1, :]
            if "r" in kind:
                y = (y * cos_ref[...] + pltpu.roll(y, LANES - ROT_HALF, 1) * sa_ref[...]
                     + pltpu.roll(y, ROT_HALF, 1) * sb_ref[...])
            if "a" in kind:
                aux_ref[...] = y
            y_ref[c] = y.astype(BF16)


def _project(h, g, w_bf16, gains, rope, kinds, *, tm=512, group=4):
    T, D = h.shape
    n_chunks = len(kinds)
    assert w_bf16.shape == (D, n_chunks * LANES) and T % tm == 0
    cos, sa, sb = rope
    tok = lambda i: (i, 0)
    fixed = lambda i: (0, 0)
    return pl.pallas_call(
        functools.partial(_proj_kernel, kinds=tuple(kinds), group=group),
        grid=(T // tm,),
        in_specs=[pl.BlockSpec((tm, D), tok), pl.BlockSpec((1, D), fixed),
                  pl.BlockSpec((D, n_chunks * LANES), fixed), pl.BlockSpec((n_chunks, LANES), fixed),
                  pl.BlockSpec((tm, LANES), tok), pl.BlockSpec((tm, LANES), tok), pl.BlockSpec((tm, LANES), tok)],
        out_specs=[pl.BlockSpec((n_chunks, tm, LANES), lambda i: (0, i, 0)), pl.BlockSpec((tm, LANES), tok)],
        out_shape=[jax.ShapeDtypeStruct((n_chunks, T, LANES), BF16), jax.ShapeDtypeStruct((T, LANES), F32)],
        compiler_params=_cparams(("parallel",), 48),
        name="proj",
    )(h, g.reshape(1, D), w_bf16, gains, cos, sa, sb)


def _rope_tables(positions):
    B, S = positions.shape
    inv = ROPE_THETA ** (-jnp.arange(ROT_HALF, dtype=F32) / ROT_HALF)
    ang = positions.astype(F32)[..., None] * inv
    cos, sin = jnp.cos(ang), jnp.sin(ang)
    rest = HEAD_DIM - 2 * ROT_HALF
    one, zero, z8 = jnp.ones((B, S, rest), F32), jnp.zeros((B, S, rest), F32), jnp.zeros((B, S, ROT_HALF), F32)
    tile = lambda parts: jnp.tile(jnp.concatenate(parts, axis=-1), (1, 1, LANES // HEAD_DIM)).reshape(B * S, LANES)
    return tile([cos, cos, one]), tile([-sin, z8, zero]), tile([z8, sin, zero])


def _kmean_kernel(k_ref, o_ref):
    n = pl.program_id(1)

    @pl.when(n == 0)
    def _():
        o_ref[...] = jnp.zeros_like(o_ref)

    o_ref[:, pl.ds(n, 1), :] = jnp.mean(k_ref[...].astype(F32), axis=1, keepdims=True)


def _block_means(y, k_off, pairs, B, S):
    nb = S // MOBA_BLOCK
    assert nb <= LANES and k_off % pairs == 0
    return pl.pallas_call(
        _kmean_kernel,
        grid=(B, nb),
        in_specs=[pl.BlockSpec((pairs, MOBA_BLOCK, LANES), lambda b, n: (k_off // pairs, b * nb + n, 0))],
        out_specs=pl.BlockSpec((pairs, LANES, LANES), lambda b, n: (0, b, 0)),
        out_shape=jax.ShapeDtypeStruct((pairs, B * LANES, LANES), F32),
        compiler_params=_cparams(("parallel", "arbitrary"), 16),
        name="moba_kmean",
    )(y)


MASK_BIG = 32768.0


def _flash_kernel(*refs, mode, pairs, tq, tk, n_steps, back):
    q_ref, k_ref, v_ref, x_ref, o_ref, m_scr, acc_scr, qa_scr = refs[:8]
    s_scr, p_scr, alpha_scr = refs[8:10], refs[10:12], refs[12:14]
    if mode == "dilated":
        b_ref = x_ref
    elif mode == "dsa":
        b_ref, bias_scr = x_ref, refs[14]
    else:
        km_ref, bias_scr, oh_scr = x_ref, refs[14], refs[15]
    i = pl.program_id(1)
    j = pl.program_id(2)
    lane = lax.broadcasted_iota(jnp.int32, (1, LANES), 1)
    halves = (lane < HEAD_DIM, lane >= HEAD_DIM)
    if mode == "dilated":
        kt = i * (tq // tk) - back + j
        active = kt >= 0
    else:
        kt = j
        active = j * tk < (i + 1) * tq
    tpos = i * tq + lax.broadcasted_iota(jnp.int32, (tq, 1), 0)

    @pl.when(j == 0)
    def _():
        m_scr[...] = jnp.full_like(m_scr, NEG)
        acc_scr[...] = jnp.zeros_like(acc_scr)
        blk = lax.broadcasted_iota(jnp.int32, (tq, LANES), 1)
        qblk = tpos // MOBA_BLOCK
        for c in range(pairs):
            if mode == "moba":
                km_hi, km_lo = _split_bf16(km_ref[c])
            for half in range(2):
                qm = jnp.where(halves[half], q_ref[c], jnp.zeros((), BF16))
                if mode == "moba":
                    gate = jnp.where(blk < qblk, _nt_dot(qm, km_hi) + _nt_dot(qm, km_lo), NEG)
                    rest, thr = gate, None
                    for _ in range(MOBA_TOPK):
                        thr = jnp.max(rest, axis=1, keepdims=True)
                        rest = jnp.where(rest >= thr, NEG, rest)
                    attended = ((gate >= thr) & (blk < qblk)) | (blk == qblk)
                    feat = jnp.where(attended, 0.0, -MASK_BIG)
                    if half == 0:
                        feat = pltpu.roll(feat, HEAD_DIM, 1)
                    qm = jnp.where(halves[half], qm, feat.astype(BF16))
                qa_scr[2 * c + half] = qm

    def attend(bias):
        n_heads = 2 * pairs
        if mode == "moba":
            kblk = (kt * tk + lax.broadcasted_iota(jnp.int32, (tk, 1), 0)) // MOBA_BLOCK
            oh_scr[0] = jnp.where(lane == kblk + HEAD_DIM, 1.0, 0.0).astype(BF16)
            oh_scr[1] = jnp.where(lane == kblk, 1.0, 0.0).astype(BF16)

        def scores(h, par):
            kp = k_ref[h // 2]
            if mode == "moba":
                kp = jnp.where(halves[par], kp, oh_scr[par])
            s_scr[par][...] = _nt_dot(qa_scr[h], kp)

        def softmax(h, par):
            s = s_scr[par][...]
            if bias is not None:
                s = s + bias[...]
            m_old = m_scr[h]
            m_new = jnp.maximum(m_old, jnp.max(s, axis=1, keepdims=True))
            p_scr[par][...] = jnp.exp2(s - m_new).astype(BF16)
            alpha_scr[par][...] = jnp.exp2(m_old - m_new)
            m_scr[h] = m_new

        def values(h, par):
            va = jnp.where(halves[par], v_ref[h // 2], jnp.ones((), BF16))
            acc_scr[h] = alpha_scr[par][...] * acc_scr[h] + jnp.dot(p_scr[par][...], va, preferred_element_type=F32)

        scores(0, 0)
        scores(1, 1)
        softmax(0, 0)

        def body(c, carry):
            scores(2 * c + 2, 0)
            softmax(2 * c + 1, 1)
            values(2 * c, 0)
            scores(2 * c + 3, 1)
            softmax(2 * c + 2, 0)
            values(2 * c + 1, 1)
            return carry

        lax.fori_loop(0, pairs - 1, body, 0)
        softmax(n_heads - 1, 1)
        values(n_heads - 2, 0)
        values(n_heads - 1, 1)

    if mode == "moba":
        on_diagonal = (kt + 1) * tk > i * tq

        @pl.when(active & on_diagonal)
        def _():
            kpos = kt * tk + lax.broadcasted_iota(jnp.int32, (tq, tk), 1)
            bias_scr[...] = jnp.where(kpos <= tpos, 0.0, -jnp.inf)
            attend(bias_scr)

        @pl.when(active & jnp.logical_not(on_diagonal))
        def _():
            attend(None)
    elif mode == "dsa":
        @pl.when(active)
        def _():
            bias_scr[...] = b_ref[0, 0].astype(F32)
            attend(bias_scr)
    else:
        @pl.when(active)
        def _():
            attend(b_ref.at[0])

    @pl.when(j == n_steps - 1)
    def _():
        for c in range(pairs):
            a0, a1 = acc_scr[2 * c], acc_scr[2 * c + 1]
            num = jnp.where(halves[0], a0, a1)
            den = pltpu.roll(jnp.where(halves[0], a1, a0), HEAD_DIM, 1)
            o_ref[c] = (num / den).astype(BF16)


def _dilated_bias_table(tq, tk, back, n_steps):
    r = np.arange(tq)[:, None]
    c = np.arange(tk)[None, :]
    tabs = []
    for j in range(n_steps):
        d = (back - j) * tk + r - c
        mult = sum(((d >= 0) & (d <= w) & (d % dil == 0)).astype(np.float64) for w, dil in DILATED_PAIRS)
        tabs.append(np.where(mult > 0, np.log2(np.maximum(mult, 1.0)), -np.inf))
    return jnp.asarray(np.stack(tabs), F32)


def _flash(mode, q_arr, k_arr, v_arr, q_off, k_off, v_off, pairs, B, S, *, tq, tk, extra=None):
    nq, nk = S // tq, S // tk
    assert S % tq == 0 and S % tk == 0 and tq % tk == 0
    assert q_off % pairs == 0 and k_off % pairs == 0 and v_off % pairs == 0
    if mode == "dilated":
        max_back = max(w for w, _ in DILATED_PAIRS)
        back = -(-max_back // tk)
        n_steps = back + tq // tk
        kv_tile = lambda i, j: jnp.maximum(i * (tq // tk) - back + j, 0)
    else:
        back = 0
        n_steps = nk
        kv_tile = lambda i, j: jnp.minimum(j, ((i + 1) * tq - 1) // tk)
    in_specs = [pl.BlockSpec((pairs, tq, LANES), lambda b, i, j: (q_off // pairs, b * nq + i, 0)),
                pl.BlockSpec((pairs, tk, LANES), lambda b, i, j: (k_off // pairs, b * nk + kv_tile(i, j), 0)),
                pl.BlockSpec((pairs, tk, LANES), lambda b, i, j: (v_off // pairs, b * nk + kv_tile(i, j), 0))]
    args = [q_arr, k_arr, v_arr]
    scratch = [pltpu.VMEM((2 * pairs, tq, 1), F32), pltpu.VMEM((2 * pairs, tq, LANES), F32),
               pltpu.VMEM((2 * pairs, tq, LANES), BF16), pltpu.VMEM((tq, tk), F32), pltpu.VMEM((tq, tk), F32),
               pltpu.VMEM((tq, tk), BF16), pltpu.VMEM((tq, tk), BF16),
               pltpu.VMEM((tq, 1), F32), pltpu.VMEM((tq, 1), F32)]
    if mode == "moba":
        assert tk % MOBA_BLOCK == 0 and S // MOBA_BLOCK <= HEAD_DIM
        in_specs.append(pl.BlockSpec((pairs, LANES, LANES), lambda b, i, j: (0, b, 0)))
        args.append(extra)
        scratch += [pltpu.VMEM((tq, tk), F32), pltpu.VMEM((2, tk, LANES), BF16)]
    elif mode == "dsa":
        in_specs.append(pl.BlockSpec((1, 1, tq, tk), lambda b, i, j: (b, kv_tile(i, j), i, 0)))
        args.append(extra)
        scratch.append(pltpu.VMEM((tq, tk), F32))
    else:
        in_specs.append(pl.BlockSpec((1, tq, tk), lambda b, i, j: (j, 0, 0)))
        args.append(_dilated_bias_table(tq, tk, back, n_steps))
    return pl.pallas_call(
        functools.partial(_flash_kernel, mode=mode, pairs=pairs, tq=tq, tk=tk, n_steps=n_steps, back=back),
        grid=(B, nq, n_steps),
        in_specs=in_specs,
        out_specs=pl.BlockSpec((pairs, tq, LANES), lambda b, i, j: (0, b * nq + i, 0)),
        out_shape=jax.ShapeDtypeStruct((pairs, B * S, LANES), BF16),
        scratch_shapes=scratch,
        compiler_params=_cparams(("parallel", "parallel", "arbitrary"), 48),
        name="flash_" + mode,
    )(*args)


def _dsa_select_strip(key_scr, tri_scr, o_ref, tpos, n_live, *, r0, rs, ck, topk):
    def count(pred):
        def body(c, acc):
            hit = jnp.where(pred(key_scr[c, r0:r0 + rs, :], c), 1.0, 0.0)
            for s in range(ck // LANES):
                acc = acc + hit[:, s * LANES:(s + 1) * LANES]
            return acc
        acc = lax.fori_loop(0, n_live, body, jnp.zeros((rs, LANES), F32))
        return jnp.sum(acc, axis=1, keepdims=True)

    def value_bit(it, lo):
        cand = lo + jnp.left_shift(jnp.int32(1), 31 - it)
        return jnp.where(count(lambda kc, c: kc >= cand) >= topk, cand, lo)

    thr = lax.fori_loop(0, 32, value_bit, jnp.full((rs, 1), -2**31, jnp.int32))
    need = topk - count(lambda kc, c: kc > thr)

    def emit(c, before):
        kc = key_scr[c, r0:r0 + rs, :]
        kpos = c * ck + lax.broadcasted_iota(jnp.int32, (rs, ck), 1)
        equal = kc == thr
        rank = before + jnp.dot(jnp.where(equal, 1.0, 0.0).astype(BF16), tri_scr[...], preferred_element_type=F32)
        take = ((kc > thr) | (equal & (rank <= need))) & (kpos <= tpos)
        o_ref[0, c, r0:r0 + rs, :] = jnp.where(take, 0.0, -jnp.inf).astype(BF16)
        return rank[:, ck - 1:ck]

    lax.fori_loop(0, n_live, emit, jnp.zeros((rs, 1), F32))


def _dsa_select_kernel(qi_ref, ki_ref, w_ref, o_ref, key_scr, tri_scr, *, tq, ck, rs, n_chunks, topk):
    i = pl.program_id(1)
    tri_scr[...] = jnp.where(lax.broadcasted_iota(jnp.int32, (ck, ck), 0) <= lax.broadcasted_iota(jnp.int32, (ck, ck), 1),
                             1.0, 0.0).astype(BF16)
    lane = lax.broadcasted_iota(jnp.int32, (1, LANES), 1)
    halves = (lane < HEAD_DIM, lane >= HEAD_DIM)
    tpos = i * tq + lax.broadcasted_iota(jnp.int32, (tq, 1), 0)
    n_live = ((i + 1) * tq + ck - 1) // ck
    w = w_ref[...]

    def score_chunk(c, carry):
        kc = ki_ref[0, pl.ds(pl.multiple_of(c * ck, ck), ck), :]
        score = jnp.zeros((tq, ck), F32)
        for h in range(IDX_HEADS):
            qm = jnp.where(halves[h % 2], qi_ref[h // 2], jnp.zeros((), BF16))
            score = score + w[:, h:h + 1] * jnp.maximum(_nt_dot(qm, kc), 0.0)
        kpos = c * ck + lax.broadcasted_iota(jnp.int32, (tq, ck), 1)
        score = jnp.where(kpos <= tpos, score + 0.0, NEG)
        bits = pltpu.bitcast(score, jnp.int32)
        key_scr[c] = jnp.where(bits < 0, bits ^ jnp.int32(0x7FFFFFFF), bits)
        return carry

    lax.fori_loop(0, n_live, score_chunk, 0)

    for r0 in range(0, tq, rs):
        _dsa_select_strip(key_scr, tri_scr, o_ref, tpos[r0:r0 + rs], n_live, r0=r0, rs=rs, ck=ck, topk=topk)

    def blank(c, carry):
        o_ref[0, c] = jnp.full((tq, ck), -jnp.inf, BF16)
        return carry

    lax.fori_loop(n_live, n_chunks, blank, 0)


def _dsa_select(y, aux, qi_off, ki_off, B, S, *, tq=256, ck=512, rs=128):
    n_chunks = S // ck
    assert S % ck == 0 and S % tq == 0 and tq % rs == 0 and qi_off % 2 == 0
    nq = S // tq
    topk = min(DSA_TOPK, S // 4)
    return pl.pallas_call(
        functools.partial(_dsa_select_kernel, tq=tq, ck=ck, rs=rs, n_chunks=n_chunks, topk=topk),
        grid=(B, nq),
        in_specs=[pl.BlockSpec((2, tq, LANES), lambda b, i: (qi_off // 2, b * nq + i, 0)),
                  pl.BlockSpec((1, S, LANES), lambda b, i: (ki_off, b, 0)),
                  pl.BlockSpec((tq, LANES), lambda b, i: (b * nq + i, 0))],
        out_specs=pl.BlockSpec((1, n_chunks, tq, ck), lambda b, i: (b, 0, i, 0)),
        out_shape=jax.ShapeDtypeStruct((B, n_chunks, S, ck), BF16),
        scratch_shapes=[pltpu.VMEM((n_chunks, tq, ck), jnp.int32), pltpu.VMEM((ck, ck), BF16)],
        compiler_params=_cparams(("parallel", "parallel"), 48),
        name="dsa_select",
    )(y, y, aux)


def _out_proj_kernel(*refs, n_in):
    o_refs, w_ref, h_ref, out_ref = refs[:n_in], refs[n_in], refs[n_in + 1], refs[n_in + 2]
    slabs = [o[c] for o in o_refs for c in range(o.shape[0])]
    o = jnp.concatenate(slabs, axis=-1)
    out_ref[...] = h_ref[...] + jnp.dot(o, w_ref[...], preferred_element_type=F32)


def _out_proj(os_, w_bf16, h, *, tm=512):
    T, D = h.shape
    width = sum(o.shape[0] for o in os_) * LANES
    assert w_bf16.shape == (width, D)
    in_specs = [pl.BlockSpec((o.shape[0], tm, LANES), lambda i: (0, i, 0)) for o in os_]
    in_specs += [pl.BlockSpec((width, D), lambda i: (0, 0)), pl.BlockSpec((tm, D), lambda i: (i, 0))]
    return pl.pallas_call(
        functools.partial(_out_proj_kernel, n_in=len(os_)),
        grid=(T // tm,),
        in_specs=in_specs,
        out_specs=pl.BlockSpec((tm, D), lambda i: (i, 0)),
        out_shape=jax.ShapeDtypeStruct((T, D), F32),
        compiler_params=_cparams(("parallel",), 32),
        name="out_proj",
    )(*os_, w_bf16, h)


def _top_values(s, k):
    out = []
    for _ in range(k + 1):
        cur = jnp.max(s, axis=0, keepdims=True)
        out.append(cur)
        s = jnp.where(s >= cur, -jnp.inf, s)
    return jnp.concatenate(out[:k], axis=0), out[k]


def _pair_candidates(t1, t2):
    k = PEER_TOPK
    parts = [t1[0:1] + t2]
    parts += [t1[a:a + 1] + t2[0:k // 2] for a in range(1, k // 2)]
    parts.append(t1[k // 2:k] + t2[0:1])
    return jnp.concatenate(parts, axis=0)


def _kth_largest(c, k):
    left = jnp.full((1, c.shape[1]), float(k), F32)
    tau = jnp.zeros((1, c.shape[1]), F32)
    for _ in range(k):
        cur = jnp.max(c, axis=0, keepdims=True)
        hit = c >= cur
        tau = jnp.where(left > 0.0, cur, tau)
        left = left - jnp.sum(jnp.where(hit, 1.0, 0.0), axis=0, keepdims=True)
        c = jnp.where(hit, -jnp.inf, c)
    return tau


def _peer_route_kernel(h_ref, g_ref, wq_ref, sk_ref, xn_ref, th1_ref, e1_ref, e2_ref):
    xn = _rms(h_ref[...], g_ref[...]).astype(BF16)
    xn_ref[...] = xn
    for hd in range(PEER_HEADS):
        q = jnp.dot(xn, wq_ref[:, hd * 2 * N_KEYS:(hd + 1) * 2 * N_KEYS], preferred_element_type=F32).astype(BF16)
        s1 = _nt_dot(sk_ref[2 * hd], q[:, :N_KEYS])
        s2 = _nt_dot(sk_ref[2 * hd + 1], q[:, N_KEYS:])
        s1 = s1 - jnp.max(s1, axis=0, keepdims=True)
        s2 = s2 - jnp.max(s2, axis=0, keepdims=True)
        (t1, next1), (t2, next2) = _top_values(s1, PEER_TOPK), _top_values(s2, PEER_TOPK)
        cand = _pair_candidates(t1, t2)
        chosen = cand >= _kth_largest(cand, PEER_TOPK)
        log_z = jnp.log(jnp.sum(jnp.where(chosen, jnp.exp(cand), 0.0), axis=0, keepdims=True))
        runner_up = jnp.maximum(jnp.max(jnp.where(chosen, -jnp.inf, cand), axis=0, keepdims=True),
                                jnp.maximum(next1 + t2[0:1], t1[0:1] + next2))
        smallest = jnp.min(jnp.where(chosen, cand, jnp.inf), axis=0, keepdims=True)
        tau = 0.5 * (smallest + runner_up)
        s1 = s1 - log_z
        th1_ref[hd] = jnp.exp((tau - log_z) - s1)
        e1_ref[hd] = jnp.exp(s1)
        e2_ref[hd] = jnp.exp(s2)


def _peer_route(h, g, wq_bf16, sk_bf16, *, tm=256):
    T, D = h.shape
    nq = PEER_HEADS * 2 * N_KEYS
    assert wq_bf16.shape == (D, nq) and sk_bf16.shape == (2 * PEER_HEADS, N_KEYS, N_KEYS)
    return pl.pallas_call(
        _peer_route_kernel,
        grid=(T // tm,),
        in_specs=[pl.BlockSpec((tm, D), lambda i: (i, 0)), pl.BlockSpec((1, D), lambda i: (0, 0)),
                  pl.BlockSpec((D, nq), lambda i: (0, 0)),
                  pl.BlockSpec((2 * PEER_HEADS, N_KEYS, N_KEYS), lambda i: (0, 0, 0))],
        out_specs=[pl.BlockSpec((tm, D), lambda i: (i, 0))]
        + [pl.BlockSpec((PEER_HEADS, N_KEYS, tm), lambda i: (0, 0, i))] * 3,
        out_shape=[jax.ShapeDtypeStruct((T, D), BF16)]
        + [jax.ShapeDtypeStruct((PEER_HEADS, N_KEYS, T), F32)] * 3,
        compiler_params=_cparams(("parallel",), 40),
        name="peer_route",
    )(h, g.reshape(1, D), wq_bf16, sk_bf16)


def _gelu_tanh(x):
    return 0.5 * x * (1.0 + jnp.tanh(math.sqrt(2.0 / math.pi) * (x + 0.044715 * (x * x * x))))


def _peer_expert_kernel(xn_ref, u_ref, vt_ref, th1_ref, e1_ref, e2_ref, h_ref, out_ref,
                        acc_scr, rows_scr, act0_scr, act1_scr, a0_scr, a1_scr, *, eb, sub, tq):
    e = pl.program_id(1)
    rows, sub_rows, n_sub = eb // N_KEYS, sub // N_KEYS, eb // sub
    sublanes = rows_scr.shape[2]
    act_scr, a_scr = (act0_scr, act1_scr), (a0_scr, a1_scr)

    @pl.when(e == 0)
    def _():
        acc_scr[...] = jnp.zeros_like(acc_scr)

    row0 = pl.multiple_of(e * rows, rows)
    for k, tab in enumerate((th1_ref, e1_ref)):
        for hd in range(PEER_HEADS):
            block = tab[hd, pl.ds(row0, rows), :]
            for r in range(rows):
                rows_scr[k, hd * rows + r] = jnp.broadcast_to(block[r:r + 1, :], (sublanes, tq))

    def activations(sb):
        act_scr[sb % 2][...] = _gelu_tanh(_nt_dot(u_ref[sb * sub:(sb + 1) * sub, :], xn_ref[...]))

    def weigh(sb):
        span = 64
        reps = (span // sublanes, 1)
        for r in range(sub_rows):
            for t0 in range(0, tq, LANES):
                for i2 in range(0, N_KEYS, span):
                    w = jnp.zeros((span, LANES), F32)
                    for hd in range(PEER_HEADS):
                        k = hd * rows + sb * sub_rows + r
                        th1 = jnp.tile(rows_scr[0, k, :, t0:t0 + LANES], reps)
                        e1 = jnp.tile(rows_scr[1, k, :, t0:t0 + LANES], reps)
                        e2 = e2_ref[hd, i2:i2 + span, t0:t0 + LANES]
                        w = w + e1 * jnp.where(e2 >= th1, e2, 0.0)
                    lo = r * N_KEYS + i2
                    a_scr[sb % 2][lo:lo + span, t0:t0 + LANES] = (
                        act_scr[sb % 2][lo:lo + span, t0:t0 + LANES] * w).astype(BF16)

    def project(sb):
        acc_scr[...] += jnp.dot(vt_ref[:, sb * sub:(sb + 1) * sub], a_scr[sb % 2][...], preferred_element_type=F32)

    activations(0)
    for sb in range(n_sub):
        if sb + 1 < n_sub:
            activations(sb + 1)
        if sb >= 1:
            project(sb - 1)
        weigh(sb)
    project(n_sub - 1)

    @pl.when(e == pl.num_programs(1) - 1)
    def _():
        out_ref[...] = h_ref[...] + acc_scr[...].T


F32_SUBLANES = 8


def _peer_experts(xn, u_bf16, vt_bf16, tables, h, *, tq=512, eb=2048, sub=512):
    T, D = h.shape
    E = u_bf16.shape[0]
    assert E == N_KEYS * N_KEYS and E % eb == 0 and eb % (F32_SUBLANES * N_KEYS) == 0 and T % tq == 0
    assert eb % sub == 0 and sub % N_KEYS == 0
    table_spec = pl.BlockSpec((PEER_HEADS, N_KEYS, tq), lambda i, e: (0, 0, i), pipeline_mode=pl.Buffered(1))
    return pl.pallas_call(
        functools.partial(_peer_expert_kernel, eb=eb, sub=sub, tq=tq),
        grid=(T // tq, E // eb),
        in_specs=[pl.BlockSpec((tq, D), lambda i, e: (i, 0)),
                  pl.BlockSpec((eb, D), lambda i, e: (e, 0)),
                  pl.BlockSpec((D, eb), lambda i, e: (0, e)),
                  table_spec, table_spec, table_spec,
                  pl.BlockSpec((tq, D), lambda i, e: (i, 0), pipeline_mode=pl.Buffered(1))],
        out_specs=pl.BlockSpec((tq, D), lambda i, e: (i, 0)),
        out_shape=jax.ShapeDtypeStruct((T, D), F32),
        scratch_shapes=[pltpu.VMEM((D, tq), F32),
                        pltpu.VMEM((2, PEER_HEADS * (eb // N_KEYS), F32_SUBLANES, tq), F32),
                        pltpu.VMEM((sub, tq), F32), pltpu.VMEM((sub, tq), F32),
                        pltpu.VMEM((sub, tq), BF16), pltpu.VMEM((sub, tq), BF16)],
        compiler_params=_cparams(("parallel", "arbitrary"), 60),
        name="peer_experts",
    )(xn, u_bf16, vt_bf16, *tables, h)


def _ple_kernel(h_ref, g_ref, p_ref, wg_ref, wu_ref, out_ref):
    h = h_ref[...]
    gate = jax.nn.sigmoid(jnp.dot(_rms(h, g_ref[...]).astype(BF16), wg_ref[...], preferred_element_type=F32))
    up = jnp.dot(p_ref[...].astype(BF16), wu_ref[...], preferred_element_type=F32)
    out_ref[...] = h + up * gate


def _ple(h, g, p, wg_bf16, wu_bf16, *, tm=512):
    T, D = h.shape
    P = p.shape[1]
    return pl.pallas_call(
        _ple_kernel,
        grid=(T // tm,),
        in_specs=[pl.BlockSpec((tm, D), lambda i: (i, 0)), pl.BlockSpec((1, D), lambda i: (0, 0)),
                  pl.BlockSpec((tm, P), lambda i: (i, 0)), pl.BlockSpec((D, D), lambda i: (0, 0)),
                  pl.BlockSpec((P, D), lambda i: (0, 0))],
        out_specs=pl.BlockSpec((tm, D), lambda i: (i, 0)),
        out_shape=jax.ShapeDtypeStruct((T, D), F32),
        compiler_params=_cparams(("parallel",), 32),
        name="ple",
    )(h, g.reshape(1, D), p, wg_bf16, wu_bf16)


def _tile_gain(g, scale=1.0):
    return jnp.tile(g.astype(F32) * scale, LANES // HEAD_DIM)


def _ab_mixer(h, rope, B, S, g, w_in, w_out, a_q, a_k, b_q, b_k):
    A, Bw = A_HEADS * HEAD_DIM, B_HEADS * HEAD_DIM
    cuts = np.cumsum([A, A, A, Bw, Bw, Bw, IDX_HEADS * HEAD_DIM, HEAD_DIM, IDX_HEADS])
    base, ki0, wi0 = w_in[:, :cuts[6]], w_in[:, cuts[6]:cuts[7]], w_in[:, cuts[7]:cuts[8]]
    w_pad = jnp.concatenate([base, ki0, ki0, wi0, jnp.zeros((D_MODEL, LANES - IDX_HEADS), F32)], axis=1).astype(BF16)
    scale = HEAD_DIM ** -0.5 * LOG2E
    kinds = ["nr"] * 8 + ["p"] * 4 + ["nr"] * 8 + ["p"] * 4 + ["r"] * 3 + ["pa"]
    one = jnp.ones((LANES,), F32)
    gains = jnp.stack([_tile_gain(a_q, scale)] * 4 + [_tile_gain(a_k)] * 4 + [one] * 4
                      + [_tile_gain(b_q, scale)] * 4 + [_tile_gain(b_k)] * 4 + [one] * 8)
    y, aux = _project(h, g, w_pad, gains, rope, kinds)
    kmean = _block_means(y, 4, 4, B, S)
    oa = _flash("moba", y, y, y, 0, 4, 8, 4, B, S, tq=512, tk=512, extra=kmean)
    bias = _dsa_select(y, aux, 24, 26, B, S)
    ob = _flash("dsa", y, y, y, 12, 16, 20, 4, B, S, tq=512, tk=512, extra=bias)
    return _out_proj([oa, ob], w_out.astype(BF16), h)


def _c_mixer(h, rope, B, S, g, w_in, w_out, q_norm, k_norm):
    scale = HEAD_DIM ** -0.5 * LOG2E
    kinds = ["nr"] * 16 + ["p"] * 7 + ["pa"]
    one = jnp.ones((LANES,), F32)
    gains = jnp.stack([_tile_gain(q_norm, scale)] * 8 + [_tile_gain(k_norm)] * 8 + [one] * 8)
    y, _ = _project(h, g, w_in.astype(BF16), gains, rope, kinds)
    o = _flash("dilated", y, y, y, 0, 8, 16, 8, B, S, tq=512, tk=512)
    return _out_proj([o], w_out.astype(BF16), h)


def _peer(h, g, w_query, sub_keys, expert_u, expert_v):
    sk = sub_keys.reshape(2 * PEER_HEADS, N_KEYS, N_KEYS).astype(BF16)
    xn, *tables = _peer_route(h, g, w_query.astype(BF16), sk)
    return _peer_experts(xn, expert_u.astype(BF16), expert_v.T.astype(BF16), tables, h)


def kernel(x, p, positions, attn_norm, ffn_norm, ple_norm, ab_w_in, ab_w_out, a_q_norm, a_k_norm, b_q_norm,
           b_k_norm, c_w_in, c_w_out, c_q_norm, c_k_norm, peer_w_query, peer_sub_keys, peer_u, peer_v,
           ple_w_gate, ple_w_up):
    B, S, D = x.shape
    depth = p.shape[0]
    rope = _rope_tables(positions)
    h = x.reshape(B * S, D)
    for i in range(depth):
        j = i // 2
        if i % 2 == 0:
            h = _ab_mixer(h, rope, B, S, attn_norm[i], ab_w_in[j], ab_w_out[j],
                          a_q_norm[j], a_k_norm[j], b_q_norm[j], b_k_norm[j])
        else:
            h = _c_mixer(h, rope, B, S, attn_norm[i], c_w_in[j], c_w_out[j], c_q_norm[j], c_k_norm[j])
        h = _peer(h, ffn_norm[i], peer_w_query[i], peer_sub_keys[i], peer_u[i], peer_v[i])
        h = _ple(h, ple_norm[i], p[i].reshape(B * S, -1), ple_w_gate[i].astype(BF16), ple_w_up[i].astype(BF16))
    return h.reshape(B, S, D)
```

```python
import functools
import math

import numpy as np
import jax
import jax.numpy as jnp
from jax import lax
from jax.experimental import pallas as pl
from jax.experimental.pallas import tpu as pltpu

F32 = jnp.float32
BF16 = jnp.bfloat16

D_MODEL = 1024
HEAD_DIM = 64
ROT_HALF = 8
ROPE_THETA = 500000.0
NORM_EPS = 1e-6
NEG = -1e30
LOG2E = math.log2(math.e)
LANES = 128

A_HEADS = 8
B_HEADS = 8
C_HEADS = 16
IDX_HEADS = 4
MOBA_BLOCK = 256
MOBA_TOPK = 3
DSA_TOPK = 256
DILATED_PAIRS = ((128, 1), (512, 4), (2048, 16))
PEER_HEADS = 8
N_KEYS = 128
PEER_TOPK = 16

V7X_VMEM_BYTES = 64 * 1024 * 1024


def _cparams(sem, vmem_mb):
    assert vmem_mb * 2**20 < V7X_VMEM_BYTES
    return pltpu.CompilerParams(dimension_semantics=sem, vmem_limit_bytes=vmem_mb * 2**20)


def _rms(x, g):
    ms = jnp.mean(x * x, axis=-1, keepdims=True)
    return x * lax.rsqrt(ms + NORM_EPS) * g


def _nt_dot(a, b):
    return lax.dot_general(a, b, (((1,), (1,)), ((), ())), preferred_element_type=F32)


def _split_bf16(x):
    hi = x.astype(BF16)
    lo = (x - hi.astype(F32)).astype(BF16)
    return hi, lo


def _proj_kernel(x_ref, g_ref, w_ref, gains_ref, cos_ref, sa_ref, sb_ref, y_ref, aux_ref, *, kinds, group):
    xn = _rms(x_ref[...], g_ref[...]).astype(BF16)
    row = lax.broadcasted_iota(jnp.int32, (LANES, LANES), 0) // HEAD_DIM
    col = lax.broadcasted_iota(jnp.int32, (LANES, LANES), 1) // HEAD_DIM
    head_avg = jnp.where(row == col, 1.0 / HEAD_DIM, 0.0).astype(BF16)
    n_chunks = len(kinds)
    for g0 in range(0, n_chunks, group):
        g1 = min(g0 + group, n_chunks)
        yg = jnp.dot(xn, w_ref[:, g0 * LANES:g1 * LANES], preferred_element_type=F32)
        for c in range(g0, g1):
            y = yg[:, (c - g0) * LANES:(c - g0 + 1) * LANES]
            kind = kinds[c]
            if "n" in kind:
                hi, lo = _split_bf16(y * y)
                msq = (jnp.dot(hi, head_avg, preferred_element_type=F32)
                       + jnp.dot(lo, head_avg, preferred_element_type=F32))
                y = y * lax.rsqrt(msq + NORM_EPS) * gains_ref[c:c + 1, :]
            if "r" in kind:
                y = (y * cos_ref[...] + pltpu.roll(y, LANES - ROT_HALF, 1) * sa_ref[...]
                     + pltpu.roll(y, ROT_HALF, 1) * sb_ref[...])
            if "a" in kind:
                aux_ref[...] = y
            y_ref[c] = y.astype(BF16)


def _project(h, g, w_bf16, gains, rope, kinds, *, tm=512, group=4):
    T, D = h.shape
    n_chunks = len(kinds)
    assert w_bf16.shape == (D, n_chunks * LANES) and T % tm == 0
    cos, sa, sb = rope
    tok = lambda i: (i, 0)
    fixed = lambda i: (0, 0)
    return pl.pallas_call(
        functools.partial(_proj_kernel, kinds=tuple(kinds), group=group),
        grid=(T // tm,),
        in_specs=[pl.BlockSpec((tm, D), tok), pl.BlockSpec((1, D), fixed),
                  pl.BlockSpec((D, n_chunks * LANES), fixed), pl.BlockSpec((n_chunks, LANES), fixed),
                  pl.BlockSpec((tm, LANES), tok), pl.BlockSpec((tm, LANES), tok), pl.BlockSpec((tm, LANES), tok)],
        out_specs=[pl.BlockSpec((n_chunks, tm, LANES), lambda i: (0, i, 0)), pl.BlockSpec((tm, LANES), tok)],
        out_shape=[jax.ShapeDtypeStruct((n_chunks, T, LANES), BF16), jax.ShapeDtypeStruct((T, LANES), F32)],
        compiler_params=_cparams(("parallel",), 48),
        name="proj",
    )(h, g.reshape(1, D), w_bf16, gains, cos, sa, sb)


def _rope_tables(positions):
    B, S = positions.shape
    inv = ROPE_THETA ** (-jnp.arange(ROT_HALF, dtype=F32) / ROT_HALF)
    ang = positions.astype(F32)[..., None] * inv
    cos, sin = jnp.cos(ang), jnp.sin(ang)
    rest = HEAD_DIM - 2 * ROT_HALF
    one, zero, z8 = jnp.ones((B, S, rest), F32), jnp.zeros((B, S, rest), F32), jnp.zeros((B, S, ROT_HALF), F32)
    tile = lambda parts: jnp.tile(jnp.concatenate(parts, axis=-1), (1, 1, LANES // HEAD_DIM)).reshape(B * S, LANES)
    return tile([cos, cos, one]), tile([-sin, z8, zero]), tile([z8, sin, zero])


def _kmean_kernel(k_ref, o_ref):
    n = pl.program_id(1)

    @pl.when(n == 0)
    def _():
        o_ref[...] = jnp.zeros_like(o_ref)

    o_ref[:, pl.ds(n, 1), :] = jnp.mean(k_ref[...].astype(F32), axis=1, keepdims=True)


def _block_means(y, k_off, pairs, B, S):
    nb = S // MOBA_BLOCK
    assert nb <= LANES and k_off % pairs == 0
    return pl.pallas_call(
        _kmean_kernel,
        grid=(B, nb),
        in_specs=[pl.BlockSpec((pairs, MOBA_BLOCK, LANES), lambda b, n: (k_off // pairs, b * nb + n, 0))],
        out_specs=pl.BlockSpec((pairs, LANES, LANES), lambda b, n: (0, b, 0)),
        out_shape=jax.ShapeDtypeStruct((pairs, B * LANES, LANES), F32),
        compiler_params=_cparams(("parallel", "arbitrary"), 16),
        name="moba_kmean",
    )(y)


MASK_BIG = 32768.0


def _flash_kernel(*refs, mode, pairs, tq, ts, tk, n_steps, back):
    q_ref, k_ref, v_ref, x_ref, o_ref, m_scr, acc_scr, qa_scr = refs[:8]
    s_scr, p_scr, alpha_scr = refs[8:10], refs[10:12], refs[12:14]
    if mode == "dilated":
        b_ref = x_ref
    elif mode == "dsa":
        b_ref, bias_scr = x_ref, refs[14]
    else:
        km_ref, bias_scr, oh_scr = x_ref, refs[14], refs[15]
    i = pl.program_id(1)
    j = pl.program_id(2)
    lane = lax.broadcasted_iota(jnp.int32, (1, LANES), 1)
    halves = (lane < HEAD_DIM, lane >= HEAD_DIM)
    if mode == "dilated":
        kt = i * (tq // tk) - back + j
        active = kt >= 0
    else:
        kt = j
        active = j * tk < (i + 1) * tq
    tpos = i * tq + lax.broadcasted_iota(jnp.int32, (tq, 1), 0)

    n_heads = 2 * pairs
    n_sub = tq // ts
    n_units = n_sub * n_heads

    @pl.when(j == 0)
    def _():
        m_scr[...] = jnp.full_like(m_scr, NEG)
        acc_scr[...] = jnp.zeros_like(acc_scr)
        blk = lax.broadcasted_iota(jnp.int32, (ts, LANES), 1)
        for sub in range(n_sub):
            rows = slice(sub * ts, (sub + 1) * ts)
            qblk = tpos[rows] // MOBA_BLOCK
            for c in range(pairs):
                if mode == "moba":
                    km_hi, km_lo = _split_bf16(km_ref[c])
                for half in range(2):
                    qm = jnp.where(halves[half], q_ref[c, rows, :], jnp.zeros((), BF16))
                    if mode == "moba":
                        gate = jnp.where(blk < qblk, _nt_dot(qm, km_hi) + _nt_dot(qm, km_lo), NEG)
                        rest, thr = gate, None
                        for _ in range(MOBA_TOPK):
                            thr = jnp.max(rest, axis=1, keepdims=True)
                            rest = jnp.where(rest >= thr, NEG, rest)
                        attended = ((gate >= thr) & (blk < qblk)) | (blk == qblk)
                        feat = jnp.where(attended, 0.0, -MASK_BIG)
                        if half == 0:
                            feat = pltpu.roll(feat, HEAD_DIM, 1)
                        qm = jnp.where(halves[half], qm, feat.astype(BF16))
                    qa_scr[sub * n_heads + 2 * c + half] = qm

    def attend(bias):
        if mode == "moba":
            kblk = (kt * tk + lax.broadcasted_iota(jnp.int32, (tk, 1), 0)) // MOBA_BLOCK
            oh_scr[0] = jnp.where(lane == kblk + HEAD_DIM, 1.0, 0.0).astype(BF16)
            oh_scr[1] = jnp.where(lane == kblk, 1.0, 0.0).astype(BF16)

        def scores(u, par):
            kp = k_ref[(u % n_heads) // 2]
            if mode == "moba":
                kp = jnp.where(halves[par], kp, oh_scr[par])
            s_scr[par][...] = _nt_dot(qa_scr[u], kp)

        def bias_rows(u):
            start = u // n_heads * ts
            if isinstance(start, int):
                return bias[start:start + ts, :]
            return bias[pl.ds(pl.multiple_of(start, ts), ts), :]

        def softmax(u, par):
            s = s_scr[par][...]
            if bias is not None:
                s = s + (bias[...] if n_sub == 1 else bias_rows(u))
            m_old = m_scr[u]
            m_new = jnp.maximum(m_old, jnp.max(s, axis=1, keepdims=True))
            p_scr[par][...] = jnp.exp2(s - m_new).astype(BF16)
            alpha_scr[par][...] = jnp.exp2(m_old - m_new)
            m_scr[u] = m_new

        def values(u, par):
            va = jnp.where(halves[par], v_ref[(u % n_heads) // 2], jnp.ones((), BF16))
            acc_scr[u] = alpha_scr[par][...] * acc_scr[u] + jnp.dot(p_scr[par][...], va, preferred_element_type=F32)

        scores(0, 0)
        scores(1, 1)
        softmax(0, 0)

        def body(c, carry):
            scores(2 * c + 2, 0)
            softmax(2 * c + 1, 1)
            values(2 * c, 0)
            scores(2 * c + 3, 1)
            softmax(2 * c + 2, 0)
            values(2 * c + 1, 1)
            return carry

        lax.fori_loop(0, n_units // 2 - 1, body, 0)
        softmax(n_units - 1, 1)
        values(n_units - 2, 0)
        values(n_units - 1, 1)

    if mode == "moba":
        on_diagonal = (kt + 1) * tk > i * tq

        @pl.when(active & on_diagonal)
        def _():
            kpos = kt * tk + lax.broadcasted_iota(jnp.int32, (tq, tk), 1)
            bias_scr[...] = jnp.where(kpos <= tpos, 0.0, -jnp.inf)
            attend(bias_scr)

        @pl.when(active & jnp.logical_not(on_diagonal))
        def _():
            attend(None)
    elif mode == "dsa":
        @pl.when(active)
        def _():
            bias_scr[...] = b_ref[0, 0].astype(F32)
            attend(bias_scr)
    else:
        @pl.when(active)
        def _():
            attend(b_ref.at[0])

    @pl.when(j == n_steps - 1)
    def _():
        for sub in range(n_sub):
            for c in range(pairs):
                a0, a1 = acc_scr[sub * n_heads + 2 * c], acc_scr[sub * n_heads + 2 * c + 1]
                num = jnp.where(halves[0], a0, a1)
                den = pltpu.roll(jnp.where(halves[0], a1, a0), HEAD_DIM, 1)
                o_ref[c, sub * ts:(sub + 1) * ts, :] = (num / den).astype(BF16)


def _dilated_bias_table(tq, tk, back, n_steps):
    r = np.arange(tq)[:, None]
    c = np.arange(tk)[None, :]
    tabs = []
    for j in range(n_steps):
        d = (back - j) * tk + r - c
        mult = sum(((d >= 0) & (d <= w) & (d % dil == 0)).astype(np.float64) for w, dil in DILATED_PAIRS)
        tabs.append(np.where(mult > 0, np.log2(np.maximum(mult, 1.0)), -np.inf))
    return jnp.asarray(np.stack(tabs), F32)


def _flash(mode, q_arr, k_arr, v_arr, q_off, k_off, v_off, pairs, B, S, *, tq, tk, ts=None, extra=None):
    ts = ts or tq
    nq, nk = S // tq, S // tk
    n_units = tq // ts * 2 * pairs
    assert S % tq == 0 and S % tk == 0 and tq % tk == 0 and tq % ts == 0
    assert q_off % pairs == 0 and k_off % pairs == 0 and v_off % pairs == 0
    if mode == "dilated":
        max_back = max(w for w, _ in DILATED_PAIRS)
        back = -(-max_back // tk)
        n_steps = back + tq // tk
        kv_tile = lambda i, j: jnp.maximum(i * (tq // tk) - back + j, 0)
    else:
        back = 0
        n_steps = nk
        kv_tile = lambda i, j: jnp.minimum(j, ((i + 1) * tq - 1) // tk)
    in_specs = [pl.BlockSpec((pairs, tq, LANES), lambda b, i, j: (q_off // pairs, b * nq + i, 0)),
                pl.BlockSpec((pairs, tk, LANES), lambda b, i, j: (k_off // pairs, b * nk + kv_tile(i, j), 0)),
                pl.BlockSpec((pairs, tk, LANES), lambda b, i, j: (v_off // pairs, b * nk + kv_tile(i, j), 0))]
    args = [q_arr, k_arr, v_arr]
    scratch = [pltpu.VMEM((n_units, ts, 1), F32), pltpu.VMEM((n_units, ts, LANES), F32),
               pltpu.VMEM((n_units, ts, LANES), BF16), pltpu.VMEM((ts, tk), F32), pltpu.VMEM((ts, tk), F32),
               pltpu.VMEM((ts, tk), BF16), pltpu.VMEM((ts, tk), BF16),
               pltpu.VMEM((ts, 1), F32), pltpu.VMEM((ts, 1), F32)]
    if mode == "moba":
        assert tk % MOBA_BLOCK == 0 and S // MOBA_BLOCK <= HEAD_DIM
        in_specs.append(pl.BlockSpec((pairs, LANES, LANES), lambda b, i, j: (0, b, 0)))
        args.append(extra)
        scratch += [pltpu.VMEM((tq, tk), F32), pltpu.VMEM((2, tk, LANES), BF16)]
    elif mode == "dsa":
        in_specs.append(pl.BlockSpec((1, 1, tq, tk), lambda b, i, j: (b, kv_tile(i, j), i, 0)))
        args.append(extra)
        scratch.append(pltpu.VMEM((tq, tk), F32))
    else:
        in_specs.append(pl.BlockSpec((1, tq, tk), lambda b, i, j: (j, 0, 0)))
        args.append(_dilated_bias_table(tq, tk, back, n_steps))
    return pl.pallas_call(
        functools.partial(_flash_kernel, mode=mode, pairs=pairs, tq=tq, ts=ts, tk=tk, n_steps=n_steps, back=back),
        grid=(B, nq, n_steps),
        in_specs=in_specs,
        out_specs=pl.BlockSpec((pairs, tq, LANES), lambda b, i, j: (0, b * nq + i, 0)),
        out_shape=jax.ShapeDtypeStruct((pairs, B * S, LANES), BF16),
        scratch_shapes=scratch,
        compiler_params=_cparams(("parallel", "parallel", "arbitrary"), 48),
        name="flash_" + mode,
    )(*args)


def _dsa_select_strip(key_scr, top_scr, tri_scr, o_ref, tpos, n_live, *, r0, rs, ck, topk):
    def count(pred, src=key_scr, dtype=F32):
        assert (key_scr.shape[0] * ck) // LANES <= 256
        def body(c, acc):
            hit = jnp.where(pred(src[c, r0:r0 + rs, :], c), jnp.ones((), dtype), jnp.zeros((), dtype))
            for s in range(ck // LANES):
                acc = acc + hit[:, s * LANES:(s + 1) * LANES]
            return acc
        acc = lax.fori_loop(0, n_live, body, jnp.zeros((rs, LANES), dtype))
        return jnp.sum(acc.astype(F32), axis=1, keepdims=True)

    def upper_bit(it, lo):
        cand = lo + jnp.left_shift(jnp.int32(1), 15 - it)
        pattern = jnp.where(cand < 0, cand ^ jnp.int32(0x7FFF), cand)
        pattern = jnp.where((pattern > 0) & (pattern < 0x80), jnp.int32(0x80), pattern)
        value = pltpu.bitcast(jnp.left_shift(pattern, 16), F32).astype(BF16)
        return jnp.where(count(lambda sc, c: sc >= value, top_scr, BF16) >= topk, cand, lo)

    upper = lax.fori_loop(0, 16, upper_bit, jnp.full((rs, 1), -2**15, jnp.int32))

    def lower_bit(it, lo):
        cand = lo + jnp.left_shift(jnp.int32(1), 15 - it)
        return jnp.where(count(lambda kc, c: kc >= cand) >= topk, cand, lo)

    thr = lax.fori_loop(0, 16, lower_bit, jnp.left_shift(upper, 16))
    need = topk - count(lambda kc, c: kc > thr)

    def emit(c, before):
        kc = key_scr[c, r0:r0 + rs, :]
        kpos = c * ck + lax.broadcasted_iota(jnp.int32, (rs, ck), 1)
        equal = kc == thr
        rank = before + jnp.dot(jnp.where(equal, 1.0, 0.0).astype(BF16), tri_scr[...], preferred_element_type=F32)
        take = ((kc > thr) | (equal & (rank <= need))) & (kpos <= tpos)
        o_ref[0, c, r0:r0 + rs, :] = jnp.where(take, 0.0, -jnp.inf).astype(BF16)
        return rank[:, ck - 1:ck]

    lax.fori_loop(0, n_live, emit, jnp.zeros((rs, 1), F32))


def _dsa_select_kernel(qi_ref, ki_ref, w_ref, o_ref, key_scr, top_scr, tri_scr, *, tq, ck, rs, n_chunks, topk):
    i = pl.program_id(1)
    tri_scr[...] = jnp.where(lax.broadcasted_iota(jnp.int32, (ck, ck), 0) <= lax.broadcasted_iota(jnp.int32, (ck, ck), 1),
                             1.0, 0.0).astype(BF16)
    lane = lax.broadcasted_iota(jnp.int32, (1, LANES), 1)
    halves = (lane < HEAD_DIM, lane >= HEAD_DIM)
    tpos = i * tq + lax.broadcasted_iota(jnp.int32, (tq, 1), 0)
    n_live = ((i + 1) * tq + ck - 1) // ck
    w = w_ref[...]

    def score_chunk(c, carry):
        kc = ki_ref[0, pl.ds(pl.multiple_of(c * ck, ck), ck), :]
        score = jnp.zeros((tq, ck), F32)
        for h in range(IDX_HEADS):
            qm = jnp.where(halves[h % 2], qi_ref[h // 2], jnp.zeros((), BF16))
            score = score + w[:, h:h + 1] * jnp.maximum(_nt_dot(qm, kc), 0.0)
        kpos = c * ck + lax.broadcasted_iota(jnp.int32, (tq, ck), 1)
        score = jnp.where(jnp.abs(score) >= jnp.finfo(F32).tiny, score, 0.0)
        score = jnp.where(kpos <= tpos, score, NEG)
        bits = pltpu.bitcast(score, jnp.int32)
        key_scr[c] = jnp.where(bits < 0, bits ^ jnp.int32(0x7FFFFFFF), bits)
        top_scr[c] = pltpu.bitcast(bits & jnp.int32(-65536), F32).astype(BF16)
        return carry

    lax.fori_loop(0, n_live, score_chunk, 0)

    for r0 in range(0, tq, rs):
        _dsa_select_strip(key_scr, top_scr, tri_scr, o_ref, tpos[r0:r0 + rs], n_live, r0=r0, rs=rs, ck=ck, topk=topk)

    def blank(c, carry):
        o_ref[0, c] = jnp.full((tq, ck), -jnp.inf, BF16)
        return carry

    lax.fori_loop(n_live, n_chunks, blank, 0)


def _dsa_select(y, aux, qi_off, ki_off, B, S, *, tq=256, ck=512, rs=128):
    n_chunks = S // ck
    assert S % ck == 0 and S % tq == 0 and tq % rs == 0 and qi_off % 2 == 0
    nq = S // tq
    topk = min(DSA_TOPK, S // 4)
    return pl.pallas_call(
        functools.partial(_dsa_select_kernel, tq=tq, ck=ck, rs=rs, n_chunks=n_chunks, topk=topk),
        grid=(B, nq),
        in_specs=[pl.BlockSpec((2, tq, LANES), lambda b, i: (qi_off // 2, b * nq + i, 0)),
                  pl.BlockSpec((1, S, LANES), lambda b, i: (ki_off, b, 0)),
                  pl.BlockSpec((tq, LANES), lambda b, i: (b * nq + i, 0))],
        out_specs=pl.BlockSpec((1, n_chunks, tq, ck), lambda b, i: (b, 0, i, 0)),
        out_shape=jax.ShapeDtypeStruct((B, n_chunks, S, ck), BF16),
        scratch_shapes=[pltpu.VMEM((n_chunks, tq, ck), jnp.int32), pltpu.VMEM((n_chunks, tq, ck), BF16),
                        pltpu.VMEM((ck, ck), BF16)],
        compiler_params=_cparams(("parallel", "parallel"), 48),
        name="dsa_select",
    )(y, y, aux)


def _out_proj_kernel(*refs, n_in):
    o_refs, w_ref, h_ref, out_ref = refs[:n_in], refs[n_in], refs[n_in + 1], refs[n_in + 2]
    slabs = [o[c] for o in o_refs for c in range(o.shape[0])]
    o = jnp.concatenate(slabs, axis=-1)
    out_ref[...] = h_ref[...] + jnp.dot(o, w_ref[...], preferred_element_type=F32)


def _out_proj(os_, w_bf16, h, *, tm=512):
    T, D = h.shape
    width = sum(o.shape[0] for o in os_) * LANES
    assert w_bf16.shape == (width, D)
    in_specs = [pl.BlockSpec((o.shape[0], tm, LANES), lambda i: (0, i, 0)) for o in os_]
    in_specs += [pl.BlockSpec((width, D), lambda i: (0, 0)), pl.BlockSpec((tm, D), lambda i: (i, 0))]
    return pl.pallas_call(
        functools.partial(_out_proj_kernel, n_in=len(os_)),
        grid=(T // tm,),
        in_specs=in_specs,
        out_specs=pl.BlockSpec((tm, D), lambda i: (i, 0)),
        out_shape=jax.ShapeDtypeStruct((T, D), F32),
        compiler_params=_cparams(("parallel",), 32),
        name="out_proj",
    )(*os_, w_bf16, h)


def _top_values(s, k):
    out = []
    for _ in range(k + 1):
        cur = jnp.max(s, axis=0, keepdims=True)
        out.append(cur)
        s = jnp.where(s >= cur, -jnp.inf, s)
    return jnp.concatenate(out[:k], axis=0), out[k]


def _pair_candidates(t1, t2):
    k = PEER_TOPK
    parts = [t1[0:1] + t2]
    parts += [t1[a:a + 1] + t2[0:k // 2] for a in range(1, k // 2)]
    parts.append(t1[k // 2:k] + t2[0:1])
    return jnp.concatenate(parts, axis=0)


def _kth_largest(c, k):
    left = jnp.full((1, c.shape[1]), float(k), F32)
    tau = jnp.zeros((1, c.shape[1]), F32)
    for _ in range(k):
        cur = jnp.max(c, axis=0, keepdims=True)
        hit = c >= cur
        tau = jnp.where(left > 0.0, cur, tau)
        left = left - jnp.sum(jnp.where(hit, 1.0, 0.0), axis=0, keepdims=True)
        c = jnp.where(hit, -jnp.inf, c)
    return tau


def _peer_route_kernel(h_ref, g_ref, wq_ref, sk_ref, xn_ref, th1_ref, e1_ref, e2_ref):
    xn = _rms(h_ref[...], g_ref[...]).astype(BF16)
    xn_ref[...] = xn
    for hd in range(PEER_HEADS):
        q = jnp.dot(xn, wq_ref[:, hd * 2 * N_KEYS:(hd + 1) * 2 * N_KEYS], preferred_element_type=F32).astype(BF16)
        s1 = _nt_dot(sk_ref[2 * hd], q[:, :N_KEYS])
        s2 = _nt_dot(sk_ref[2 * hd + 1], q[:, N_KEYS:])
        s1 = s1 - jnp.max(s1, axis=0, keepdims=True)
        s2 = s2 - jnp.max(s2, axis=0, keepdims=True)
        (t1, next1), (t2, next2) = _top_values(s1, PEER_TOPK), _top_values(s2, PEER_TOPK)
        cand = _pair_candidates(t1, t2)
        chosen = cand >= _kth_largest(cand, PEER_TOPK)
        log_z = jnp.log(jnp.sum(jnp.where(chosen, jnp.exp(cand), 0.0), axis=0, keepdims=True))
        runner_up = jnp.maximum(jnp.max(jnp.where(chosen, -jnp.inf, cand), axis=0, keepdims=True),
                                jnp.maximum(next1 + t2[0:1], t1[0:1] + next2))
        smallest = jnp.min(jnp.where(chosen, cand, jnp.inf), axis=0, keepdims=True)
        tau = 0.5 * (smallest + runner_up)
        s1 = s1 - log_z
        th1_ref[hd] = jnp.exp((tau - log_z) - s1)
        e1_ref[hd] = jnp.exp(s1)
        e2_ref[hd] = jnp.exp(s2)


def _peer_route(h, g, wq_bf16, sk_bf16, *, tm=256):
    T, D = h.shape
    nq = PEER_HEADS * 2 * N_KEYS
    assert wq_bf16.shape == (D, nq) and sk_bf16.shape == (2 * PEER_HEADS, N_KEYS, N_KEYS)
    return pl.pallas_call(
        _peer_route_kernel,
        grid=(T // tm,),
        in_specs=[pl.BlockSpec((tm, D), lambda i: (i, 0)), pl.BlockSpec((1, D), lambda i: (0, 0)),
                  pl.BlockSpec((D, nq), lambda i: (0, 0)),
                  pl.BlockSpec((2 * PEER_HEADS, N_KEYS, N_KEYS), lambda i: (0, 0, 0))],
        out_specs=[pl.BlockSpec((tm, D), lambda i: (i, 0))]
        + [pl.BlockSpec((PEER_HEADS, N_KEYS, tm), lambda i: (0, 0, i))] * 3,
        out_shape=[jax.ShapeDtypeStruct((T, D), BF16)]
        + [jax.ShapeDtypeStruct((PEER_HEADS, N_KEYS, T), F32)] * 3,
        compiler_params=_cparams(("parallel",), 40),
        name="peer_route",
    )(h, g.reshape(1, D), wq_bf16, sk_bf16)


def _gelu_tanh(x):
    return 0.5 * x * (1.0 + jnp.tanh(math.sqrt(2.0 / math.pi) * (x + 0.044715 * (x * x * x))))


def _peer_expert_kernel(xn_ref, u_ref, vt_ref, th1_ref, e1_ref, e2_ref, h_ref, out_ref,
                        acc_scr, rows_scr, act0_scr, act1_scr, a0_scr, a1_scr, *, eb, sub, tq):
    e = pl.program_id(1)
    rows, sub_rows, n_sub = eb // N_KEYS, sub // N_KEYS, eb // sub
    sublanes = rows_scr.shape[2]
    act_scr, a_scr = (act0_scr, act1_scr), (a0_scr, a1_scr)

    @pl.when(e == 0)
    def _():
        acc_scr[...] = jnp.zeros_like(acc_scr)

    row0 = pl.multiple_of(e * rows, rows)
    for k, tab in enumerate((th1_ref, e1_ref)):
        for hd in range(PEER_HEADS):
            block = tab[hd, pl.ds(row0, rows), :]
            for r in range(rows):
                rows_scr[k, hd * rows + r] = jnp.broadcast_to(block[r:r + 1, :], (sublanes, tq))

    def activations(sb):
        act_scr[sb % 2][...] = _gelu_tanh(_nt_dot(u_ref[sb * sub:(sb + 1) * sub, :], xn_ref[...]))

    def weigh(sb):
        span = 64
        reps = (span // sublanes, 1)
        for r in range(sub_rows):
            for t0 in range(0, tq, LANES):
                for i2 in range(0, N_KEYS, span):
                    w = jnp.zeros((span, LANES), F32)
                    for hd in range(PEER_HEADS):
                        k = hd * rows + sb * sub_rows + r
                        th1 = jnp.tile(rows_scr[0, k, :, t0:t0 + LANES], reps)
                        e1 = jnp.tile(rows_scr[1, k, :, t0:t0 + LANES], reps)
                        e2 = e2_ref[hd, i2:i2 + span, t0:t0 + LANES]
                        w = w + e1 * jnp.where(e2 >= th1, e2, 0.0)
                    lo = r * N_KEYS + i2
                    a_scr[sb % 2][lo:lo + span, t0:t0 + LANES] = (
                        act_scr[sb % 2][lo:lo + span, t0:t0 + LANES] * w).astype(BF16)

    def project(sb):
        acc_scr[...] += jnp.dot(vt_ref[:, sb * sub:(sb + 1) * sub], a_scr[sb % 2][...], preferred_element_type=F32)

    activations(0)
    for sb in range(n_sub):
        if sb + 1 < n_sub:
            activations(sb + 1)
        if sb >= 1:
            project(sb - 1)
        weigh(sb)
    project(n_sub - 1)

    @pl.when(e == pl.num_programs(1) - 1)
    def _():
        out_ref[...] = h_ref[...] + acc_scr[...].T


F32_SUBLANES = 8


def _peer_experts(xn, u_bf16, vt_bf16, tables, h, *, tq=512, eb=2048, sub=512):
    T, D = h.shape
    E = u_bf16.shape[0]
    assert E == N_KEYS * N_KEYS and E % eb == 0 and eb % (F32_SUBLANES * N_KEYS) == 0 and T % tq == 0
    assert eb % sub == 0 and sub % N_KEYS == 0
    table_spec = pl.BlockSpec((PEER_HEADS, N_KEYS, tq), lambda i, e: (0, 0, i))
    return pl.pallas_call(
        functools.partial(_peer_expert_kernel, eb=eb, sub=sub, tq=tq),
        grid=(T // tq, E // eb),
        in_specs=[pl.BlockSpec((tq, D), lambda i, e: (i, 0)),
                  pl.BlockSpec((eb, D), lambda i, e: (e, 0)),
                  pl.BlockSpec((D, eb), lambda i, e: (0, e)),
                  table_spec, table_spec, table_spec,
                  pl.BlockSpec((tq, D), lambda i, e: (i, 0))],
        out_specs=pl.BlockSpec((tq, D), lambda i, e: (i, 0)),
        out_shape=jax.ShapeDtypeStruct((T, D), F32),
        scratch_shapes=[pltpu.VMEM((D, tq), F32),
                        pltpu.VMEM((2, PEER_HEADS * (eb // N_KEYS), F32_SUBLANES, tq), F32),
                        pltpu.VMEM((sub, tq), F32), pltpu.VMEM((sub, tq), F32),
                        pltpu.VMEM((sub, tq), BF16), pltpu.VMEM((sub, tq), BF16)],
        compiler_params=_cparams(("parallel", "arbitrary"), 56),
        name="peer_experts",
    )(xn, u_bf16, vt_bf16, *tables, h)


def _ple_kernel(h_ref, g_ref, p_ref, wg_ref, wu_ref, out_ref):
    h = h_ref[...]
    gate = jax.nn.sigmoid(jnp.dot(_rms(h, g_ref[...]).astype(BF16), wg_ref[...], preferred_element_type=F32))
    up = jnp.dot(p_ref[...].astype(BF16), wu_ref[...], preferred_element_type=F32)
    out_ref[...] = h + up * gate


def _ple(h, g, p, wg_bf16, wu_bf16, *, tm=512):
    T, D = h.shape
    P = p.shape[1]
    return pl.pallas_call(
        _ple_kernel,
        grid=(T // tm,),
        in_specs=[pl.BlockSpec((tm, D), lambda i: (i, 0)), pl.BlockSpec((1, D), lambda i: (0, 0)),
                  pl.BlockSpec((tm, P), lambda i: (i, 0)), pl.BlockSpec((D, D), lambda i: (0, 0)),
                  pl.BlockSpec((P, D), lambda i: (0, 0))],
        out_specs=pl.BlockSpec((tm, D), lambda i: (i, 0)),
        out_shape=jax.ShapeDtypeStruct((T, D), F32),
        compiler_params=_cparams(("parallel",), 32),
        name="ple",
    )(h, g.reshape(1, D), p, wg_bf16, wu_bf16)


def _tile_gain(g, scale=1.0):
    return jnp.tile(g.astype(F32) * scale, LANES // HEAD_DIM)


def _ab_mixer(h, rope, B, S, g, w_in, w_out, a_q, a_k, b_q, b_k):
    A, Bw = A_HEADS * HEAD_DIM, B_HEADS * HEAD_DIM
    cuts = np.cumsum([A, A, A, Bw, Bw, Bw, IDX_HEADS * HEAD_DIM, HEAD_DIM, IDX_HEADS])
    base, ki0, wi0 = w_in[:, :cuts[6]], w_in[:, cuts[6]:cuts[7]], w_in[:, cuts[7]:cuts[8]]
    w_pad = jnp.concatenate([base, ki0, ki0, wi0, jnp.zeros((D_MODEL, LANES - IDX_HEADS), F32)], axis=1).astype(BF16)
    scale = HEAD_DIM ** -0.5 * LOG2E
    kinds = ["nr"] * 8 + ["p"] * 4 + ["nr"] * 8 + ["p"] * 4 + ["r"] * 3 + ["pa"]
    one = jnp.ones((LANES,), F32)
    gains = jnp.stack([_tile_gain(a_q, scale)] * 4 + [_tile_gain(a_k)] * 4 + [one] * 4
                      + [_tile_gain(b_q, scale)] * 4 + [_tile_gain(b_k)] * 4 + [one] * 8)
    y, aux = _project(h, g, w_pad, gains, rope, kinds)
    kmean = _block_means(y, 4, 4, B, S)
    tq = min(1024, S)
    oa = _flash("moba", y, y, y, 0, 4, 8, 4, B, S, tq=tq, ts=512, tk=512, extra=kmean)
    bias = _dsa_select(y, aux, 24, 26, B, S)
    ob = _flash("dsa", y, y, y, 12, 16, 20, 4, B, S, tq=tq, ts=512, tk=512, extra=bias)
    return _out_proj([oa, ob], w_out.astype(BF16), h)


def _c_mixer(h, rope, B, S, g, w_in, w_out, q_norm, k_norm):
    scale = HEAD_DIM ** -0.5 * LOG2E
    kinds = ["nr"] * 16 + ["p"] * 7 + ["pa"]
    one = jnp.ones((LANES,), F32)
    gains = jnp.stack([_tile_gain(q_norm, scale)] * 8 + [_tile_gain(k_norm)] * 8 + [one] * 8)
    y, _ = _project(h, g, w_in.astype(BF16), gains, rope, kinds)
    o = _flash("dilated", y, y, y, 0, 8, 16, 8, B, S, tq=512, tk=512)
    return _out_proj([o], w_out.astype(BF16), h)


def _peer(h, g, w_query, sub_keys, expert_u, expert_v):
    sk = sub_keys.reshape(2 * PEER_HEADS, N_KEYS, N_KEYS).astype(BF16)
    xn, *tables = _peer_route(h, g, w_query.astype(BF16), sk)
    return _peer_experts(xn, expert_u.astype(BF16), expert_v.T.astype(BF16), tables, h)


def kernel(x, p, positions, attn_norm, ffn_norm, ple_norm, ab_w_in, ab_w_out, a_q_norm, a_k_norm, b_q_norm,
           b_k_norm, c_w_in, c_w_out, c_q_norm, c_k_norm, peer_w_query, peer_sub_keys, peer_u, peer_v,
           ple_w_gate, ple_w_up):
    B, S, D = x.shape
    depth = p.shape[0]
    rope = _rope_tables(positions)
    h = x.reshape(B * S, D)
    for i in range(depth):
        j = i // 2
        if i % 2 == 0:
            h = _ab_mixer(h, rope, B, S, attn_norm[i], ab_w_in[j], ab_w_out[j],
                          a_q_norm[j], a_k_norm[j], b_q_norm[j], b_k_norm[j])
        else:
            h = _c_mixer(h, rope, B, S, attn_norm[i], c_w_in[j], c_w_out[j], c_q_norm[j], c_k_norm[j])
        h = _peer(h, ffn_norm[i], peer_w_query[i], peer_sub_keys[i], peer_u[i], peer_v[i])
        h = _ple(h, ple_norm[i], p[i].reshape(B * S, -1), ple_w_gate[i].astype(BF16), ple_w_up[i].astype(BF16))
    return h.reshape(B, S, D)
```

```python
import functools
import math

import numpy as np
import jax
import jax.numpy as jnp
from jax import lax
from jax.experimental import pallas as pl
from jax.experimental.pallas import tpu as pltpu

F32 = jnp.float32
BF16 = jnp.bfloat16

D_MODEL = 1024
HEAD_DIM = 64
ROT_HALF = 8
ROPE_THETA = 500000.0
NORM_EPS = 1e-6
NEG = -1e30
LOG2E = math.log2(math.e)
LANES = 128

A_HEADS = 8
B_HEADS = 8
C_HEADS = 16
IDX_HEADS = 4
MOBA_BLOCK = 256
MOBA_TOPK = 3
DSA_TOPK = 256
DILATED_PAIRS = ((128, 1), (512, 4), (2048, 16))
PEER_HEADS = 8
N_KEYS = 128
PEER_TOPK = 16

V7X_VMEM_BYTES = 64 * 1024 * 1024


def _cparams(sem, vmem_mb):
    assert vmem_mb * 2**20 < V7X_VMEM_BYTES
    return pltpu.CompilerParams(dimension_semantics=sem, vmem_limit_bytes=vmem_mb * 2**20)


def _rms(x, g):
    ms = jnp.mean(x * x, axis=-1, keepdims=True)
    return x * lax.rsqrt(ms + NORM_EPS) * g


def _nt_dot(a, b):
    return lax.dot_general(a, b, (((1,), (1,)), ((), ())), preferred_element_type=F32)


def _split_bf16(x):
    hi = x.astype(BF16)
    lo = (x - hi.astype(F32)).astype(BF16)
    return hi, lo


def _proj_kernel(x_ref, g_ref, w_ref, gains_ref, cos_ref, sa_ref, sb_ref, y_ref, aux_ref, *, kinds, group):
    xn = _rms(x_ref[...], g_ref[...]).astype(BF16)
    row = lax.broadcasted_iota(jnp.int32, (LANES, LANES), 0) // HEAD_DIM
    col = lax.broadcasted_iota(jnp.int32, (LANES, LANES), 1) // HEAD_DIM
    head_avg = jnp.where(row == col, 1.0 / HEAD_DIM, 0.0).astype(BF16)
    n_chunks = len(kinds)
    for g0 in range(0, n_chunks, group):
        g1 = min(g0 + group, n_chunks)
        yg = jnp.dot(xn, w_ref[:, g0 * LANES:g1 * LANES], preferred_element_type=F32)
        for c in range(g0, g1):
            y = yg[:, (c - g0) * LANES:(c - g0 + 1) * LANES]
            kind = kinds[c]
            if "n" in kind:
                hi, lo = _split_bf16(y * y)
                msq = (jnp.dot(hi, head_avg, preferred_element_type=F32)
                       + jnp.dot(lo, head_avg, preferred_element_type=F32))
                y = y * lax.rsqrt(msq + NORM_EPS) * gains_ref[c:c + 1, :]
            if "r" in kind:
                y = (y * cos_ref[...] + pltpu.roll(y, LANES - ROT_HALF, 1) * sa_ref[...]
                     + pltpu.roll(y, ROT_HALF, 1) * sb_ref[...])
            if "a" in kind:
                aux_ref[...] = y
            y_ref[c] = y.astype(BF16)


def _project(h, g, w_bf16, gains, rope, kinds, *, tm=512, group=4):
    T, D = h.shape
    n_chunks = len(kinds)
    assert w_bf16.shape == (D, n_chunks * LANES) and T % tm == 0
    cos, sa, sb = rope
    tok = lambda i: (i, 0)
    fixed = lambda i: (0, 0)
    return pl.pallas_call(
        functools.partial(_proj_kernel, kinds=tuple(kinds), group=group),
        grid=(T // tm,),
        in_specs=[pl.BlockSpec((tm, D), tok), pl.BlockSpec((1, D), fixed),
                  pl.BlockSpec((D, n_chunks * LANES), fixed), pl.BlockSpec((n_chunks, LANES), fixed),
                  pl.BlockSpec((tm, LANES), tok), pl.BlockSpec((tm, LANES), tok), pl.BlockSpec((tm, LANES), tok)],
        out_specs=[pl.BlockSpec((n_chunks, tm, LANES), lambda i: (0, i, 0)), pl.BlockSpec((tm, LANES), tok)],
        out_shape=[jax.ShapeDtypeStruct((n_chunks, T, LANES), BF16), jax.ShapeDtypeStruct((T, LANES), F32)],
        compiler_params=_cparams(("parallel",), 48),
        name="proj",
    )(h, g.reshape(1, D), w_bf16, gains, cos, sa, sb)


def _rope_tables(positions):
    B, S = positions.shape
    inv = ROPE_THETA ** (-jnp.arange(ROT_HALF, dtype=F32) / ROT_HALF)
    ang = positions.astype(F32)[..., None] * inv
    cos, sin = jnp.cos(ang), jnp.sin(ang)
    rest = HEAD_DIM - 2 * ROT_HALF
    one, zero, z8 = jnp.ones((B, S, rest), F32), jnp.zeros((B, S, rest), F32), jnp.zeros((B, S, ROT_HALF), F32)
    tile = lambda parts: jnp.tile(jnp.concatenate(parts, axis=-1), (1, 1, LANES // HEAD_DIM)).reshape(B * S, LANES)
    return tile([cos, cos, one]), tile([-sin, z8, zero]), tile([z8, sin, zero])


def _kmean_kernel(k_ref, o_ref):
    n = pl.program_id(1)

    @pl.when(n == 0)
    def _():
        o_ref[...] = jnp.zeros_like(o_ref)

    o_ref[:, pl.ds(n, 1), :] = jnp.mean(k_ref[...].astype(F32), axis=1, keepdims=True)


def _block_means(y, k_off, pairs, B, S):
    nb = S // MOBA_BLOCK
    assert nb <= LANES and k_off % pairs == 0
    return pl.pallas_call(
        _kmean_kernel,
        grid=(B, nb),
        in_specs=[pl.BlockSpec((pairs, MOBA_BLOCK, LANES), lambda b, n: (k_off // pairs, b * nb + n, 0))],
        out_specs=pl.BlockSpec((pairs, LANES, LANES), lambda b, n: (0, b, 0)),
        out_shape=jax.ShapeDtypeStruct((pairs, B * LANES, LANES), F32),
        compiler_params=_cparams(("parallel", "arbitrary"), 16),
        name="moba_kmean",
    )(y)


MASK_BIG = 32768.0


def _flash_kernel(*refs, mode, pairs, tq, ts, tk, n_steps, back):
    q_ref, k_ref, v_ref, x_ref, o_ref, m_scr, acc_scr, qa_scr = refs[:8]
    s_scr, p_scr, alpha_scr = refs[8:10], refs[10:12], refs[12:14]
    if mode == "dilated":
        b_ref = x_ref
    elif mode == "dsa":
        b_ref, bias_scr = x_ref, refs[14]
    else:
        km_ref, bias_scr, oh_scr = x_ref, refs[14], refs[15]
    i = pl.program_id(1)
    j = pl.program_id(2)
    lane = lax.broadcasted_iota(jnp.int32, (1, LANES), 1)
    halves = (lane < HEAD_DIM, lane >= HEAD_DIM)
    if mode == "dilated":
        kt = i * (tq // tk) - back + j
        active = kt >= 0
    else:
        kt = j
        active = j * tk < (i + 1) * tq
    tpos = i * tq + lax.broadcasted_iota(jnp.int32, (tq, 1), 0)

    n_heads = 2 * pairs
    n_sub = tq // ts
    n_units = n_sub * n_heads

    @pl.when(j == 0)
    def _():
        m_scr[...] = jnp.full_like(m_scr, NEG)
        acc_scr[...] = jnp.zeros_like(acc_scr)
        blk = lax.broadcasted_iota(jnp.int32, (ts, LANES), 1)
        for sub in range(n_sub):
            rows = slice(sub * ts, (sub + 1) * ts)
            qblk = tpos[rows] // MOBA_BLOCK
            for c in range(pairs):
                if mode == "moba":
                    km_hi, km_lo = _split_bf16(km_ref[c])
                for half in range(2):
                    qm = jnp.where(halves[half], q_ref[c, rows, :], jnp.zeros((), BF16))
                    if mode == "moba":
                        gate = jnp.where(blk < qblk, _nt_dot(qm, km_hi) + _nt_dot(qm, km_lo), NEG)
                        rest, thr = gate, None
                        for _ in range(MOBA_TOPK):
                            thr = jnp.max(rest, axis=1, keepdims=True)
                            rest = jnp.where(rest >= thr, NEG, rest)
                        attended = ((gate >= thr) & (blk < qblk)) | (blk == qblk)
                        feat = jnp.where(attended, 0.0, -MASK_BIG)
                        if half == 0:
                            feat = pltpu.roll(feat, HEAD_DIM, 1)
                        qm = jnp.where(halves[half], qm, feat.astype(BF16))
                    qa_scr[sub * n_heads + 2 * c + half] = qm

    def attend(bias):
        if mode == "moba":
            kblk = (kt * tk + lax.broadcasted_iota(jnp.int32, (tk, 1), 0)) // MOBA_BLOCK
            oh_scr[0] = jnp.where(lane == kblk + HEAD_DIM, 1.0, 0.0).astype(BF16)
            oh_scr[1] = jnp.where(lane == kblk, 1.0, 0.0).astype(BF16)

        def scores(u, par):
            kp = k_ref[(u % n_heads) // 2]
            if mode == "moba":
                kp = jnp.where(halves[par], kp, oh_scr[par])
            s_scr[par][...] = _nt_dot(qa_scr[u], kp)

        def bias_rows(u):
            start = u // n_heads * ts
            if isinstance(start, int):
                return bias[start:start + ts, :]
            return bias[pl.ds(pl.multiple_of(start, ts), ts), :]

        def softmax(u, par):
            s = s_scr[par][...]
            if bias is not None:
                s = s + (bias[...] if n_sub == 1 else bias_rows(u))
            m_old = m_scr[u]
            m_new = jnp.maximum(m_old, jnp.max(s, axis=1, keepdims=True))
            p_scr[par][...] = jnp.exp2(s - m_new).astype(BF16)
            alpha_scr[par][...] = jnp.exp2(m_old - m_new)
            m_scr[u] = m_new

        def values(u, par):
            va = jnp.where(halves[par], v_ref[(u % n_heads) // 2], jnp.ones((), BF16))
            acc_scr[u] = alpha_scr[par][...] * acc_scr[u] + jnp.dot(p_scr[par][...], va, preferred_element_type=F32)

        scores(0, 0)
        scores(1, 1)
        softmax(0, 0)

        def body(c, carry):
            scores(2 * c + 2, 0)
            softmax(2 * c + 1, 1)
            values(2 * c, 0)
            scores(2 * c + 3, 1)
            softmax(2 * c + 2, 0)
            values(2 * c + 1, 1)
            return carry

        lax.fori_loop(0, n_units // 2 - 1, body, 0)
        softmax(n_units - 1, 1)
        values(n_units - 2, 0)
        values(n_units - 1, 1)

    if mode == "moba":
        on_diagonal = (kt + 1) * tk > i * tq

        @pl.when(active & on_diagonal)
        def _():
            kpos = kt * tk + lax.broadcasted_iota(jnp.int32, (tq, tk), 1)
            bias_scr[...] = jnp.where(kpos <= tpos, 0.0, -jnp.inf)
            attend(bias_scr)

        @pl.when(active & jnp.logical_not(on_diagonal))
        def _():
            attend(None)
    elif mode == "dsa":
        @pl.when(active)
        def _():
            bias_scr[...] = b_ref[0, 0].astype(F32)
            attend(bias_scr)
    else:
        @pl.when(active)
        def _():
            attend(b_ref.at[0])

    @pl.when(j == n_steps - 1)
    def _():
        for sub in range(n_sub):
            for c in range(pairs):
                a0, a1 = acc_scr[sub * n_heads + 2 * c], acc_scr[sub * n_heads + 2 * c + 1]
                num = jnp.where(halves[0], a0, a1)
                den = pltpu.roll(jnp.where(halves[0], a1, a0), HEAD_DIM, 1)
                o_ref[c, sub * ts:(sub + 1) * ts, :] = (num / den).astype(BF16)


def _dilated_bias_table(tq, tk, back, n_steps):
    r = np.arange(tq)[:, None]
    c = np.arange(tk)[None, :]
    tabs = []
    for j in range(n_steps):
        d = (back - j) * tk + r - c
        mult = sum(((d >= 0) & (d <= w) & (d % dil == 0)).astype(np.float64) for w, dil in DILATED_PAIRS)
        tabs.append(np.where(mult > 0, np.log2(np.maximum(mult, 1.0)), -np.inf))
    return jnp.asarray(np.stack(tabs), F32)


def _flash(mode, q_arr, k_arr, v_arr, q_off, k_off, v_off, pairs, B, S, *, tq, tk, ts=None, extra=None):
    ts = ts or tq
    nq, nk = S // tq, S // tk
    n_units = tq // ts * 2 * pairs
    assert S % tq == 0 and S % tk == 0 and tq % tk == 0 and tq % ts == 0
    assert q_off % pairs == 0 and k_off % pairs == 0 and v_off % pairs == 0
    if mode == "dilated":
        max_back = max(w for w, _ in DILATED_PAIRS)
        back = -(-max_back // tk)
        n_steps = back + tq // tk
        kv_tile = lambda i, j: jnp.maximum(i * (tq // tk) - back + j, 0)
    else:
        back = 0
        n_steps = nk
        kv_tile = lambda i, j: jnp.minimum(j, ((i + 1) * tq - 1) // tk)
    in_specs = [pl.BlockSpec((pairs, tq, LANES), lambda b, i, j: (q_off // pairs, b * nq + i, 0)),
                pl.BlockSpec((pairs, tk, LANES), lambda b, i, j: (k_off // pairs, b * nk + kv_tile(i, j), 0)),
                pl.BlockSpec((pairs, tk, LANES), lambda b, i, j: (v_off // pairs, b * nk + kv_tile(i, j), 0))]
    args = [q_arr, k_arr, v_arr]
    scratch = [pltpu.VMEM((n_units, ts, 1), F32), pltpu.VMEM((n_units, ts, LANES), F32),
               pltpu.VMEM((n_units, ts, LANES), BF16), pltpu.VMEM((ts, tk), F32), pltpu.VMEM((ts, tk), F32),
               pltpu.VMEM((ts, tk), BF16), pltpu.VMEM((ts, tk), BF16),
               pltpu.VMEM((ts, 1), F32), pltpu.VMEM((ts, 1), F32)]
    if mode == "moba":
        assert tk % MOBA_BLOCK == 0 and S // MOBA_BLOCK <= HEAD_DIM
        in_specs.append(pl.BlockSpec((pairs, LANES, LANES), lambda b, i, j: (0, b, 0)))
        args.append(extra)
        scratch += [pltpu.VMEM((tq, tk), F32), pltpu.VMEM((2, tk, LANES), BF16)]
    elif mode == "dsa":
        in_specs.append(pl.BlockSpec((1, 1, tq, tk), lambda b, i, j: (b, kv_tile(i, j), i, 0)))
        args.append(extra)
        scratch.append(pltpu.VMEM((tq, tk), F32))
    else:
        in_specs.append(pl.BlockSpec((1, tq, tk), lambda b, i, j: (j, 0, 0)))
        args.append(_dilated_bias_table(tq, tk, back, n_steps))
    return pl.pallas_call(
        functools.partial(_flash_kernel, mode=mode, pairs=pairs, tq=tq, ts=ts, tk=tk, n_steps=n_steps, back=back),
        grid=(B, nq, n_steps),
        in_specs=in_specs,
        out_specs=pl.BlockSpec((pairs, tq, LANES), lambda b, i, j: (0, b * nq + i, 0)),
        out_shape=jax.ShapeDtypeStruct((pairs, B * S, LANES), BF16),
        scratch_shapes=scratch,
        compiler_params=_cparams(("parallel", "parallel", "arbitrary"), 48),
        name="flash_" + mode,
    )(*args)


def _row_counts(src, pred, n_live, r0, rs, ck, dtype):
    assert (src.shape[0] * ck) // LANES <= 256

    def body(c, acc):
        hit = jnp.where(pred(src[c, r0:r0 + rs, :], c), jnp.ones((), dtype), jnp.zeros((), dtype))
        for s in range(ck // LANES):
            acc = acc + hit[:, s * LANES:(s + 1) * LANES]
        return acc

    acc = lax.fori_loop(0, n_live, body, jnp.zeros((rs, LANES), dtype))
    return jnp.sum(acc.astype(F32), axis=1, keepdims=True)


def _dsa_threshold_upper(top_scr, n_live, *, rows, ck, topk):
    def upper_bit(it, lo):
        cand = lo + jnp.left_shift(jnp.int32(1), 15 - it)
        pattern = jnp.where(cand < 0, cand ^ jnp.int32(0x7FFF), cand)
        pattern = jnp.where((pattern > 0) & (pattern < 0x80), jnp.int32(0x80), pattern)
        value = pltpu.bitcast(jnp.left_shift(pattern, 16), F32).astype(BF16)
        cnt = _row_counts(top_scr, lambda sc, c: sc >= value, n_live, 0, rows, ck, BF16)
        return jnp.where(cnt >= topk, cand, lo)

    return lax.fori_loop(0, 16, upper_bit, jnp.full((rows, 1), -2**15, jnp.int32))


def _dsa_select_strip(key_scr, tri_scr, o_ref, tpos, upper, n_live, *, r0, rs, ck, topk):
    def count(pred):
        return _row_counts(key_scr, pred, n_live, r0, rs, ck, F32)

    def lower_bit(it, lo):
        cand = lo + jnp.left_shift(jnp.int32(1), 15 - it)
        return jnp.where(count(lambda kc, c: kc >= cand) >= topk, cand, lo)

    thr = lax.fori_loop(0, 16, lower_bit, jnp.left_shift(upper, 16))
    need = topk - count(lambda kc, c: kc > thr)

    def emit(c, before):
        kc = key_scr[c, r0:r0 + rs, :]
        kpos = c * ck + lax.broadcasted_iota(jnp.int32, (rs, ck), 1)
        equal = kc == thr
        rank = before + jnp.dot(jnp.where(equal, 1.0, 0.0).astype(BF16), tri_scr[...], preferred_element_type=F32)
        take = ((kc > thr) | (equal & (rank <= need))) & (kpos <= tpos)
        o_ref[0, c, r0:r0 + rs, :] = jnp.where(take, 0.0, -jnp.inf).astype(BF16)
        return rank[:, ck - 1:ck]

    lax.fori_loop(0, n_live, emit, jnp.zeros((rs, 1), F32))


def _dsa_select_kernel(qi_ref, ki_ref, w_ref, o_ref, key_scr, top_scr, tri_scr, *, tq, ck, rs, n_chunks, topk):
    i = pl.program_id(1)
    tri_scr[...] = jnp.where(lax.broadcasted_iota(jnp.int32, (ck, ck), 0) <= lax.broadcasted_iota(jnp.int32, (ck, ck), 1),
                             1.0, 0.0).astype(BF16)
    lane = lax.broadcasted_iota(jnp.int32, (1, LANES), 1)
    halves = (lane < HEAD_DIM, lane >= HEAD_DIM)
    tpos = i * tq + lax.broadcasted_iota(jnp.int32, (tq, 1), 0)
    n_live = ((i + 1) * tq + ck - 1) // ck
    w = w_ref[...]

    def score_chunk(c, carry):
        kc = ki_ref[0, pl.ds(pl.multiple_of(c * ck, ck), ck), :]
        score = jnp.zeros((tq, ck), F32)
        for h in range(IDX_HEADS):
            qm = jnp.where(halves[h % 2], qi_ref[h // 2], jnp.zeros((), BF16))
            score = score + w[:, h:h + 1] * jnp.maximum(_nt_dot(qm, kc), 0.0)
        kpos = c * ck + lax.broadcasted_iota(jnp.int32, (tq, ck), 1)
        score = jnp.where(jnp.abs(score) >= jnp.finfo(F32).tiny, score, 0.0)
        score = jnp.where(kpos <= tpos, score, NEG)
        bits = pltpu.bitcast(score, jnp.int32)
        key_scr[c] = jnp.where(bits < 0, bits ^ jnp.int32(0x7FFFFFFF), bits)
        top_scr[c] = pltpu.bitcast(bits & jnp.int32(-65536), F32).astype(BF16)
        return carry

    lax.fori_loop(0, n_live, score_chunk, 0)

    upper = _dsa_threshold_upper(top_scr, n_live, rows=tq, ck=ck, topk=topk)
    for r0 in range(0, tq, rs):
        _dsa_select_strip(key_scr, tri_scr, o_ref, tpos[r0:r0 + rs], upper[r0:r0 + rs], n_live,
                          r0=r0, rs=rs, ck=ck, topk=topk)

    def blank(c, carry):
        o_ref[0, c] = jnp.full((tq, ck), -jnp.inf, BF16)
        return carry

    lax.fori_loop(n_live, n_chunks, blank, 0)


def _dsa_select(y, aux, qi_off, ki_off, B, S, *, tq=256, ck=512, rs=128):
    n_chunks = S // ck
    assert S % ck == 0 and S % tq == 0 and tq % rs == 0 and qi_off % 2 == 0
    nq = S // tq
    topk = min(DSA_TOPK, S // 4)
    return pl.pallas_call(
        functools.partial(_dsa_select_kernel, tq=tq, ck=ck, rs=rs, n_chunks=n_chunks, topk=topk),
        grid=(B, nq),
        in_specs=[pl.BlockSpec((2, tq, LANES), lambda b, i: (qi_off // 2, b * nq + i, 0)),
                  pl.BlockSpec((1, S, LANES), lambda b, i: (ki_off, b, 0)),
                  pl.BlockSpec((tq, LANES), lambda b, i: (b * nq + i, 0))],
        out_specs=pl.BlockSpec((1, n_chunks, tq, ck), lambda b, i: (b, 0, i, 0)),
        out_shape=jax.ShapeDtypeStruct((B, n_chunks, S, ck), BF16),
        scratch_shapes=[pltpu.VMEM((n_chunks, tq, ck), jnp.int32), pltpu.VMEM((n_chunks, tq, ck), BF16),
                        pltpu.VMEM((ck, ck), BF16)],
        compiler_params=_cparams(("parallel", "parallel"), 48),
        name="dsa_select",
    )(y, y, aux)


def _out_proj_kernel(*refs, n_in):
    o_refs, w_ref, h_ref, out_ref = refs[:n_in], refs[n_in], refs[n_in + 1], refs[n_in + 2]
    slabs = [o[c] for o in o_refs for c in range(o.shape[0])]
    o = jnp.concatenate(slabs, axis=-1)
    out_ref[...] = h_ref[...] + jnp.dot(o, w_ref[...], preferred_element_type=F32)


def _out_proj(os_, w_bf16, h, *, tm=512):
    T, D = h.shape
    width = sum(o.shape[0] for o in os_) * LANES
    assert w_bf16.shape == (width, D)
    in_specs = [pl.BlockSpec((o.shape[0], tm, LANES), lambda i: (0, i, 0)) for o in os_]
    in_specs += [pl.BlockSpec((width, D), lambda i: (0, 0)), pl.BlockSpec((tm, D), lambda i: (i, 0))]
    return pl.pallas_call(
        functools.partial(_out_proj_kernel, n_in=len(os_)),
        grid=(T // tm,),
        in_specs=in_specs,
        out_specs=pl.BlockSpec((tm, D), lambda i: (i, 0)),
        out_shape=jax.ShapeDtypeStruct((T, D), F32),
        compiler_params=_cparams(("parallel",), 32),
        name="out_proj",
    )(*os_, w_bf16, h)


def _top_values(s, k):
    out = []
    for _ in range(k + 1):
        cur = jnp.max(s, axis=0, keepdims=True)
        out.append(cur)
        s = jnp.where(s >= cur, -jnp.inf, s)
    return jnp.concatenate(out[:k], axis=0), out[k]


def _pair_candidates(t1, t2):
    k = PEER_TOPK
    parts = [t1[0:1] + t2]
    parts += [t1[a:a + 1] + t2[0:k // 2] for a in range(1, k // 2)]
    parts.append(t1[k // 2:k] + t2[0:1])
    return jnp.concatenate(parts, axis=0)


def _kth_largest(c, k):
    left = jnp.full((1, c.shape[1]), float(k), F32)
    tau = jnp.zeros((1, c.shape[1]), F32)
    for _ in range(k):
        cur = jnp.max(c, axis=0, keepdims=True)
        hit = c >= cur
        tau = jnp.where(left > 0.0, cur, tau)
        left = left - jnp.sum(jnp.where(hit, 1.0, 0.0), axis=0, keepdims=True)
        c = jnp.where(hit, -jnp.inf, c)
    return tau


def _peer_route_kernel(h_ref, g_ref, wq_ref, sk_ref, xn_ref, th1_ref, e1_ref, e2_ref):
    xn = _rms(h_ref[...], g_ref[...]).astype(BF16)
    xn_ref[...] = xn
    for hd in range(PEER_HEADS):
        q = jnp.dot(xn, wq_ref[:, hd * 2 * N_KEYS:(hd + 1) * 2 * N_KEYS], preferred_element_type=F32).astype(BF16)
        s1 = _nt_dot(sk_ref[2 * hd], q[:, :N_KEYS])
        s2 = _nt_dot(sk_ref[2 * hd + 1], q[:, N_KEYS:])
        s1 = s1 - jnp.max(s1, axis=0, keepdims=True)
        s2 = s2 - jnp.max(s2, axis=0, keepdims=True)
        (t1, next1), (t2, next2) = _top_values(s1, PEER_TOPK), _top_values(s2, PEER_TOPK)
        cand = _pair_candidates(t1, t2)
        chosen = cand >= _kth_largest(cand, PEER_TOPK)
        log_z = jnp.log(jnp.sum(jnp.where(chosen, jnp.exp(cand), 0.0), axis=0, keepdims=True))
        runner_up = jnp.maximum(jnp.max(jnp.where(chosen, -jnp.inf, cand), axis=0, keepdims=True),
                                jnp.maximum(next1 + t2[0:1], t1[0:1] + next2))
        smallest = jnp.min(jnp.where(chosen, cand, jnp.inf), axis=0, keepdims=True)
        tau = 0.5 * (smallest + runner_up)
        s1 = s1 - log_z
        th1_ref[hd] = jnp.exp((tau - log_z) - s1)
        e1_ref[hd] = jnp.exp(s1)
        e2_ref[hd] = jnp.exp(s2)


def _peer_route(h, g, wq_bf16, sk_bf16, *, tm=256):
    T, D = h.shape
    nq = PEER_HEADS * 2 * N_KEYS
    assert wq_bf16.shape == (D, nq) and sk_bf16.shape == (2 * PEER_HEADS, N_KEYS, N_KEYS)
    return pl.pallas_call(
        _peer_route_kernel,
        grid=(T // tm,),
        in_specs=[pl.BlockSpec((tm, D), lambda i: (i, 0)), pl.BlockSpec((1, D), lambda i: (0, 0)),
                  pl.BlockSpec((D, nq), lambda i: (0, 0)),
                  pl.BlockSpec((2 * PEER_HEADS, N_KEYS, N_KEYS), lambda i: (0, 0, 0))],
        out_specs=[pl.BlockSpec((tm, D), lambda i: (i, 0))]
        + [pl.BlockSpec((PEER_HEADS, N_KEYS, tm), lambda i: (0, 0, i))] * 3,
        out_shape=[jax.ShapeDtypeStruct((T, D), BF16)]
        + [jax.ShapeDtypeStruct((PEER_HEADS, N_KEYS, T), F32)] * 3,
        compiler_params=_cparams(("parallel",), 40),
        name="peer_route",
    )(h, g.reshape(1, D), wq_bf16, sk_bf16)


def _gelu_tanh(x):
    return 0.5 * x * (1.0 + jnp.tanh(math.sqrt(2.0 / math.pi) * (x + 0.044715 * (x * x * x))))


def _peer_expert_kernel(xn_ref, u_ref, vt_ref, th1_ref, e1_ref, e2_ref, h_ref, out_ref,
                        acc_scr, rows_scr, act0_scr, act1_scr, a0_scr, a1_scr, *, eb, sub, tq):
    e = pl.program_id(1)
    rows, sub_rows, n_sub = eb // N_KEYS, sub // N_KEYS, eb // sub
    sublanes = rows_scr.shape[2]
    act_scr, a_scr = (act0_scr, act1_scr), (a0_scr, a1_scr)

    @pl.when(e == 0)
    def _():
        acc_scr[...] = jnp.zeros_like(acc_scr)

    row0 = pl.multiple_of(e * rows, rows)
    for k, tab in enumerate((th1_ref, e1_ref)):
        for hd in range(PEER_HEADS):
            block = tab[hd, pl.ds(row0, rows), :]
            for r in range(rows):
                rows_scr[k, hd * rows + r] = jnp.broadcast_to(block[r:r + 1, :], (sublanes, tq))

    def activations(sb):
        act_scr[sb % 2][...] = _gelu_tanh(_nt_dot(u_ref[sb * sub:(sb + 1) * sub, :], xn_ref[...]).astype(BF16))

    def weigh(sb):
        span = 64
        reps = (span // sublanes, 1)
        for r in range(sub_rows):
            for t0 in range(0, tq, LANES):
                for i2 in range(0, N_KEYS, span):
                    w = jnp.zeros((span, LANES), F32)
                    for hd in range(PEER_HEADS):
                        k = hd * rows + sb * sub_rows + r
                        th1 = jnp.tile(rows_scr[0, k, :, t0:t0 + LANES], reps)
                        e1 = jnp.tile(rows_scr[1, k, :, t0:t0 + LANES], reps)
                        e2 = e2_ref[hd, i2:i2 + span, t0:t0 + LANES]
                        w = w + e1 * jnp.where(e2 >= th1, e2, 0.0)
                    lo = r * N_KEYS + i2
                    a_scr[sb % 2][lo:lo + span, t0:t0 + LANES] = (
                        act_scr[sb % 2][lo:lo + span, t0:t0 + LANES] * w.astype(BF16))

    def project(sb):
        acc_scr[...] += jnp.dot(vt_ref[:, sb * sub:(sb + 1) * sub], a_scr[sb % 2][...], preferred_element_type=F32)

    activations(0)
    for sb in range(n_sub):
        if sb + 1 < n_sub:
            activations(sb + 1)
        if sb >= 1:
            project(sb - 1)
        weigh(sb)
    project(n_sub - 1)

    @pl.when(e == pl.num_programs(1) - 1)
    def _():
        out_ref[...] = h_ref[...] + acc_scr[...].T


F32_SUBLANES = 8


def _peer_experts(xn, u_bf16, vt_bf16, tables, h, *, tq=512, eb=2048, sub=1024):
    T, D = h.shape
    E = u_bf16.shape[0]
    assert E == N_KEYS * N_KEYS and E % eb == 0 and eb % (F32_SUBLANES * N_KEYS) == 0 and T % tq == 0
    assert eb % sub == 0 and sub % N_KEYS == 0
    table_spec = pl.BlockSpec((PEER_HEADS, N_KEYS, tq), lambda i, e: (0, 0, i))
    return pl.pallas_call(
        functools.partial(_peer_expert_kernel, eb=eb, sub=sub, tq=tq),
        grid=(T // tq, E // eb),
        in_specs=[pl.BlockSpec((tq, D), lambda i, e: (i, 0)),
                  pl.BlockSpec((eb, D), lambda i, e: (e, 0)),
                  pl.BlockSpec((D, eb), lambda i, e: (0, e)),
                  table_spec, table_spec, table_spec,
                  pl.BlockSpec((tq, D), lambda i, e: (i, 0))],
        out_specs=pl.BlockSpec((tq, D), lambda i, e: (i, 0)),
        out_shape=jax.ShapeDtypeStruct((T, D), F32),
        scratch_shapes=[pltpu.VMEM((D, tq), F32),
                        pltpu.VMEM((2, PEER_HEADS * (eb // N_KEYS), F32_SUBLANES, tq), F32),
                        pltpu.VMEM((sub, tq), BF16), pltpu.VMEM((sub, tq), BF16),
                        pltpu.VMEM((sub, tq), BF16), pltpu.VMEM((sub, tq), BF16)],
        compiler_params=_cparams(("parallel", "arbitrary"), 56),
        name="peer_experts",
    )(xn, u_bf16, vt_bf16, *tables, h)


def _ple_kernel(h_ref, g_ref, p_ref, wg_ref, wu_ref, out_ref):
    h = h_ref[...]
    gate = jax.nn.sigmoid(jnp.dot(_rms(h, g_ref[...]).astype(BF16), wg_ref[...], preferred_element_type=F32))
    up = jnp.dot(p_ref[...].astype(BF16), wu_ref[...], preferred_element_type=F32)
    out_ref[...] = h + up * gate


def _ple(h, g, p, wg_bf16, wu_bf16, *, tm=512):
    T, D = h.shape
    P = p.shape[1]
    return pl.pallas_call(
        _ple_kernel,
        grid=(T // tm,),
        in_specs=[pl.BlockSpec((tm, D), lambda i: (i, 0)), pl.BlockSpec((1, D), lambda i: (0, 0)),
                  pl.BlockSpec((tm, P), lambda i: (i, 0)), pl.BlockSpec((D, D), lambda i: (0, 0)),
                  pl.BlockSpec((P, D), lambda i: (0, 0))],
        out_specs=pl.BlockSpec((tm, D), lambda i: (i, 0)),
        out_shape=jax.ShapeDtypeStruct((T, D), F32),
        compiler_params=_cparams(("parallel",), 32),
        name="ple",
    )(h, g.reshape(1, D), p, wg_bf16, wu_bf16)


def _tile_gain(g, scale=1.0):
    return jnp.tile(g.astype(F32) * scale, LANES // HEAD_DIM)


def _ab_mixer(h, rope, B, S, g, w_in, w_out, a_q, a_k, b_q, b_k):
    A, Bw = A_HEADS * HEAD_DIM, B_HEADS * HEAD_DIM
    cuts = np.cumsum([A, A, A, Bw, Bw, Bw, IDX_HEADS * HEAD_DIM, HEAD_DIM, IDX_HEADS])
    base, ki0, wi0 = w_in[:, :cuts[6]], w_in[:, cuts[6]:cuts[7]], w_in[:, cuts[7]:cuts[8]]
    w_pad = jnp.concatenate([base, ki0, ki0, wi0, jnp.zeros((D_MODEL, LANES - IDX_HEADS), F32)], axis=1).astype(BF16)
    scale = HEAD_DIM ** -0.5 * LOG2E
    kinds = ["nr"] * 8 + ["p"] * 4 + ["nr"] * 8 + ["p"] * 4 + ["r"] * 3 + ["pa"]
    one = jnp.ones((LANES,), F32)
    gains = jnp.stack([_tile_gain(a_q, scale)] * 4 + [_tile_gain(a_k)] * 4 + [one] * 4
                      + [_tile_gain(b_q, scale)] * 4 + [_tile_gain(b_k)] * 4 + [one] * 8)
    y, aux = _project(h, g, w_pad, gains, rope, kinds)
    kmean = _block_means(y, 4, 4, B, S)
    tq = min(1024, S)
    oa = _flash("moba", y, y, y, 0, 4, 8, 4, B, S, tq=tq, ts=512, tk=512, extra=kmean)
    bias = _dsa_select(y, aux, 24, 26, B, S)
    ob = _flash("dsa", y, y, y, 12, 16, 20, 4, B, S, tq=tq, ts=512, tk=512, extra=bias)
    return _out_proj([oa, ob], w_out.astype(BF16), h)


def _c_mixer(h, rope, B, S, g, w_in, w_out, q_norm, k_norm):
    scale = HEAD_DIM ** -0.5 * LOG2E
    kinds = ["nr"] * 16 + ["p"] * 7 + ["pa"]
    one = jnp.ones((LANES,), F32)
    gains = jnp.stack([_tile_gain(q_norm, scale)] * 8 + [_tile_gain(k_norm)] * 8 + [one] * 8)
    y, _ = _project(h, g, w_in.astype(BF16), gains, rope, kinds)
    o = _flash("dilated", y, y, y, 0, 8, 16, 8, B, S, tq=512, tk=512)
    return _out_proj([o], w_out.astype(BF16), h)


def _peer(h, g, w_query, sub_keys, expert_u, expert_v):
    sk = sub_keys.reshape(2 * PEER_HEADS, N_KEYS, N_KEYS).astype(BF16)
    xn, *tables = _peer_route(h, g, w_query.astype(BF16), sk)
    return _peer_experts(xn, expert_u.astype(BF16), expert_v.T.astype(BF16), tables, h)


def kernel(x, p, positions, attn_norm, ffn_norm, ple_norm, ab_w_in, ab_w_out, a_q_norm, a_k_norm, b_q_norm,
           b_k_norm, c_w_in, c_w_out, c_q_norm, c_k_norm, peer_w_query, peer_sub_keys, peer_u, peer_v,
           ple_w_gate, ple_w_up):
    B, S, D = x.shape
    depth = p.shape[0]
    rope = _rope_tables(positions)
    h = x.reshape(B * S, D)
    for i in range(depth):
        j = i // 2
        if i % 2 == 0:
            h = _ab_mixer(h, rope, B, S, attn_norm[i], ab_w_in[j], ab_w_out[j],
                          a_q_norm[j], a_k_norm[j], b_q_norm[j], b_k_norm[j])
        else:
            h = _c_mixer(h, rope, B, S, attn_norm[i], c_w_in[j], c_w_out[j], c_q_norm[j], c_k_norm[j])
        h = _peer(h, ffn_norm[i], peer_w_query[i], peer_sub_keys[i], peer_u[i], peer_v[i])
        h = _ple(h, ple_norm[i], p[i].reshape(B * S, -1), ple_w_gate[i].astype(BF16), ple_w_up[i].astype(BF16))
    return h.reshape(B, S, D)
```

```python
import functools
import math

import numpy as np
import jax
import jax.numpy as jnp
from jax import lax
from jax.experimental import pallas as pl
from jax.experimental.pallas import tpu as pltpu

F32 = jnp.float32
BF16 = jnp.bfloat16

D_MODEL = 1024
HEAD_DIM = 64
ROT_HALF = 8
ROPE_THETA = 500000.0
NORM_EPS = 1e-6
NEG = -1e30
LOG2E = math.log2(math.e)
LANES = 128

A_HEADS = 8
B_HEADS = 8
C_HEADS = 16
IDX_HEADS = 4
MOBA_BLOCK = 256
MOBA_TOPK = 3
DSA_TOPK = 256
DILATED_PAIRS = ((128, 1), (512, 4), (2048, 16))
PEER_HEADS = 8
N_KEYS = 128
PEER_TOPK = 16

V7X_VMEM_BYTES = 64 * 1024 * 1024


def _cparams(sem, vmem_mb):
    assert vmem_mb * 2**20 < V7X_VMEM_BYTES
    return pltpu.CompilerParams(dimension_semantics=sem, vmem_limit_bytes=vmem_mb * 2**20)


def _rms(x, g):
    ms = jnp.mean(x * x, axis=-1, keepdims=True)
    return x * lax.rsqrt(ms + NORM_EPS) * g


def _nt_dot(a, b):
    return lax.dot_general(a, b, (((1,), (1,)), ((), ())), preferred_element_type=F32)


def _split_bf16(x):
    hi = x.astype(BF16)
    lo = (x - hi.astype(F32)).astype(BF16)
    return hi, lo


def _proj_kernel(x_ref, g_ref, w_ref, gains_ref, cos_ref, sa_ref, sb_ref, y_ref, aux_ref, *, kinds, group):
    xn = _rms(x_ref[...], g_ref[...]).astype(BF16)
    row = lax.broadcasted_iota(jnp.int32, (LANES, LANES), 0) // HEAD_DIM
    col = lax.broadcasted_iota(jnp.int32, (LANES, LANES), 1) // HEAD_DIM
    head_avg = jnp.where(row == col, 1.0 / HEAD_DIM, 0.0).astype(BF16)
    n_chunks = len(kinds)
    for g0 in range(0, n_chunks, group):
        g1 = min(g0 + group, n_chunks)
        yg = jnp.dot(xn, w_ref[:, g0 * LANES:g1 * LANES], preferred_element_type=F32)
        for c in range(g0, g1):
            y = yg[:, (c - g0) * LANES:(c - g0 + 1) * LANES]
            kind = kinds[c]
            if "n" in kind:
                hi, lo = _split_bf16(y * y)
                msq = (jnp.dot(hi, head_avg, preferred_element_type=F32)
                       + jnp.dot(lo, head_avg, preferred_element_type=F32))
                y = y * lax.rsqrt(msq + NORM_EPS) * gains_ref[c:c + 1, :]
            if "r" in kind:
                y = (y * cos_ref[...] + pltpu.roll(y, LANES - ROT_HALF, 1) * sa_ref[...]
                     + pltpu.roll(y, ROT_HALF, 1) * sb_ref[...])
            if "a" in kind:
                aux_ref[...] = y
            y_ref[c] = y.astype(BF16)


def _project(h, g, w_bf16, gains, rope, kinds, *, tm=512, group=4):
    T, D = h.shape
    n_chunks = len(kinds)
    assert w_bf16.shape == (D, n_chunks * LANES) and T % tm == 0
    cos, sa, sb = rope
    tok = lambda i: (i, 0)
    fixed = lambda i: (0, 0)
    return pl.pallas_call(
        functools.partial(_proj_kernel, kinds=tuple(kinds), group=group),
        grid=(T // tm,),
        in_specs=[pl.BlockSpec((tm, D), tok), pl.BlockSpec((1, D), fixed),
                  pl.BlockSpec((D, n_chunks * LANES), fixed), pl.BlockSpec((n_chunks, LANES), fixed),
                  pl.BlockSpec((tm, LANES), tok), pl.BlockSpec((tm, LANES), tok), pl.BlockSpec((tm, LANES), tok)],
        out_specs=[pl.BlockSpec((n_chunks, tm, LANES), lambda i: (0, i, 0)), pl.BlockSpec((tm, LANES), tok)],
        out_shape=[jax.ShapeDtypeStruct((n_chunks, T, LANES), BF16), jax.ShapeDtypeStruct((T, LANES), F32)],
        compiler_params=_cparams(("parallel",), 48),
        name="proj",
    )(h, g.reshape(1, D), w_bf16, gains, cos, sa, sb)


def _rope_tables(positions):
    B, S = positions.shape
    inv = ROPE_THETA ** (-jnp.arange(ROT_HALF, dtype=F32) / ROT_HALF)
    ang = positions.astype(F32)[..., None] * inv
    cos, sin = jnp.cos(ang), jnp.sin(ang)
    rest = HEAD_DIM - 2 * ROT_HALF
    one, zero, z8 = jnp.ones((B, S, rest), F32), jnp.zeros((B, S, rest), F32), jnp.zeros((B, S, ROT_HALF), F32)
    tile = lambda parts: jnp.tile(jnp.concatenate(parts, axis=-1), (1, 1, LANES // HEAD_DIM)).reshape(B * S, LANES)
    return tile([cos, cos, one]), tile([-sin, z8, zero]), tile([z8, sin, zero])


def _kmean_kernel(k_ref, o_ref):
    n = pl.program_id(1)

    @pl.when(n == 0)
    def _():
        o_ref[...] = jnp.zeros_like(o_ref)

    o_ref[:, pl.ds(n, 1), :] = jnp.mean(k_ref[...].astype(F32), axis=1, keepdims=True)


def _block_means(y, k_off, pairs, B, S):
    nb = S // MOBA_BLOCK
    assert nb <= LANES and k_off % pairs == 0
    return pl.pallas_call(
        _kmean_kernel,
        grid=(B, nb),
        in_specs=[pl.BlockSpec((pairs, MOBA_BLOCK, LANES), lambda b, n: (k_off // pairs, b * nb + n, 0))],
        out_specs=pl.BlockSpec((pairs, LANES, LANES), lambda b, n: (0, b, 0)),
        out_shape=jax.ShapeDtypeStruct((pairs, B * LANES, LANES), F32),
        compiler_params=_cparams(("parallel", "arbitrary"), 16),
        name="moba_kmean",
    )(y)


MASK_BIG = 32768.0


def _flash_kernel(*refs, mode, pairs, tq, ts, tk, n_steps, back):
    q_ref, k_ref, v_ref, x_ref, o_ref, m_scr, acc_scr, qa_scr = refs[:8]
    s_scr, p_scr, alpha_scr = refs[8:10], refs[10:12], refs[12:14]
    if mode == "dilated":
        b_ref = x_ref
    elif mode == "dsa":
        b_ref, bias_scr = x_ref, refs[14]
    else:
        km_ref, bias_scr, oh_scr = x_ref, refs[14], refs[15]
    i = pl.program_id(1)
    j = pl.program_id(2)
    lane = lax.broadcasted_iota(jnp.int32, (1, LANES), 1)
    halves = (lane < HEAD_DIM, lane >= HEAD_DIM)
    if mode == "dilated":
        kt = i * (tq // tk) - back + j
        active = kt >= 0
    else:
        kt = j
        active = j * tk < (i + 1) * tq
    tpos = i * tq + lax.broadcasted_iota(jnp.int32, (tq, 1), 0)

    n_heads = 2 * pairs
    n_sub = tq // ts
    n_units = n_sub * n_heads

    @pl.when(j == 0)
    def _():
        m_scr[...] = jnp.full_like(m_scr, NEG)
        acc_scr[...] = jnp.zeros_like(acc_scr)
        blk = lax.broadcasted_iota(jnp.int32, (ts, LANES), 1)
        for sub in range(n_sub):
            rows = slice(sub * ts, (sub + 1) * ts)
            qblk = tpos[rows] // MOBA_BLOCK
            for c in range(pairs):
                if mode == "moba":
                    km_hi, km_lo = _split_bf16(km_ref[c])
                for half in range(2):
                    qm = jnp.where(halves[half], q_ref[c, rows, :], jnp.zeros((), BF16))
                    if mode == "moba":
                        gate = jnp.where(blk < qblk, _nt_dot(qm, km_hi) + _nt_dot(qm, km_lo), NEG)
                        rest, thr = gate, None
                        for _ in range(MOBA_TOPK):
                            thr = jnp.max(rest, axis=1, keepdims=True)
                            rest = jnp.where(rest >= thr, NEG, rest)
                        attended = ((gate >= thr) & (blk < qblk)) | (blk == qblk)
                        feat = jnp.where(attended, 0.0, -MASK_BIG)
                        if half == 0:
                            feat = pltpu.roll(feat, HEAD_DIM, 1)
                        qm = jnp.where(halves[half], qm, feat.astype(BF16))
                    qa_scr[sub * n_heads + 2 * c + half] = qm

    def attend(bias):
        if mode == "moba":
            kblk = (kt * tk + lax.broadcasted_iota(jnp.int32, (tk, 1), 0)) // MOBA_BLOCK
            oh_scr[0] = jnp.where(lane == kblk + HEAD_DIM, 1.0, 0.0).astype(BF16)
            oh_scr[1] = jnp.where(lane == kblk, 1.0, 0.0).astype(BF16)

        def scores(u, par):
            kp = k_ref[(u % n_heads) // 2]
            if mode == "moba":
                kp = jnp.where(halves[par], kp, oh_scr[par])
            s_scr[par][...] = _nt_dot(qa_scr[u], kp)

        def bias_rows(u):
            start = u // n_heads * ts
            if isinstance(start, int):
                return bias[start:start + ts, :]
            return bias[pl.ds(pl.multiple_of(start, ts), ts), :]

        def softmax(u, par):
            s = s_scr[par][...]
            if bias is not None:
                s = s + (bias[...] if n_sub == 1 else bias_rows(u))
            m_old = m_scr[u]
            m_new = jnp.maximum(m_old, jnp.max(s, axis=1, keepdims=True))
            p_scr[par][...] = jnp.exp2(s - m_new).astype(BF16)
            alpha_scr[par][...] = jnp.exp2(m_old - m_new)
            m_scr[u] = m_new

        def values(u, par):
            va = jnp.where(halves[par], v_ref[(u % n_heads) // 2], jnp.ones((), BF16))
            acc_scr[u] = alpha_scr[par][...] * acc_scr[u] + jnp.dot(p_scr[par][...], va, preferred_element_type=F32)

        scores(0, 0)
        scores(1, 1)
        softmax(0, 0)

        def body(c, carry):
            scores(2 * c + 2, 0)
            softmax(2 * c + 1, 1)
            values(2 * c, 0)
            scores(2 * c + 3, 1)
            softmax(2 * c + 2, 0)
            values(2 * c + 1, 1)
            return carry

        lax.fori_loop(0, n_units // 2 - 1, body, 0)
        softmax(n_units - 1, 1)
        values(n_units - 2, 0)
        values(n_units - 1, 1)

    if mode == "moba":
        on_diagonal = (kt + 1) * tk > i * tq

        @pl.when(active & on_diagonal)
        def _():
            kpos = kt * tk + lax.broadcasted_iota(jnp.int32, (tq, tk), 1)
            bias_scr[...] = jnp.where(kpos <= tpos, 0.0, -jnp.inf)
            attend(bias_scr)

        @pl.when(active & jnp.logical_not(on_diagonal))
        def _():
            attend(None)
    elif mode == "dsa":
        @pl.when(active)
        def _():
            for part in range(b_ref.shape[1]):
                width = b_ref.shape[3]
                bias_scr[:, part * width:(part + 1) * width] = b_ref[0, part].astype(F32)
            attend(bias_scr)
    else:
        @pl.when(active)
        def _():
            attend(b_ref.at[0])

    @pl.when(j == n_steps - 1)
    def _():
        for sub in range(n_sub):
            for c in range(pairs):
                a0, a1 = acc_scr[sub * n_heads + 2 * c], acc_scr[sub * n_heads + 2 * c + 1]
                num = jnp.where(halves[0], a0, a1)
                den = pltpu.roll(jnp.where(halves[0], a1, a0), HEAD_DIM, 1)
                o_ref[c, sub * ts:(sub + 1) * ts, :] = (num / den).astype(BF16)


def _dilated_bias_table(tq, tk, back, n_steps):
    r = np.arange(tq)[:, None]
    c = np.arange(tk)[None, :]
    tabs = []
    for j in range(n_steps):
        d = (back - j) * tk + r - c
        mult = sum(((d >= 0) & (d <= w) & (d % dil == 0)).astype(np.float64) for w, dil in DILATED_PAIRS)
        tabs.append(np.where(mult > 0, np.log2(np.maximum(mult, 1.0)), -np.inf))
    return jnp.asarray(np.stack(tabs), F32)


def _flash(mode, q_arr, k_arr, v_arr, q_off, k_off, v_off, pairs, B, S, *, tq, tk, ts=None, extra=None):
    ts = ts or tq
    nq, nk = S // tq, S // tk
    n_units = tq // ts * 2 * pairs
    assert S % tq == 0 and S % tk == 0 and tq % tk == 0 and tq % ts == 0
    assert q_off % pairs == 0 and k_off % pairs == 0 and v_off % pairs == 0
    if mode == "dilated":
        max_back = max(w for w, _ in DILATED_PAIRS)
        back = -(-max_back // tk)
        n_steps = back + tq // tk
        kv_tile = lambda i, j: jnp.maximum(i * (tq // tk) - back + j, 0)
    else:
        back = 0
        n_steps = nk
        kv_tile = lambda i, j: jnp.minimum(j, ((i + 1) * tq - 1) // tk)
    in_specs = [pl.BlockSpec((pairs, tq, LANES), lambda b, i, j: (q_off // pairs, b * nq + i, 0)),
                pl.BlockSpec((pairs, tk, LANES), lambda b, i, j: (k_off // pairs, b * nk + kv_tile(i, j), 0)),
                pl.BlockSpec((pairs, tk, LANES), lambda b, i, j: (v_off // pairs, b * nk + kv_tile(i, j), 0))]
    args = [q_arr, k_arr, v_arr]
    scratch = [pltpu.VMEM((n_units, ts, 1), F32), pltpu.VMEM((n_units, ts, LANES), F32),
               pltpu.VMEM((n_units, ts, LANES), BF16), pltpu.VMEM((ts, tk), F32), pltpu.VMEM((ts, tk), F32),
               pltpu.VMEM((ts, tk), BF16), pltpu.VMEM((ts, tk), BF16),
               pltpu.VMEM((ts, 1), F32), pltpu.VMEM((ts, 1), F32)]
    if mode == "moba":
        assert tk % MOBA_BLOCK == 0 and S // MOBA_BLOCK <= HEAD_DIM
        in_specs.append(pl.BlockSpec((pairs, LANES, LANES), lambda b, i, j: (0, b, 0)))
        args.append(extra)
        scratch += [pltpu.VMEM((tq, tk), F32), pltpu.VMEM((2, tk, LANES), BF16)]
    elif mode == "dsa":
        ck = extra.shape[3]
        assert tk % ck == 0
        in_specs.append(pl.BlockSpec((1, tk // ck, tq, ck), lambda b, i, j: (b, kv_tile(i, j), i, 0)))
        args.append(extra)
        scratch.append(pltpu.VMEM((tq, tk), F32))
    else:
        in_specs.append(pl.BlockSpec((1, tq, tk), lambda b, i, j: (j, 0, 0)))
        args.append(_dilated_bias_table(tq, tk, back, n_steps))
    return pl.pallas_call(
        functools.partial(_flash_kernel, mode=mode, pairs=pairs, tq=tq, ts=ts, tk=tk, n_steps=n_steps, back=back),
        grid=(B, nq, n_steps),
        in_specs=in_specs,
        out_specs=pl.BlockSpec((pairs, tq, LANES), lambda b, i, j: (0, b * nq + i, 0)),
        out_shape=jax.ShapeDtypeStruct((pairs, B * S, LANES), BF16),
        scratch_shapes=scratch,
        compiler_params=_cparams(("parallel", "parallel", "arbitrary"), 48),
        name="flash_" + mode,
    )(*args)


def _row_counts(src, pred, n_live, r0, rs, ck, dtype):
    assert (src.shape[0] * ck) // LANES <= 256

    def body(c, acc):
        hit = jnp.where(pred(src[c, r0:r0 + rs, :], c), jnp.ones((), dtype), jnp.zeros((), dtype))
        for s in range(ck // LANES):
            acc = acc + hit[:, s * LANES:(s + 1) * LANES]
        return acc

    acc = lax.fori_loop(0, n_live, body, jnp.zeros((rs, LANES), dtype))
    return jnp.sum(acc.astype(F32), axis=1, keepdims=True)


def _dsa_threshold_upper(top_scr, n_live, *, rows, ck, topk):
    def upper_bit(it, lo):
        cand = lo + jnp.left_shift(jnp.int32(1), 15 - it)
        pattern = jnp.where(cand < 0, cand ^ jnp.int32(0x7FFF), cand)
        pattern = jnp.where((pattern > 0) & (pattern < 0x80), jnp.int32(0x80), pattern)
        value = pltpu.bitcast(jnp.left_shift(pattern, 16), F32).astype(BF16)
        cnt = _row_counts(top_scr, lambda sc, c: sc >= value, n_live, 0, rows, ck, BF16)
        return jnp.where(cnt >= topk, cand, lo)

    return lax.fori_loop(0, 16, upper_bit, jnp.full((rows, 1), -2**15, jnp.int32))


def _dsa_select_strip(key_scr, tri_scr, o_ref, tpos, upper, n_live, *, r0, rs, ck, topk):
    def count(pred):
        return _row_counts(key_scr, pred, n_live, r0, rs, ck, F32)

    def lower_bit(it, lo):
        cand = lo + jnp.left_shift(jnp.int32(1), 15 - it)
        return jnp.where(count(lambda kc, c: kc >= cand) >= topk, cand, lo)

    thr = lax.fori_loop(0, 16, lower_bit, jnp.left_shift(upper, 16))
    need = topk - count(lambda kc, c: kc > thr)

    def emit(c, before):
        kc = key_scr[c, r0:r0 + rs, :]
        kpos = c * ck + lax.broadcasted_iota(jnp.int32, (rs, ck), 1)
        equal = kc == thr
        rank = before + jnp.dot(jnp.where(equal, 1.0, 0.0).astype(BF16), tri_scr[...], preferred_element_type=F32)
        take = ((kc > thr) | (equal & (rank <= need))) & (kpos <= tpos)
        o_ref[0, c, r0:r0 + rs, :] = jnp.where(take, 0.0, -jnp.inf).astype(BF16)
        return rank[:, ck - 1:ck]

    lax.fori_loop(0, n_live, emit, jnp.zeros((rs, 1), F32))


def _dsa_select_kernel(qi_ref, ki_ref, w_ref, o_ref, key_scr, top_scr, tri_scr, *, tq, ck, rs, n_chunks, topk):
    i = pl.program_id(1)
    tri_scr[...] = jnp.where(lax.broadcasted_iota(jnp.int32, (ck, ck), 0) <= lax.broadcasted_iota(jnp.int32, (ck, ck), 1),
                             1.0, 0.0).astype(BF16)
    lane = lax.broadcasted_iota(jnp.int32, (1, LANES), 1)
    halves = (lane < HEAD_DIM, lane >= HEAD_DIM)
    tpos = i * tq + lax.broadcasted_iota(jnp.int32, (tq, 1), 0)
    n_live = ((i + 1) * tq + ck - 1) // ck
    w = w_ref[...]

    def score_chunk(c, carry):
        kc = ki_ref[0, pl.ds(pl.multiple_of(c * ck, ck), ck), :]
        score = jnp.zeros((tq, ck), F32)
        for h in range(IDX_HEADS):
            qm = jnp.where(halves[h % 2], qi_ref[h // 2], jnp.zeros((), BF16))
            score = score + w[:, h:h + 1] * jnp.maximum(_nt_dot(qm, kc), 0.0)
        kpos = c * ck + lax.broadcasted_iota(jnp.int32, (tq, ck), 1)
        score = jnp.where(jnp.abs(score) >= jnp.finfo(F32).tiny, score, 0.0)
        score = jnp.where(kpos <= tpos, score, NEG)
        bits = pltpu.bitcast(score, jnp.int32)
        key_scr[c] = jnp.where(bits < 0, bits ^ jnp.int32(0x7FFFFFFF), bits)
        top_scr[c] = pltpu.bitcast(bits & jnp.int32(-65536), F32).astype(BF16)
        return carry

    lax.fori_loop(0, n_live, score_chunk, 0)

    upper = _dsa_threshold_upper(top_scr, n_live, rows=tq, ck=ck, topk=topk)
    for r0 in range(0, tq, rs):
        _dsa_select_strip(key_scr, tri_scr, o_ref, tpos[r0:r0 + rs], upper[r0:r0 + rs], n_live,
                          r0=r0, rs=rs, ck=ck, topk=topk)

    def blank(c, carry):
        o_ref[0, c] = jnp.full((tq, ck), -jnp.inf, BF16)
        return carry

    lax.fori_loop(n_live, n_chunks, blank, 0)


def _dsa_select(y, aux, qi_off, ki_off, B, S, *, tq=256, ck=512, rs=128):
    n_chunks = S // ck
    assert S % ck == 0 and S % tq == 0 and tq % rs == 0 and qi_off % 2 == 0
    nq = S // tq
    topk = min(DSA_TOPK, S // 4)
    return pl.pallas_call(
        functools.partial(_dsa_select_kernel, tq=tq, ck=ck, rs=rs, n_chunks=n_chunks, topk=topk),
        grid=(B, nq),
        in_specs=[pl.BlockSpec((2, tq, LANES), lambda b, i: (qi_off // 2, b * nq + i, 0)),
                  pl.BlockSpec((1, S, LANES), lambda b, i: (ki_off, b, 0)),
                  pl.BlockSpec((tq, LANES), lambda b, i: (b * nq + i, 0))],
        out_specs=pl.BlockSpec((1, n_chunks, tq, ck), lambda b, i: (b, 0, i, 0)),
        out_shape=jax.ShapeDtypeStruct((B, n_chunks, S, ck), BF16),
        scratch_shapes=[pltpu.VMEM((n_chunks, tq, ck), jnp.int32), pltpu.VMEM((n_chunks, tq, ck), BF16),
                        pltpu.VMEM((ck, ck), BF16)],
        compiler_params=_cparams(("parallel", "parallel"), 48),
        name="dsa_select",
    )(y, y, aux)


def _out_proj_kernel(*refs, n_in):
    o_refs, w_ref, h_ref, out_ref = refs[:n_in], refs[n_in], refs[n_in + 1], refs[n_in + 2]
    slabs = [o[c] for o in o_refs for c in range(o.shape[0])]
    o = jnp.concatenate(slabs, axis=-1)
    out_ref[...] = h_ref[...] + jnp.dot(o, w_ref[...], preferred_element_type=F32)


def _out_proj(os_, w_bf16, h, *, tm=512):
    T, D = h.shape
    width = sum(o.shape[0] for o in os_) * LANES
    assert w_bf16.shape == (width, D)
    in_specs = [pl.BlockSpec((o.shape[0], tm, LANES), lambda i: (0, i, 0)) for o in os_]
    in_specs += [pl.BlockSpec((width, D), lambda i: (0, 0)), pl.BlockSpec((tm, D), lambda i: (i, 0))]
    return pl.pallas_call(
        functools.partial(_out_proj_kernel, n_in=len(os_)),
        grid=(T // tm,),
        in_specs=in_specs,
        out_specs=pl.BlockSpec((tm, D), lambda i: (i, 0)),
        out_shape=jax.ShapeDtypeStruct((T, D), F32),
        compiler_params=_cparams(("parallel",), 32),
        name="out_proj",
    )(*os_, w_bf16, h)


def _top_values(s, k):
    out = []
    for _ in range(k + 1):
        cur = jnp.max(s, axis=0, keepdims=True)
        out.append(cur)
        s = jnp.where(s >= cur, -jnp.inf, s)
    return jnp.concatenate(out[:k], axis=0), out[k]


def _pair_candidates(t1, t2):
    k = PEER_TOPK
    parts = [t1[0:1] + t2]
    parts += [t1[a:a + 1] + t2[0:k // 2] for a in range(1, k // 2)]
    parts.append(t1[k // 2:k] + t2[0:1])
    return jnp.concatenate(parts, axis=0)


def _kth_largest(c, k):
    left = jnp.full((1, c.shape[1]), float(k), F32)
    tau = jnp.zeros((1, c.shape[1]), F32)
    for _ in range(k):
        cur = jnp.max(c, axis=0, keepdims=True)
        hit = c >= cur
        tau = jnp.where(left > 0.0, cur, tau)
        left = left - jnp.sum(jnp.where(hit, 1.0, 0.0), axis=0, keepdims=True)
        c = jnp.where(hit, -jnp.inf, c)
    return tau


def _peer_route_kernel(h_ref, g_ref, wq_ref, sk_ref, xn_ref, th1_ref, e1_ref, e2_ref):
    xn = _rms(h_ref[...], g_ref[...]).astype(BF16)
    xn_ref[...] = xn
    for hd in range(PEER_HEADS):
        q = jnp.dot(xn, wq_ref[:, hd * 2 * N_KEYS:(hd + 1) * 2 * N_KEYS], preferred_element_type=F32).astype(BF16)
        s1 = _nt_dot(sk_ref[2 * hd], q[:, :N_KEYS])
        s2 = _nt_dot(sk_ref[2 * hd + 1], q[:, N_KEYS:])
        s1 = s1 - jnp.max(s1, axis=0, keepdims=True)
        s2 = s2 - jnp.max(s2, axis=0, keepdims=True)
        (t1, next1), (t2, next2) = _top_values(s1, PEER_TOPK), _top_values(s2, PEER_TOPK)
        cand = _pair_candidates(t1, t2)
        chosen = cand >= _kth_largest(cand, PEER_TOPK)
        log_z = jnp.log(jnp.sum(jnp.where(chosen, jnp.exp(cand), 0.0), axis=0, keepdims=True))
        runner_up = jnp.maximum(jnp.max(jnp.where(chosen, -jnp.inf, cand), axis=0, keepdims=True),
                                jnp.maximum(next1 + t2[0:1], t1[0:1] + next2))
        smallest = jnp.min(jnp.where(chosen, cand, jnp.inf), axis=0, keepdims=True)
        tau = 0.5 * (smallest + runner_up)
        s1 = s1 - log_z
        th1_ref[hd] = jnp.exp((tau - log_z) - s1)
        e1_ref[hd] = jnp.exp(s1)
        e2_ref[hd] = jnp.exp(s2)


def _peer_route(h, g, wq_bf16, sk_bf16, *, tm=256):
    T, D = h.shape
    nq = PEER_HEADS * 2 * N_KEYS
    assert wq_bf16.shape == (D, nq) and sk_bf16.shape == (2 * PEER_HEADS, N_KEYS, N_KEYS)
    return pl.pallas_call(
        _peer_route_kernel,
        grid=(T // tm,),
        in_specs=[pl.BlockSpec((tm, D), lambda i: (i, 0)), pl.BlockSpec((1, D), lambda i: (0, 0)),
                  pl.BlockSpec((D, nq), lambda i: (0, 0)),
                  pl.BlockSpec((2 * PEER_HEADS, N_KEYS, N_KEYS), lambda i: (0, 0, 0))],
        out_specs=[pl.BlockSpec((tm, D), lambda i: (i, 0))]
        + [pl.BlockSpec((PEER_HEADS, N_KEYS, tm), lambda i: (0, 0, i))] * 3,
        out_shape=[jax.ShapeDtypeStruct((T, D), BF16)]
        + [jax.ShapeDtypeStruct((PEER_HEADS, N_KEYS, T), F32)] * 3,
        compiler_params=_cparams(("parallel",), 40),
        name="peer_route",
    )(h, g.reshape(1, D), wq_bf16, sk_bf16)


def _gelu_tanh(x):
    return 0.5 * x * (1.0 + jnp.tanh(math.sqrt(2.0 / math.pi) * (x + 0.044715 * (x * x * x))))


def _peer_expert_kernel(xn_ref, u_ref, vt_ref, th1_ref, e1_ref, e2_ref, h_ref, out_ref,
                        acc_scr, rows_scr, act0_scr, act1_scr, a0_scr, a1_scr, *, eb, sub, tq):
    e = pl.program_id(1)
    rows, sub_rows, n_sub = eb // N_KEYS, sub // N_KEYS, eb // sub
    sublanes = rows_scr.shape[2]
    act_scr, a_scr = (act0_scr, act1_scr), (a0_scr, a1_scr)

    @pl.when(e == 0)
    def _():
        acc_scr[...] = jnp.zeros_like(acc_scr)

    row0 = pl.multiple_of(e * rows, rows)
    for k, tab in enumerate((th1_ref, e1_ref)):
        for hd in range(PEER_HEADS):
            block = tab[hd, pl.ds(row0, rows), :]
            for r in range(rows):
                rows_scr[k, hd * rows + r] = jnp.broadcast_to(block[r:r + 1, :], (sublanes, tq))

    def activations(sb):
        act_scr[sb % 2][...] = _gelu_tanh(_nt_dot(u_ref[sb * sub:(sb + 1) * sub, :], xn_ref[...]).astype(BF16))

    def weigh(sb):
        span = 64
        reps = (span // sublanes, 1)
        for r in range(sub_rows):
            for t0 in range(0, tq, LANES):
                for i2 in range(0, N_KEYS, span):
                    w = jnp.zeros((span, LANES), BF16)
                    for hd in range(PEER_HEADS):
                        k = hd * rows + sb * sub_rows + r
                        th1 = jnp.tile(rows_scr[0, k, :, t0:t0 + LANES], reps)
                        e1 = jnp.tile(rows_scr[1, k, :, t0:t0 + LANES], reps).astype(BF16)
                        e2 = e2_ref[hd, i2:i2 + span, t0:t0 + LANES]
                        w = w + e1 * jnp.where(e2 >= th1, e2, 0.0).astype(BF16)
                    lo = r * N_KEYS + i2
                    a_scr[sb % 2][lo:lo + span, t0:t0 + LANES] = act_scr[sb % 2][lo:lo + span, t0:t0 + LANES] * w

    def project(sb):
        acc_scr[...] += jnp.dot(vt_ref[:, sb * sub:(sb + 1) * sub], a_scr[sb % 2][...], preferred_element_type=F32)

    activations(0)
    for sb in range(n_sub):
        if sb + 1 < n_sub:
            activations(sb + 1)
        if sb >= 1:
            project(sb - 1)
        weigh(sb)
    project(n_sub - 1)

    @pl.when(e == pl.num_programs(1) - 1)
    def _():
        out_ref[...] = h_ref[...] + acc_scr[...].T


F32_SUBLANES = 8


def _peer_experts(xn, u_bf16, vt_bf16, tables, h, *, tq=512, eb=2048, sub=1024):
    T, D = h.shape
    E = u_bf16.shape[0]
    assert E == N_KEYS * N_KEYS and E % eb == 0 and eb % (F32_SUBLANES * N_KEYS) == 0 and T % tq == 0
    assert eb % sub == 0 and sub % N_KEYS == 0
    table_spec = pl.BlockSpec((PEER_HEADS, N_KEYS, tq), lambda i, e: (0, 0, i))
    return pl.pallas_call(
        functools.partial(_peer_expert_kernel, eb=eb, sub=sub, tq=tq),
        grid=(T // tq, E // eb),
        in_specs=[pl.BlockSpec((tq, D), lambda i, e: (i, 0)),
                  pl.BlockSpec((eb, D), lambda i, e: (e, 0)),
                  pl.BlockSpec((D, eb), lambda i, e: (0, e)),
                  table_spec, table_spec, table_spec,
                  pl.BlockSpec((tq, D), lambda i, e: (i, 0))],
        out_specs=pl.BlockSpec((tq, D), lambda i, e: (i, 0)),
        out_shape=jax.ShapeDtypeStruct((T, D), F32),
        scratch_shapes=[pltpu.VMEM((D, tq), F32),
                        pltpu.VMEM((2, PEER_HEADS * (eb // N_KEYS), F32_SUBLANES, tq), F32),
                        pltpu.VMEM((sub, tq), BF16), pltpu.VMEM((sub, tq), BF16),
                        pltpu.VMEM((sub, tq), BF16), pltpu.VMEM((sub, tq), BF16)],
        compiler_params=_cparams(("parallel", "arbitrary"), 56),
        name="peer_experts",
    )(xn, u_bf16, vt_bf16, *tables, h)


def _ple_kernel(h_ref, g_ref, p_ref, wg_ref, wu_ref, out_ref):
    h = h_ref[...]
    gate = jax.nn.sigmoid(jnp.dot(_rms(h, g_ref[...]).astype(BF16), wg_ref[...], preferred_element_type=F32))
    up = jnp.dot(p_ref[...].astype(BF16), wu_ref[...], preferred_element_type=F32)
    out_ref[...] = h + up * gate


def _ple(h, g, p, wg_bf16, wu_bf16, *, tm=512):
    T, D = h.shape
    P = p.shape[1]
    return pl.pallas_call(
        _ple_kernel,
        grid=(T // tm,),
        in_specs=[pl.BlockSpec((tm, D), lambda i: (i, 0)), pl.BlockSpec((1, D), lambda i: (0, 0)),
                  pl.BlockSpec((tm, P), lambda i: (i, 0)), pl.BlockSpec((D, D), lambda i: (0, 0)),
                  pl.BlockSpec((P, D), lambda i: (0, 0))],
        out_specs=pl.BlockSpec((tm, D), lambda i: (i, 0)),
        out_shape=jax.ShapeDtypeStruct((T, D), F32),
        compiler_params=_cparams(("parallel",), 32),
        name="ple",
    )(h, g.reshape(1, D), p, wg_bf16, wu_bf16)


def _tile_gain(g, scale=1.0):
    return jnp.tile(g.astype(F32) * scale, LANES // HEAD_DIM)


def _ab_mixer(h, rope, B, S, g, w_in, w_out, a_q, a_k, b_q, b_k):
    A, Bw = A_HEADS * HEAD_DIM, B_HEADS * HEAD_DIM
    cuts = np.cumsum([A, A, A, Bw, Bw, Bw, IDX_HEADS * HEAD_DIM, HEAD_DIM, IDX_HEADS])
    base, ki0, wi0 = w_in[:, :cuts[6]], w_in[:, cuts[6]:cuts[7]], w_in[:, cuts[7]:cuts[8]]
    w_pad = jnp.concatenate([base, ki0, ki0, wi0, jnp.zeros((D_MODEL, LANES - IDX_HEADS), F32)], axis=1).astype(BF16)
    scale = HEAD_DIM ** -0.5 * LOG2E
    kinds = ["nr"] * 8 + ["p"] * 4 + ["nr"] * 8 + ["p"] * 4 + ["r"] * 3 + ["pa"]
    one = jnp.ones((LANES,), F32)
    gains = jnp.stack([_tile_gain(a_q, scale)] * 4 + [_tile_gain(a_k)] * 4 + [one] * 4
                      + [_tile_gain(b_q, scale)] * 4 + [_tile_gain(b_k)] * 4 + [one] * 8)
    y, aux = _project(h, g, w_pad, gains, rope, kinds)
    kmean = _block_means(y, 4, 4, B, S)
    tq = min(1024, S)
    oa = _flash("moba", y, y, y, 0, 4, 8, 4, B, S, tq=tq, ts=512, tk=tq, extra=kmean)
    bias = _dsa_select(y, aux, 24, 26, B, S)
    ob = _flash("dsa", y, y, y, 12, 16, 20, 4, B, S, tq=tq, ts=512, tk=tq, extra=bias)
    return _out_proj([oa, ob], w_out.astype(BF16), h)


def _c_mixer(h, rope, B, S, g, w_in, w_out, q_norm, k_norm):
    scale = HEAD_DIM ** -0.5 * LOG2E
    kinds = ["nr"] * 16 + ["p"] * 7 + ["pa"]
    one = jnp.ones((LANES,), F32)
    gains = jnp.stack([_tile_gain(q_norm, scale)] * 8 + [_tile_gain(k_norm)] * 8 + [one] * 8)
    y, _ = _project(h, g, w_in.astype(BF16), gains, rope, kinds)
    o = _flash("dilated", y, y, y, 0, 8, 16, 8, B, S, tq=512, tk=512)
    return _out_proj([o], w_out.astype(BF16), h)


def _peer(h, g, w_query, sub_keys, expert_u, expert_v):
    sk = sub_keys.reshape(2 * PEER_HEADS, N_KEYS, N_KEYS).astype(BF16)
    xn, *tables = _peer_route(h, g, w_query.astype(BF16), sk)
    return _peer_experts(xn, expert_u.astype(BF16), expert_v.T.astype(BF16), tables, h)


def kernel(x, p, positions, attn_norm, ffn_norm, ple_norm, ab_w_in, ab_w_out, a_q_norm, a_k_norm, b_q_norm,
           b_k_norm, c_w_in, c_w_out, c_q_norm, c_k_norm, peer_w_query, peer_sub_keys, peer_u, peer_v,
           ple_w_gate, ple_w_up):
    B, S, D = x.shape
    depth = p.shape[0]
    rope = _rope_tables(positions)
    h = x.reshape(B * S, D)
    for i in range(depth):
        j = i // 2
        if i % 2 == 0:
            h = _ab_mixer(h, rope, B, S, attn_norm[i], ab_w_in[j], ab_w_out[j],
                          a_q_norm[j], a_k_norm[j], b_q_norm[j], b_k_norm[j])
        else:
            h = _c_mixer(h, rope, B, S, attn_norm[i], c_w_in[j], c_w_out[j], c_q_norm[j], c_k_norm[j])
        h = _peer(h, ffn_norm[i], peer_w_query[i], peer_sub_keys[i], peer_u[i], peer_v[i])
        h = _ple(h, ple_norm[i], p[i].reshape(B * S, -1), ple_w_gate[i].astype(BF16), ple_w_up[i].astype(BF16))
    return h.reshape(B, S, D)
```

```python
import functools
import math

import numpy as np
import jax
import jax.numpy as jnp
from jax import lax
from jax.experimental import pallas as pl
from jax.experimental.pallas import tpu as pltpu

F32 = jnp.float32
BF16 = jnp.bfloat16

D_MODEL = 1024
HEAD_DIM = 64
ROT_HALF = 8
ROPE_THETA = 500000.0
NORM_EPS = 1e-6
NEG = -1e30
LOG2E = math.log2(math.e)
LANES = 128

A_HEADS = 8
B_HEADS = 8
C_HEADS = 16
IDX_HEADS = 4
MOBA_BLOCK = 256
MOBA_TOPK = 3
DSA_TOPK = 256
DILATED_PAIRS = ((128, 1), (512, 4), (2048, 16))
PEER_HEADS = 8
N_KEYS = 128
PEER_TOPK = 16

V7X_VMEM_BYTES = 64 * 1024 * 1024


def _cparams(sem, vmem_mb):
    assert vmem_mb * 2**20 < V7X_VMEM_BYTES
    return pltpu.CompilerParams(dimension_semantics=sem, vmem_limit_bytes=vmem_mb * 2**20)


def _rms(x, g):
    ms = jnp.mean(x * x, axis=-1, keepdims=True)
    return x * lax.rsqrt(ms + NORM_EPS) * g


def _nt_dot(a, b):
    return lax.dot_general(a, b, (((1,), (1,)), ((), ())), preferred_element_type=F32)


def _split_bf16(x):
    hi = x.astype(BF16)
    lo = (x - hi.astype(F32)).astype(BF16)
    return hi, lo


def _proj_kernel(x_ref, g_ref, w_ref, gains_ref, cos_ref, sa_ref, sb_ref, y_ref, aux_ref, *, kinds, group):
    xn = _rms(x_ref[...], g_ref[...]).astype(BF16)
    row = lax.broadcasted_iota(jnp.int32, (LANES, LANES), 0) // HEAD_DIM
    col = lax.broadcasted_iota(jnp.int32, (LANES, LANES), 1) // HEAD_DIM
    head_avg = jnp.where(row == col, 1.0 / HEAD_DIM, 0.0).astype(BF16)
    n_chunks = len(kinds)
    for g0 in range(0, n_chunks, group):
        g1 = min(g0 + group, n_chunks)
        yg = jnp.dot(xn, w_ref[:, g0 * LANES:g1 * LANES], preferred_element_type=F32)
        for c in range(g0, g1):
            y = yg[:, (c - g0) * LANES:(c - g0 + 1) * LANES]
            kind = kinds[c]
            if "n" in kind:
                hi, lo = _split_bf16(y * y)
                msq = (jnp.dot(hi, head_avg, preferred_element_type=F32)
                       + jnp.dot(lo, head_avg, preferred_element_type=F32))
                y = y * lax.rsqrt(msq + NORM_EPS) * gains_ref[c:c + 1, :]
            if "r" in kind:
                y = (y * cos_ref[...] + pltpu.roll(y, LANES - ROT_HALF, 1) * sa_ref[...]
                     + pltpu.roll(y, ROT_HALF, 1) * sb_ref[...])
            if "a" in kind:
                aux_ref[...] = y
            y_ref[c] = y.astype(BF16)


def _project(h, g, w_bf16, gains, rope, kinds, *, tm=512, group=4):
    T, D = h.shape
    n_chunks = len(kinds)
    assert w_bf16.shape == (D, n_chunks * LANES) and T % tm == 0
    cos, sa, sb = rope
    tok = lambda i: (i, 0)
    fixed = lambda i: (0, 0)
    return pl.pallas_call(
        functools.partial(_proj_kernel, kinds=tuple(kinds), group=group),
        grid=(T // tm,),
        in_specs=[pl.BlockSpec((tm, D), tok), pl.BlockSpec((1, D), fixed),
                  pl.BlockSpec((D, n_chunks * LANES), fixed), pl.BlockSpec((n_chunks, LANES), fixed),
                  pl.BlockSpec((tm, LANES), tok), pl.BlockSpec((tm, LANES), tok), pl.BlockSpec((tm, LANES), tok)],
        out_specs=[pl.BlockSpec((n_chunks, tm, LANES), lambda i: (0, i, 0)), pl.BlockSpec((tm, LANES), tok)],
        out_shape=[jax.ShapeDtypeStruct((n_chunks, T, LANES), BF16), jax.ShapeDtypeStruct((T, LANES), F32)],
        compiler_params=_cparams(("parallel",), 48),
        name="proj",
    )(h, g.reshape(1, D), w_bf16, gains, cos, sa, sb)


def _rope_tables(positions):
    B, S = positions.shape
    inv = ROPE_THETA ** (-jnp.arange(ROT_HALF, dtype=F32) / ROT_HALF)
    ang = positions.astype(F32)[..., None] * inv
    cos, sin = jnp.cos(ang), jnp.sin(ang)
    rest = HEAD_DIM - 2 * ROT_HALF
    one, zero, z8 = jnp.ones((B, S, rest), F32), jnp.zeros((B, S, rest), F32), jnp.zeros((B, S, ROT_HALF), F32)
    tile = lambda parts: jnp.tile(jnp.concatenate(parts, axis=-1), (1, 1, LANES // HEAD_DIM)).reshape(B * S, LANES)
    return tile([cos, cos, one]), tile([-sin, z8, zero]), tile([z8, sin, zero])


def _kmean_kernel(k_ref, o_ref):
    n = pl.program_id(1)

    @pl.when(n == 0)
    def _():
        o_ref[...] = jnp.zeros_like(o_ref)

    o_ref[:, pl.ds(n, 1), :] = jnp.mean(k_ref[...].astype(F32), axis=1, keepdims=True)


def _block_means(y, k_off, pairs, B, S):
    nb = S // MOBA_BLOCK
    assert nb <= LANES and k_off % pairs == 0
    return pl.pallas_call(
        _kmean_kernel,
        grid=(B, nb),
        in_specs=[pl.BlockSpec((pairs, MOBA_BLOCK, LANES), lambda b, n: (k_off // pairs, b * nb + n, 0))],
        out_specs=pl.BlockSpec((pairs, LANES, LANES), lambda b, n: (0, b, 0)),
        out_shape=jax.ShapeDtypeStruct((pairs, B * LANES, LANES), F32),
        compiler_params=_cparams(("parallel", "arbitrary"), 16),
        name="moba_kmean",
    )(y)


MASK_BIG = 32768.0


def _flash_kernel(*refs, mode, pairs, tq, ts, tk, n_steps, back):
    q_ref, k_ref, v_ref, x_ref, o_ref, m_scr, acc_scr, qa_scr = refs[:8]
    s_scr, p_scr, alpha_scr = refs[8:10], refs[10:12], refs[12:14]
    if mode == "dilated":
        b_ref = x_ref
    elif mode == "dsa":
        b_ref, bias_scr = x_ref, refs[14]
    else:
        km_ref, bias_scr, oh_scr = x_ref, refs[14], refs[15]
    i = pl.program_id(1)
    j = pl.program_id(2)
    lane = lax.broadcasted_iota(jnp.int32, (1, LANES), 1)
    halves = (lane < HEAD_DIM, lane >= HEAD_DIM)
    if mode == "dilated":
        kt = i * (tq // tk) - back + j
        active = kt >= 0
    else:
        kt = j
        active = j * tk < (i + 1) * tq
    tpos = i * tq + lax.broadcasted_iota(jnp.int32, (tq, 1), 0)

    n_heads = 2 * pairs
    n_sub = tq // ts
    n_units = n_sub * n_heads

    @pl.when(j == 0)
    def _():
        m_scr[...] = jnp.full_like(m_scr, NEG)
        acc_scr[...] = jnp.zeros_like(acc_scr)
        blk = lax.broadcasted_iota(jnp.int32, (ts, LANES), 1)
        for sub in range(n_sub):
            rows = slice(sub * ts, (sub + 1) * ts)
            qblk = tpos[rows] // MOBA_BLOCK
            for c in range(pairs):
                if mode == "moba":
                    km_hi, km_lo = _split_bf16(km_ref[c])
                for half in range(2):
                    qm = jnp.where(halves[half], q_ref[c, rows, :], jnp.zeros((), BF16))
                    if mode == "moba":
                        gate = jnp.where(blk < qblk, _nt_dot(qm, km_hi) + _nt_dot(qm, km_lo), NEG)
                        rest, thr = gate, None
                        for _ in range(MOBA_TOPK):
                            thr = jnp.max(rest, axis=1, keepdims=True)
                            rest = jnp.where(rest >= thr, NEG, rest)
                        attended = ((gate >= thr) & (blk < qblk)) | (blk == qblk)
                        feat = jnp.where(attended, 0.0, -MASK_BIG)
                        if half == 0:
                            feat = pltpu.roll(feat, HEAD_DIM, 1)
                        qm = jnp.where(halves[half], qm, feat.astype(BF16))
                    qa_scr[sub * n_heads + 2 * c + half] = qm

    def attend(bias):
        if mode == "moba":
            kblk = (kt * tk + lax.broadcasted_iota(jnp.int32, (tk, 1), 0)) // MOBA_BLOCK
            oh_scr[0] = jnp.where(lane == kblk + HEAD_DIM, 1.0, 0.0).astype(BF16)
            oh_scr[1] = jnp.where(lane == kblk, 1.0, 0.0).astype(BF16)

        def scores(u, par):
            kp = k_ref[(u % n_heads) // 2]
            if mode == "moba":
                kp = jnp.where(halves[par], kp, oh_scr[par])
            s_scr[par][...] = _nt_dot(qa_scr[u], kp)

        def bias_rows(u):
            start = u // n_heads * ts
            if isinstance(start, int):
                return bias[start:start + ts, :]
            return bias[pl.ds(pl.multiple_of(start, ts), ts), :]

        def softmax(u, par):
            s = s_scr[par][...]
            if bias is not None:
                s = s + (bias[...] if n_sub == 1 else bias_rows(u))
            m_old = m_scr[u]
            m_new = jnp.maximum(m_old, jnp.max(s, axis=1, keepdims=True))
            p_scr[par][...] = jnp.exp2(s - m_new).astype(BF16)
            alpha_scr[par][...] = jnp.exp2(m_old - m_new)
            m_scr[u] = m_new

        def values(u, par):
            va = jnp.where(halves[par], v_ref[(u % n_heads) // 2], jnp.ones((), BF16))
            acc_scr[u] = alpha_scr[par][...] * acc_scr[u] + jnp.dot(p_scr[par][...], va, preferred_element_type=F32)

        scores(0, 0)
        scores(1, 1)
        softmax(0, 0)

        def body(c, carry):
            scores(2 * c + 2, 0)
            softmax(2 * c + 1, 1)
            values(2 * c, 0)
            scores(2 * c + 3, 1)
            softmax(2 * c + 2, 0)
            values(2 * c + 1, 1)
            return carry

        lax.fori_loop(0, n_units // 2 - 1, body, 0)
        softmax(n_units - 1, 1)
        values(n_units - 2, 0)
        values(n_units - 1, 1)

    if mode == "moba":
        on_diagonal = (kt + 1) * tk > i * tq

        @pl.when(active & on_diagonal)
        def _():
            kpos = kt * tk + lax.broadcasted_iota(jnp.int32, (tq, tk), 1)
            bias_scr[...] = jnp.where(kpos <= tpos, 0.0, -jnp.inf)
            attend(bias_scr)

        @pl.when(active & jnp.logical_not(on_diagonal))
        def _():
            attend(None)
    elif mode == "dsa":
        @pl.when(active)
        def _():
            for part in range(b_ref.shape[1]):
                width = b_ref.shape[3]
                bias_scr[:, part * width:(part + 1) * width] = b_ref[0, part].astype(F32)
            attend(bias_scr)
    else:
        @pl.when(active)
        def _():
            attend(b_ref.at[0])

    @pl.when(j == n_steps - 1)
    def _():
        for sub in range(n_sub):
            for c in range(pairs):
                a0, a1 = acc_scr[sub * n_heads + 2 * c], acc_scr[sub * n_heads + 2 * c + 1]
                num = jnp.where(halves[0], a0, a1)
                den = pltpu.roll(jnp.where(halves[0], a1, a0), HEAD_DIM, 1)
                o_ref[c, sub * ts:(sub + 1) * ts, :] = (num / den).astype(BF16)


def _dilated_bias_table(tq, tk, back, n_steps):
    r = np.arange(tq)[:, None]
    c = np.arange(tk)[None, :]
    tabs = []
    for j in range(n_steps):
        d = (back - j) * tk + r - c
        mult = sum(((d >= 0) & (d <= w) & (d % dil == 0)).astype(np.float64) for w, dil in DILATED_PAIRS)
        tabs.append(np.where(mult > 0, np.log2(np.maximum(mult, 1.0)), -np.inf))
    return jnp.asarray(np.stack(tabs), F32)


def _flash(mode, q_arr, k_arr, v_arr, q_off, k_off, v_off, pairs, B, S, *, tq, tk, ts=None, extra=None):
    ts = ts or tq
    nq, nk = S // tq, S // tk
    n_units = tq // ts * 2 * pairs
    assert S % tq == 0 and S % tk == 0 and tq % tk == 0 and tq % ts == 0
    assert q_off % pairs == 0 and k_off % pairs == 0 and v_off % pairs == 0
    if mode == "dilated":
        max_back = max(w for w, _ in DILATED_PAIRS)
        back = -(-max_back // tk)
        n_steps = back + tq // tk
        kv_tile = lambda i, j: jnp.maximum(i * (tq // tk) - back + j, 0)
    else:
        back = 0
        n_steps = nk
        kv_tile = lambda i, j: jnp.minimum(j, ((i + 1) * tq - 1) // tk)
    in_specs = [pl.BlockSpec((pairs, tq, LANES), lambda b, i, j: (q_off // pairs, b * nq + i, 0)),
                pl.BlockSpec((pairs, tk, LANES), lambda b, i, j: (k_off // pairs, b * nk + kv_tile(i, j), 0)),
                pl.BlockSpec((pairs, tk, LANES), lambda b, i, j: (v_off // pairs, b * nk + kv_tile(i, j), 0))]
    args = [q_arr, k_arr, v_arr]
    scratch = [pltpu.VMEM((n_units, ts, 1), F32), pltpu.VMEM((n_units, ts, LANES), F32),
               pltpu.VMEM((n_units, ts, LANES), BF16), pltpu.VMEM((ts, tk), F32), pltpu.VMEM((ts, tk), F32),
               pltpu.VMEM((ts, tk), BF16), pltpu.VMEM((ts, tk), BF16),
               pltpu.VMEM((ts, 1), F32), pltpu.VMEM((ts, 1), F32)]
    if mode == "moba":
        assert tk % MOBA_BLOCK == 0 and S // MOBA_BLOCK <= HEAD_DIM
        in_specs.append(pl.BlockSpec((pairs, LANES, LANES), lambda b, i, j: (0, b, 0)))
        args.append(extra)
        scratch += [pltpu.VMEM((tq, tk), F32), pltpu.VMEM((2, tk, LANES), BF16)]
    elif mode == "dsa":
        ck = extra.shape[3]
        assert tk % ck == 0
        in_specs.append(pl.BlockSpec((1, tk // ck, tq, ck), lambda b, i, j: (b, kv_tile(i, j), i, 0)))
        args.append(extra)
        scratch.append(pltpu.VMEM((tq, tk), F32))
    else:
        in_specs.append(pl.BlockSpec((1, tq, tk), lambda b, i, j: (j, 0, 0)))
        args.append(_dilated_bias_table(tq, tk, back, n_steps))
    return pl.pallas_call(
        functools.partial(_flash_kernel, mode=mode, pairs=pairs, tq=tq, ts=ts, tk=tk, n_steps=n_steps, back=back),
        grid=(B, nq, n_steps),
        in_specs=in_specs,
        out_specs=pl.BlockSpec((pairs, tq, LANES), lambda b, i, j: (0, b * nq + i, 0)),
        out_shape=jax.ShapeDtypeStruct((pairs, B * S, LANES), BF16),
        scratch_shapes=scratch,
        compiler_params=_cparams(("parallel", "parallel", "arbitrary"), 48),
        name="flash_" + mode,
    )(*args)


F32_SUBLANES = 8
BF16_SUBLANES = 16


def _dsa_select_kernel(qi_ref, ki_ref, wt_ref, o_ref, key_scr, top_scr, tri_scr, *, tq, ck, n_chunks, topk):
    i = pl.program_id(1)
    tri_scr[...] = jnp.where(lax.broadcasted_iota(jnp.int32, (ck, ck), 0) >= lax.broadcasted_iota(jnp.int32, (ck, ck), 1),
                             1.0, 0.0).astype(BF16)
    lane = lax.broadcasted_iota(jnp.int32, (1, LANES), 1)
    halves = (lane < HEAD_DIM, lane >= HEAD_DIM)
    tpos = i * tq + lax.broadcasted_iota(jnp.int32, (1, tq), 1)
    n_live = ((i + 1) * tq + ck - 1) // ck
    wt = wt_ref[...]

    def score_chunk(c, carry):
        kc = ki_ref[0, pl.ds(pl.multiple_of(c * ck, ck), ck), :]
        score = jnp.zeros((ck, tq), F32)
        for h in range(IDX_HEADS):
            qm = jnp.where(halves[h % 2], qi_ref[h // 2], jnp.zeros((), BF16))
            score = score + wt[h:h + 1, :] * jnp.maximum(_nt_dot(kc, qm), 0.0)
        kpos = c * ck + lax.broadcasted_iota(jnp.int32, (ck, tq), 0)
        score = jnp.where(jnp.abs(score) >= jnp.finfo(F32).tiny, score, 0.0)
        score = jnp.where(kpos <= tpos, score, NEG)
        bits = pltpu.bitcast(score, jnp.int32)
        key_scr[c] = jnp.where(bits < 0, bits ^ jnp.int32(0x7FFFFFFF), bits)
        top_scr[c] = pltpu.bitcast(bits & jnp.int32(-65536), F32).astype(BF16)
        return carry

    lax.fori_loop(0, n_live, score_chunk, 0)

    def count(src, pred, dtype, group):
        assert ck // group <= 256

        def body(c, acc):
            hit = jnp.where(pred(src[c]), jnp.ones((), dtype), jnp.zeros((), dtype))
            parts = [hit[s * group:(s + 1) * group, :] for s in range(ck // group)]
            while len(parts) > 1:
                parts = [a + b for a, b in zip(parts[0::2], parts[1::2])]
            return acc + parts[0].astype(F32)

        acc = lax.fori_loop(0, n_live, body, jnp.zeros((group, tq), F32))
        return jnp.sum(acc, axis=0, keepdims=True)

    def upper_bit(it, lo):
        cand = lo + jnp.left_shift(jnp.int32(1), 15 - it)
        pattern = jnp.where(cand < 0, cand ^ jnp.int32(0x7FFF), cand)
        pattern = jnp.where((pattern > 0) & (pattern < 0x80), jnp.int32(0x80), pattern)
        value = pltpu.bitcast(jnp.left_shift(pattern, 16), F32).astype(BF16)
        return jnp.where(count(top_scr, lambda sc: sc >= value, BF16, BF16_SUBLANES) >= topk, cand, lo)

    upper = lax.fori_loop(0, 16, upper_bit, jnp.full((1, tq), -2**15, jnp.int32))

    def lower_bit(it, lo):
        cand = lo + jnp.left_shift(jnp.int32(1), 15 - it)
        return jnp.where(count(key_scr, lambda kc: kc >= cand, F32, F32_SUBLANES) >= topk, cand, lo)

    thr = lax.fori_loop(0, 16, lower_bit, jnp.left_shift(upper, 16))
    need = topk - count(key_scr, lambda kc: kc > thr, F32, F32_SUBLANES)

    def emit(c, before):
        kc = key_scr[c]
        kpos = c * ck + lax.broadcasted_iota(jnp.int32, (ck, tq), 0)
        equal = kc == thr
        rank = before + jnp.dot(tri_scr[...], jnp.where(equal, 1.0, 0.0).astype(BF16), preferred_element_type=F32)
        take = ((kc > thr) | (equal & (rank <= need))) & (kpos <= tpos)
        o_ref[0, c] = jnp.where(take, 0.0, -jnp.inf).T.astype(BF16)
        return rank[ck - 1:ck, :]

    lax.fori_loop(0, n_live, emit, jnp.zeros((1, tq), F32))

    def blank(c, carry):
        o_ref[0, c] = jnp.full((tq, ck), -jnp.inf, BF16)
        return carry

    lax.fori_loop(n_live, n_chunks, blank, 0)


def _dsa_select(y, aux, qi_off, ki_off, B, S, *, tq=256, ck=512):
    n_chunks = S // ck
    assert S % ck == 0 and S % tq == 0 and qi_off % 2 == 0
    nq = S // tq
    topk = min(DSA_TOPK, S // 4)
    wt = aux[:, :F32_SUBLANES].T
    return pl.pallas_call(
        functools.partial(_dsa_select_kernel, tq=tq, ck=ck, n_chunks=n_chunks, topk=topk),
        grid=(B, nq),
        in_specs=[pl.BlockSpec((2, tq, LANES), lambda b, i: (qi_off // 2, b * nq + i, 0)),
                  pl.BlockSpec((1, S, LANES), lambda b, i: (ki_off, b, 0)),
                  pl.BlockSpec((F32_SUBLANES, tq), lambda b, i: (0, b * nq + i))],
        out_specs=pl.BlockSpec((1, n_chunks, tq, ck), lambda b, i: (b, 0, i, 0)),
        out_shape=jax.ShapeDtypeStruct((B, n_chunks, S, ck), BF16),
        scratch_shapes=[pltpu.VMEM((n_chunks, ck, tq), jnp.int32), pltpu.VMEM((n_chunks, ck, tq), BF16),
                        pltpu.VMEM((ck, ck), BF16)],
        compiler_params=_cparams(("parallel", "parallel"), 48),
        name="dsa_select",
    )(y, y, wt)


def _out_proj_kernel(*refs, n_in):
    o_refs, w_ref, h_ref, out_ref = refs[:n_in], refs[n_in], refs[n_in + 1], refs[n_in + 2]
    slabs = [o[c] for o in o_refs for c in range(o.shape[0])]
    o = jnp.concatenate(slabs, axis=-1)
    out_ref[...] = h_ref[...] + jnp.dot(o, w_ref[...], preferred_element_type=F32)


def _out_proj(os_, w_bf16, h, *, tm=512):
    T, D = h.shape
    width = sum(o.shape[0] for o in os_) * LANES
    assert w_bf16.shape == (width, D)
    in_specs = [pl.BlockSpec((o.shape[0], tm, LANES), lambda i: (0, i, 0)) for o in os_]
    in_specs += [pl.BlockSpec((width, D), lambda i: (0, 0)), pl.BlockSpec((tm, D), lambda i: (i, 0))]
    return pl.pallas_call(
        functools.partial(_out_proj_kernel, n_in=len(os_)),
        grid=(T // tm,),
        in_specs=in_specs,
        out_specs=pl.BlockSpec((tm, D), lambda i: (i, 0)),
        out_shape=jax.ShapeDtypeStruct((T, D), F32),
        compiler_params=_cparams(("parallel",), 32),
        name="out_proj",
    )(*os_, w_bf16, h)


def _top_values(s, k):
    out = []
    for _ in range(k + 1):
        cur = jnp.max(s, axis=0, keepdims=True)
        out.append(cur)
        s = jnp.where(s >= cur, -jnp.inf, s)
    return jnp.concatenate(out[:k], axis=0), out[k]


def _pair_candidates(t1, t2):
    k = PEER_TOPK
    parts = [t1[0:1] + t2]
    parts += [t1[a:a + 1] + t2[0:k // 2] for a in range(1, k // 2)]
    parts.append(t1[k // 2:k] + t2[0:1])
    return jnp.concatenate(parts, axis=0)


def _kth_largest(c, k):
    left = jnp.full((1, c.shape[1]), float(k), F32)
    tau = jnp.zeros((1, c.shape[1]), F32)
    for _ in range(k):
        cur = jnp.max(c, axis=0, keepdims=True)
        hit = c >= cur
        tau = jnp.where(left > 0.0, cur, tau)
        left = left - jnp.sum(jnp.where(hit, 1.0, 0.0), axis=0, keepdims=True)
        c = jnp.where(hit, -jnp.inf, c)
    return tau


def _peer_route_kernel(h_ref, g_ref, wq_ref, sk_ref, xn_ref, th1_ref, e1_ref, e2_ref):
    xn = _rms(h_ref[...], g_ref[...]).astype(BF16)
    xn_ref[...] = xn
    for hd in range(PEER_HEADS):
        q = jnp.dot(xn, wq_ref[:, hd * 2 * N_KEYS:(hd + 1) * 2 * N_KEYS], preferred_element_type=F32).astype(BF16)
        s1 = _nt_dot(sk_ref[2 * hd], q[:, :N_KEYS])
        s2 = _nt_dot(sk_ref[2 * hd + 1], q[:, N_KEYS:])
        s1 = s1 - jnp.max(s1, axis=0, keepdims=True)
        s2 = s2 - jnp.max(s2, axis=0, keepdims=True)
        (t1, next1), (t2, next2) = _top_values(s1, PEER_TOPK), _top_values(s2, PEER_TOPK)
        cand = _pair_candidates(t1, t2)
        chosen = cand >= _kth_largest(cand, PEER_TOPK)
        log_z = jnp.log(jnp.sum(jnp.where(chosen, jnp.exp(cand), 0.0), axis=0, keepdims=True))
        runner_up = jnp.maximum(jnp.max(jnp.where(chosen, -jnp.inf, cand), axis=0, keepdims=True),
                                jnp.maximum(next1 + t2[0:1], t1[0:1] + next2))
        smallest = jnp.min(jnp.where(chosen, cand, jnp.inf), axis=0, keepdims=True)
        tau = 0.5 * (smallest + runner_up)
        s1 = s1 - log_z
        th1_ref[hd] = jnp.exp((tau - log_z) - s1)
        e1_ref[hd] = jnp.exp(s1)
        e2_ref[hd] = jnp.exp(s2)


def _peer_route(h, g, wq_bf16, sk_bf16, *, tm=256):
    T, D = h.shape
    nq = PEER_HEADS * 2 * N_KEYS
    assert wq_bf16.shape == (D, nq) and sk_bf16.shape == (2 * PEER_HEADS, N_KEYS, N_KEYS)
    return pl.pallas_call(
        _peer_route_kernel,
        grid=(T // tm,),
        in_specs=[pl.BlockSpec((tm, D), lambda i: (i, 0)), pl.BlockSpec((1, D), lambda i: (0, 0)),
                  pl.BlockSpec((D, nq), lambda i: (0, 0)),
                  pl.BlockSpec((2 * PEER_HEADS, N_KEYS, N_KEYS), lambda i: (0, 0, 0))],
        out_specs=[pl.BlockSpec((tm, D), lambda i: (i, 0))]
        + [pl.BlockSpec((PEER_HEADS, N_KEYS, tm), lambda i: (0, 0, i))] * 3,
        out_shape=[jax.ShapeDtypeStruct((T, D), BF16)]
        + [jax.ShapeDtypeStruct((PEER_HEADS, N_KEYS, T), F32)] * 3,
        compiler_params=_cparams(("parallel",), 40),
        name="peer_route",
    )(h, g.reshape(1, D), wq_bf16, sk_bf16)


def _gelu_tanh(x):
    return 0.5 * x * (1.0 + jnp.tanh(math.sqrt(2.0 / math.pi) * (x + 0.044715 * (x * x * x))))


def _peer_expert_kernel(xn_ref, u_ref, vt_ref, th1_ref, e1_ref, e2_ref, h_ref, out_ref,
                        acc_scr, rows_scr, act0_scr, act1_scr, a0_scr, a1_scr, *, eb, sub, tq):
    e = pl.program_id(1)
    rows, sub_rows, n_sub = eb // N_KEYS, sub // N_KEYS, eb // sub
    sublanes = rows_scr.shape[2]
    act_scr, a_scr = (act0_scr, act1_scr), (a0_scr, a1_scr)

    @pl.when(e == 0)
    def _():
        acc_scr[...] = jnp.zeros_like(acc_scr)

    row0 = pl.multiple_of(e * rows, rows)
    for k, tab in enumerate((th1_ref, e1_ref)):
        for hd in range(PEER_HEADS):
            block = tab[hd, pl.ds(row0, rows), :]
            for r in range(rows):
                rows_scr[k, hd * rows + r] = jnp.broadcast_to(block[r:r + 1, :], (sublanes, tq))

    def activations(sb):
        act_scr[sb % 2][...] = _gelu_tanh(_nt_dot(u_ref[sb * sub:(sb + 1) * sub, :], xn_ref[...]).astype(BF16))

    def weigh(sb):
        span = 64
        reps = (span // sublanes, 1)
        for r in range(sub_rows):
            for t0 in range(0, tq, LANES):
                for i2 in range(0, N_KEYS, span):
                    w = jnp.zeros((span, LANES), BF16)
                    for hd in range(PEER_HEADS):
                        k = hd * rows + sb * sub_rows + r
                        th1 = jnp.tile(rows_scr[0, k, :, t0:t0 + LANES], reps)
                        e1 = jnp.tile(rows_scr[1, k, :, t0:t0 + LANES], reps).astype(BF16)
                        e2 = e2_ref[hd, i2:i2 + span, t0:t0 + LANES]
                        w = w + e1 * jnp.where(e2 >= th1, e2, 0.0).astype(BF16)
                    lo = r * N_KEYS + i2
                    a_scr[sb % 2][lo:lo + span, t0:t0 + LANES] = act_scr[sb % 2][lo:lo + span, t0:t0 + LANES] * w

    def project(sb):
        acc_scr[...] += jnp.dot(vt_ref[:, sb * sub:(sb + 1) * sub], a_scr[sb % 2][...], preferred_element_type=F32)

    activations(0)
    for sb in range(n_sub):
        if sb + 1 < n_sub:
            activations(sb + 1)
        if sb >= 1:
            project(sb - 1)
        weigh(sb)
    project(n_sub - 1)

    @pl.when(e == pl.num_programs(1) - 1)
    def _():
        out_ref[...] = h_ref[...] + acc_scr[...].T


def _peer_experts(xn, u_bf16, vt_bf16, tables, h, *, tq=512, eb=2048, sub=1024):
    T, D = h.shape
    E = u_bf16.shape[0]
    assert E == N_KEYS * N_KEYS and E % eb == 0 and eb % (F32_SUBLANES * N_KEYS) == 0 and T % tq == 0
    assert eb % sub == 0 and sub % N_KEYS == 0
    table_spec = pl.BlockSpec((PEER_HEADS, N_KEYS, tq), lambda i, e: (0, 0, i))
    return pl.pallas_call(
        functools.partial(_peer_expert_kernel, eb=eb, sub=sub, tq=tq),
        grid=(T // tq, E // eb),
        in_specs=[pl.BlockSpec((tq, D), lambda i, e: (i, 0)),
                  pl.BlockSpec((eb, D), lambda i, e: (e, 0)),
                  pl.BlockSpec((D, eb), lambda i, e: (0, e)),
                  table_spec, table_spec, table_spec,
                  pl.BlockSpec((tq, D), lambda i, e: (i, 0))],
        out_specs=pl.BlockSpec((tq, D), lambda i, e: (i, 0)),
        out_shape=jax.ShapeDtypeStruct((T, D), F32),
        scratch_shapes=[pltpu.VMEM((D, tq), F32),
                        pltpu.VMEM((2, PEER_HEADS * (eb // N_KEYS), F32_SUBLANES, tq), F32),
                        pltpu.VMEM((sub, tq), BF16), pltpu.VMEM((sub, tq), BF16),
                        pltpu.VMEM((sub, tq), BF16), pltpu.VMEM((sub, tq), BF16)],
        compiler_params=_cparams(("parallel", "arbitrary"), 56),
        name="peer_experts",
    )(xn, u_bf16, vt_bf16, *tables, h)


def _ple_kernel(h_ref, g_ref, p_ref, wg_ref, wu_ref, out_ref):
    h = h_ref[...]
    gate = jax.nn.sigmoid(jnp.dot(_rms(h, g_ref[...]).astype(BF16), wg_ref[...], preferred_element_type=F32))
    up = jnp.dot(p_ref[...].astype(BF16), wu_ref[...], preferred_element_type=F32)
    out_ref[...] = h + up * gate


def _ple(h, g, p, wg_bf16, wu_bf16, *, tm=512):
    T, D = h.shape
    P = p.shape[1]
    return pl.pallas_call(
        _ple_kernel,
        grid=(T // tm,),
        in_specs=[pl.BlockSpec((tm, D), lambda i: (i, 0)), pl.BlockSpec((1, D), lambda i: (0, 0)),
                  pl.BlockSpec((tm, P), lambda i: (i, 0)), pl.BlockSpec((D, D), lambda i: (0, 0)),
                  pl.BlockSpec((P, D), lambda i: (0, 0))],
        out_specs=pl.BlockSpec((tm, D), lambda i: (i, 0)),
        out_shape=jax.ShapeDtypeStruct((T, D), F32),
        compiler_params=_cparams(("parallel",), 32),
        name="ple",
    )(h, g.reshape(1, D), p, wg_bf16, wu_bf16)


def _tile_gain(g, scale=1.0):
    return jnp.tile(g.astype(F32) * scale, LANES // HEAD_DIM)


def _ab_mixer(h, rope, B, S, g, w_in, w_out, a_q, a_k, b_q, b_k):
    A, Bw = A_HEADS * HEAD_DIM, B_HEADS * HEAD_DIM
    cuts = np.cumsum([A, A, A, Bw, Bw, Bw, IDX_HEADS * HEAD_DIM, HEAD_DIM, IDX_HEADS])
    base, ki0, wi0 = w_in[:, :cuts[6]], w_in[:, cuts[6]:cuts[7]], w_in[:, cuts[7]:cuts[8]]
    w_pad = jnp.concatenate([base, ki0, ki0, wi0, jnp.zeros((D_MODEL, LANES - IDX_HEADS), F32)], axis=1).astype(BF16)
    scale = HEAD_DIM ** -0.5 * LOG2E
    kinds = ["nr"] * 8 + ["p"] * 4 + ["nr"] * 8 + ["p"] * 4 + ["r"] * 3 + ["pa"]
    one = jnp.ones((LANES,), F32)
    gains = jnp.stack([_tile_gain(a_q, scale)] * 4 + [_tile_gain(a_k)] * 4 + [one] * 4
                      + [_tile_gain(b_q, scale)] * 4 + [_tile_gain(b_k)] * 4 + [one] * 8)
    y, aux = _project(h, g, w_pad, gains, rope, kinds)
    kmean = _block_means(y, 4, 4, B, S)
    tq = min(1024, S)
    oa = _flash("moba", y, y, y, 0, 4, 8, 4, B, S, tq=tq, ts=512, tk=tq, extra=kmean)
    bias = _dsa_select(y, aux, 24, 26, B, S)
    ob = _flash("dsa", y, y, y, 12, 16, 20, 4, B, S, tq=tq, ts=512, tk=tq, extra=bias)
    return _out_proj([oa, ob], w_out.astype(BF16), h)


def _c_mixer(h, rope, B, S, g, w_in, w_out, q_norm, k_norm):
    scale = HEAD_DIM ** -0.5 * LOG2E
    kinds = ["nr"] * 16 + ["p"] * 7 + ["pa"]
    one = jnp.ones((LANES,), F32)
    gains = jnp.stack([_tile_gain(q_norm, scale)] * 8 + [_tile_gain(k_norm)] * 8 + [one] * 8)
    y, _ = _project(h, g, w_in.astype(BF16), gains, rope, kinds)
    o = _flash("dilated", y, y, y, 0, 8, 16, 8, B, S, tq=512, tk=512)
    return _out_proj([o], w_out.astype(BF16), h)


def _peer(h, g, w_query, sub_keys, expert_u, expert_v):
    sk = sub_keys.reshape(2 * PEER_HEADS, N_KEYS, N_KEYS).astype(BF16)
    xn, *tables = _peer_route(h, g, w_query.astype(BF16), sk)
    return _peer_experts(xn, expert_u.astype(BF16), expert_v.T.astype(BF16), tables, h)


def kernel(x, p, positions, attn_norm, ffn_norm, ple_norm, ab_w_in, ab_w_out, a_q_norm, a_k_norm, b_q_norm,
           b_k_norm, c_w_in, c_w_out, c_q_norm, c_k_norm, peer_w_query, peer_sub_keys, peer_u, peer_v,
           ple_w_gate, ple_w_up):
    B, S, D = x.shape
    depth = p.shape[0]
    rope = _rope_tables(positions)
    h = x.reshape(B * S, D)
    for i in range(depth):
        j = i // 2
        if i % 2 == 0:
            h = _ab_mixer(h, rope, B, S, attn_norm[i], ab_w_in[j], ab_w_out[j],
                          a_q_norm[j], a_k_norm[j], b_q_norm[j], b_k_norm[j])
        else:
            h = _c_mixer(h, rope, B, S, attn_norm[i], c_w_in[j], c_w_out[j], c_q_norm[j], c_k_norm[j])
        h = _peer(h, ffn_norm[i], peer_w_query[i], peer_sub_keys[i], peer_u[i], peer_v[i])
        h = _ple(h, ple_norm[i], p[i].reshape(B * S, -1), ple_w_gate[i].astype(BF16), ple_w_up[i].astype(BF16))
    return h.reshape(B, S, D)
```

```python
import functools
import math

import numpy as np
import jax
import jax.numpy as jnp
from jax import lax
from jax.experimental import pallas as pl
from jax.experimental.pallas import tpu as pltpu

F32 = jnp.float32
BF16 = jnp.bfloat16

D_MODEL = 1024
HEAD_DIM = 64
ROT_HALF = 8
ROPE_THETA = 500000.0
NORM_EPS = 1e-6
NEG = -1e30
LOG2E = math.log2(math.e)
LANES = 128

A_HEADS = 8
B_HEADS = 8
C_HEADS = 16
IDX_HEADS = 4
MOBA_BLOCK = 256
MOBA_TOPK = 3
DSA_TOPK = 256
DILATED_PAIRS = ((128, 1), (512, 4), (2048, 16))
PEER_HEADS = 8
N_KEYS = 128
PEER_TOPK = 16

V7X_VMEM_BYTES = 64 * 1024 * 1024


def _cparams(sem, vmem_mb):
    assert vmem_mb * 2**20 < V7X_VMEM_BYTES
    return pltpu.CompilerParams(dimension_semantics=sem, vmem_limit_bytes=vmem_mb * 2**20)


def _rms(x, g):
    ms = jnp.mean(x * x, axis=-1, keepdims=True)
    return x * lax.rsqrt(ms + NORM_EPS) * g


def _nt_dot(a, b):
    return lax.dot_general(a, b, (((1,), (1,)), ((), ())), preferred_element_type=F32)


def _split_bf16(x):
    hi = x.astype(BF16)
    lo = (x - hi.astype(F32)).astype(BF16)
    return hi, lo


def _proj_kernel(x_ref, g_ref, w_ref, gains_ref, cos_ref, sa_ref, sb_ref, y_ref, aux_ref, *, kinds, group):
    xn = _rms(x_ref[...], g_ref[...]).astype(BF16)
    row = lax.broadcasted_iota(jnp.int32, (LANES, LANES), 0) // HEAD_DIM
    col = lax.broadcasted_iota(jnp.int32, (LANES, LANES), 1) // HEAD_DIM
    head_avg = jnp.where(row == col, 1.0 / HEAD_DIM, 0.0).astype(BF16)
    n_chunks = len(kinds)
    for g0 in range(0, n_chunks, group):
        g1 = min(g0 + group, n_chunks)
        yg = jnp.dot(xn, w_ref[:, g0 * LANES:g1 * LANES], preferred_element_type=F32)
        for c in range(g0, g1):
            y = yg[:, (c - g0) * LANES:(c - g0 + 1) * LANES]
            kind = kinds[c]
            if "n" in kind:
                hi, lo = _split_bf16(y * y)
                msq = (jnp.dot(hi, head_avg, preferred_element_type=F32)
                       + jnp.dot(lo, head_avg, preferred_element_type=F32))
                y = y * lax.rsqrt(msq + NORM_EPS) * gains_ref[c:c + 1, :]
            if "r" in kind:
                y = (y * cos_ref[...] + pltpu.roll(y, LANES - ROT_HALF, 1) * sa_ref[...]
                     + pltpu.roll(y, ROT_HALF, 1) * sb_ref[...])
            if "a" in kind:
                aux_ref[...] = y
            y_ref[c] = y.astype(BF16)


def _project(h, g, w_bf16, gains, rope, kinds, *, tm=512, group=4):
    T, D = h.shape
    n_chunks = len(kinds)
    assert w_bf16.shape == (D, n_chunks * LANES) and T % tm == 0
    cos, sa, sb = rope
    tok = lambda i: (i, 0)
    fixed = lambda i: (0, 0)
    return pl.pallas_call(
        functools.partial(_proj_kernel, kinds=tuple(kinds), group=group),
        grid=(T // tm,),
        in_specs=[pl.BlockSpec((tm, D), tok), pl.BlockSpec((1, D), fixed),
                  pl.BlockSpec((D, n_chunks * LANES), fixed), pl.BlockSpec((n_chunks, LANES), fixed),
                  pl.BlockSpec((tm, LANES), tok), pl.BlockSpec((tm, LANES), tok), pl.BlockSpec((tm, LANES), tok)],
        out_specs=[pl.BlockSpec((n_chunks, tm, LANES), lambda i: (0, i, 0)), pl.BlockSpec((tm, LANES), tok)],
        out_shape=[jax.ShapeDtypeStruct((n_chunks, T, LANES), BF16), jax.ShapeDtypeStruct((T, LANES), F32)],
        compiler_params=_cparams(("parallel",), 48),
        name="proj",
    )(h, g.reshape(1, D), w_bf16, gains, cos, sa, sb)


def _rope_tables(positions):
    B, S = positions.shape
    inv = ROPE_THETA ** (-jnp.arange(ROT_HALF, dtype=F32) / ROT_HALF)
    ang = positions.astype(F32)[..., None] * inv
    cos, sin = jnp.cos(ang), jnp.sin(ang)
    rest = HEAD_DIM - 2 * ROT_HALF
    one, zero, z8 = jnp.ones((B, S, rest), F32), jnp.zeros((B, S, rest), F32), jnp.zeros((B, S, ROT_HALF), F32)
    tile = lambda parts: jnp.tile(jnp.concatenate(parts, axis=-1), (1, 1, LANES // HEAD_DIM)).reshape(B * S, LANES)
    return tile([cos, cos, one]), tile([-sin, z8, zero]), tile([z8, sin, zero])


def _kmean_kernel(k_ref, o_ref):
    n = pl.program_id(1)

    @pl.when(n == 0)
    def _():
        o_ref[...] = jnp.zeros_like(o_ref)

    o_ref[:, pl.ds(n, 1), :] = jnp.mean(k_ref[...].astype(F32), axis=1, keepdims=True)


def _block_means(y, k_off, pairs, B, S):
    nb = S // MOBA_BLOCK
    assert nb <= LANES and k_off % pairs == 0
    return pl.pallas_call(
        _kmean_kernel,
        grid=(B, nb),
        in_specs=[pl.BlockSpec((pairs, MOBA_BLOCK, LANES), lambda b, n: (k_off // pairs, b * nb + n, 0))],
        out_specs=pl.BlockSpec((pairs, LANES, LANES), lambda b, n: (0, b, 0)),
        out_shape=jax.ShapeDtypeStruct((pairs, B * LANES, LANES), F32),
        compiler_params=_cparams(("parallel", "arbitrary"), 16),
        name="moba_kmean",
    )(y)


MASK_BIG = 32768.0


def _flash_kernel(*refs, mode, pairs, tq, ts, tk, n_steps, back):
    q_ref, k_ref, v_ref, x_ref, o_ref, m_scr, acc_scr, qa_scr = refs[:8]
    s_scr, p_scr, alpha_scr = refs[8:10], refs[10:12], refs[12:14]
    if mode == "dilated":
        b_ref = x_ref
    elif mode == "dsa":
        b_ref, bias_scr = x_ref, refs[14]
    else:
        km_ref, bias_scr, oh_scr = x_ref, refs[14], refs[15]
    i = pl.program_id(1)
    j = pl.program_id(2)
    lane = lax.broadcasted_iota(jnp.int32, (1, LANES), 1)
    halves = (lane < HEAD_DIM, lane >= HEAD_DIM)
    if mode == "dilated":
        kt = i * (tq // tk) - back + j
        active = kt >= 0
    else:
        kt = j
        active = j * tk < (i + 1) * tq
    tpos = i * tq + lax.broadcasted_iota(jnp.int32, (tq, 1), 0)

    n_heads = 2 * pairs
    n_sub = tq // ts
    n_units = n_sub * n_heads

    @pl.when(j == 0)
    def _():
        m_scr[...] = jnp.full_like(m_scr, NEG)
        acc_scr[...] = jnp.zeros_like(acc_scr)
        blk = lax.broadcasted_iota(jnp.int32, (ts, LANES), 1)
        for sub in range(n_sub):
            rows = slice(sub * ts, (sub + 1) * ts)
            qblk = tpos[rows] // MOBA_BLOCK
            for c in range(pairs):
                if mode == "moba":
                    km_hi, km_lo = _split_bf16(km_ref[c])
                for half in range(2):
                    qm = jnp.where(halves[half], q_ref[c, rows, :], jnp.zeros((), BF16))
                    if mode == "moba":
                        gate = jnp.where(blk < qblk, _nt_dot(qm, km_hi) + _nt_dot(qm, km_lo), NEG)
                        rest, thr = gate, None
                        for _ in range(MOBA_TOPK):
                            thr = jnp.max(rest, axis=1, keepdims=True)
                            rest = jnp.where(rest >= thr, NEG, rest)
                        attended = ((gate >= thr) & (blk < qblk)) | (blk == qblk)
                        feat = jnp.where(attended, 0.0, -MASK_BIG)
                        if half == 0:
                            feat = pltpu.roll(feat, HEAD_DIM, 1)
                        qm = jnp.where(halves[half], qm, feat.astype(BF16))
                    qa_scr[sub * n_heads + 2 * c + half] = qm

    def attend(bias):
        if mode == "moba":
            kblk = (kt * tk + lax.broadcasted_iota(jnp.int32, (tk, 1), 0)) // MOBA_BLOCK
            oh_scr[0] = jnp.where(lane == kblk + HEAD_DIM, 1.0, 0.0).astype(BF16)
            oh_scr[1] = jnp.where(lane == kblk, 1.0, 0.0).astype(BF16)

        def scores(u, par):
            kp = k_ref[(u % n_heads) // 2]
            if mode == "moba":
                kp = jnp.where(halves[par], kp, oh_scr[par])
            s_scr[par][...] = _nt_dot(qa_scr[u], kp)

        def bias_rows(u):
            start = u // n_heads * ts
            if isinstance(start, int):
                return bias[start:start + ts, :]
            return bias[pl.ds(pl.multiple_of(start, ts), ts), :]

        def softmax(u, par):
            s = s_scr[par][...]
            if bias is not None:
                s = s + (bias[...] if n_sub == 1 else bias_rows(u))
            m_old = m_scr[u]
            m_new = jnp.maximum(m_old, jnp.max(s, axis=1, keepdims=True))
            p_scr[par][...] = jnp.exp2(s - m_new).astype(BF16)
            alpha_scr[par][...] = jnp.exp2(m_old - m_new)
            m_scr[u] = m_new

        def values(u, par):
            va = jnp.where(halves[par], v_ref[(u % n_heads) // 2], jnp.ones((), BF16))
            acc_scr[u] = alpha_scr[par][...] * acc_scr[u] + jnp.dot(p_scr[par][...], va, preferred_element_type=F32)

        scores(0, 0)
        scores(1, 1)
        softmax(0, 0)

        def body(c, carry):
            scores(2 * c + 2, 0)
            softmax(2 * c + 1, 1)
            values(2 * c, 0)
            scores(2 * c + 3, 1)
            softmax(2 * c + 2, 0)
            values(2 * c + 1, 1)
            return carry

        lax.fori_loop(0, n_units // 2 - 1, body, 0)
        softmax(n_units - 1, 1)
        values(n_units - 2, 0)
        values(n_units - 1, 1)

    if mode == "moba":
        on_diagonal = (kt + 1) * tk > i * tq

        @pl.when(active & on_diagonal)
        def _():
            kpos = kt * tk + lax.broadcasted_iota(jnp.int32, (tq, tk), 1)
            bias_scr[...] = jnp.where(kpos <= tpos, 0.0, -jnp.inf)
            attend(bias_scr)

        @pl.when(active & jnp.logical_not(on_diagonal))
        def _():
            attend(None)
    elif mode == "dsa":
        @pl.when(active)
        def _():
            for part in range(b_ref.shape[1]):
                width = b_ref.shape[3]
                bias_scr[:, part * width:(part + 1) * width] = b_ref[0, part].astype(F32)
            attend(bias_scr)
    else:
        @pl.when(active)
        def _():
            attend(b_ref.at[0])

    @pl.when(j == n_steps - 1)
    def _():
        for sub in range(n_sub):
            for c in range(pairs):
                a0, a1 = acc_scr[sub * n_heads + 2 * c], acc_scr[sub * n_heads + 2 * c + 1]
                num = jnp.where(halves[0], a0, a1)
                den = pltpu.roll(jnp.where(halves[0], a1, a0), HEAD_DIM, 1)
                o_ref[c, sub * ts:(sub + 1) * ts, :] = (num / den).astype(BF16)


def _dilated_bias_table(tq, tk, back, n_steps):
    r = np.arange(tq)[:, None]
    c = np.arange(tk)[None, :]
    tabs = []
    for j in range(n_steps):
        d = (back - j) * tk + r - c
        mult = sum(((d >= 0) & (d <= w) & (d % dil == 0)).astype(np.float64) for w, dil in DILATED_PAIRS)
        tabs.append(np.where(mult > 0, np.log2(np.maximum(mult, 1.0)), -np.inf))
    return jnp.asarray(np.stack(tabs), F32)


def _flash(mode, q_arr, k_arr, v_arr, q_off, k_off, v_off, pairs, B, S, *, tq, tk, ts=None, extra=None):
    ts = ts or tq
    nq, nk = S // tq, S // tk
    n_units = tq // ts * 2 * pairs
    assert S % tq == 0 and S % tk == 0 and tq % tk == 0 and tq % ts == 0
    assert q_off % pairs == 0 and k_off % pairs == 0 and v_off % pairs == 0
    if mode == "dilated":
        max_back = max(w for w, _ in DILATED_PAIRS)
        back = -(-max_back // tk)
        n_steps = back + tq // tk
        kv_tile = lambda i, j: jnp.maximum(i * (tq // tk) - back + j, 0)
    else:
        back = 0
        n_steps = nk
        kv_tile = lambda i, j: jnp.minimum(j, ((i + 1) * tq - 1) // tk)
    in_specs = [pl.BlockSpec((pairs, tq, LANES), lambda b, i, j: (q_off // pairs, b * nq + i, 0)),
                pl.BlockSpec((pairs, tk, LANES), lambda b, i, j: (k_off // pairs, b * nk + kv_tile(i, j), 0)),
                pl.BlockSpec((pairs, tk, LANES), lambda b, i, j: (v_off // pairs, b * nk + kv_tile(i, j), 0))]
    args = [q_arr, k_arr, v_arr]
    scratch = [pltpu.VMEM((n_units, ts, 1), F32), pltpu.VMEM((n_units, ts, LANES), F32),
               pltpu.VMEM((n_units, ts, LANES), BF16), pltpu.VMEM((ts, tk), F32), pltpu.VMEM((ts, tk), F32),
               pltpu.VMEM((ts, tk), BF16), pltpu.VMEM((ts, tk), BF16),
               pltpu.VMEM((ts, 1), F32), pltpu.VMEM((ts, 1), F32)]
    if mode == "moba":
        assert tk % MOBA_BLOCK == 0 and S // MOBA_BLOCK <= HEAD_DIM
        in_specs.append(pl.BlockSpec((pairs, LANES, LANES), lambda b, i, j: (0, b, 0)))
        args.append(extra)
        scratch += [pltpu.VMEM((tq, tk), F32), pltpu.VMEM((2, tk, LANES), BF16)]
    elif mode == "dsa":
        ck = extra.shape[3]
        assert tk % ck == 0
        in_specs.append(pl.BlockSpec((1, tk // ck, tq, ck), lambda b, i, j: (b, kv_tile(i, j), i, 0)))
        args.append(extra)
        scratch.append(pltpu.VMEM((tq, tk), F32))
    else:
        in_specs.append(pl.BlockSpec((1, tq, tk), lambda b, i, j: (j, 0, 0)))
        args.append(_dilated_bias_table(tq, tk, back, n_steps))
    return pl.pallas_call(
        functools.partial(_flash_kernel, mode=mode, pairs=pairs, tq=tq, ts=ts, tk=tk, n_steps=n_steps, back=back),
        grid=(B, nq, n_steps),
        in_specs=in_specs,
        out_specs=pl.BlockSpec((pairs, tq, LANES), lambda b, i, j: (0, b * nq + i, 0)),
        out_shape=jax.ShapeDtypeStruct((pairs, B * S, LANES), BF16),
        scratch_shapes=scratch,
        compiler_params=_cparams(("parallel", "parallel", "arbitrary"), 48),
        name="flash_" + mode,
    )(*args)


F32_SUBLANES = 8
BF16_SUBLANES = 16


def _dsa_select_kernel(qi_ref, ki_ref, wt_ref, o_ref, key_scr, top_scr, tri_scr, *, tq, ck, n_chunks, topk):
    i = pl.program_id(1)
    tri_scr[...] = jnp.where(lax.broadcasted_iota(jnp.int32, (ck, ck), 0) >= lax.broadcasted_iota(jnp.int32, (ck, ck), 1),
                             1.0, 0.0).astype(BF16)
    lane = lax.broadcasted_iota(jnp.int32, (1, LANES), 1)
    halves = (lane < HEAD_DIM, lane >= HEAD_DIM)
    tpos = i * tq + lax.broadcasted_iota(jnp.int32, (1, tq), 1)
    n_live = ((i + 1) * tq + ck - 1) // ck
    wt = wt_ref[...]

    def score_chunk(c, carry):
        kc = ki_ref[0, pl.ds(pl.multiple_of(c * ck, ck), ck), :]
        score = jnp.zeros((ck, tq), F32)
        for h in range(IDX_HEADS):
            qm = jnp.where(halves[h % 2], qi_ref[h // 2], jnp.zeros((), BF16))
            score = score + wt[h:h + 1, :] * jnp.maximum(_nt_dot(kc, qm), 0.0)
        kpos = c * ck + lax.broadcasted_iota(jnp.int32, (ck, tq), 0)
        score = jnp.where(jnp.abs(score) >= jnp.finfo(F32).tiny, score, 0.0)
        score = jnp.where(kpos <= tpos, score, NEG)
        bits = pltpu.bitcast(score, jnp.int32)
        key_scr[c] = jnp.where(bits < 0, bits ^ jnp.int32(0x7FFFFFFF), bits)
        top_scr[c] = pltpu.bitcast(bits & jnp.int32(-65536), F32).astype(BF16)
        return carry

    lax.fori_loop(0, n_live, score_chunk, 0)

    def count(src, pred, dtype, group):
        assert ck // group <= 256

        def body(c, acc):
            hit = jnp.where(pred(src[c]), jnp.ones((), dtype), jnp.zeros((), dtype))
            parts = [hit[s * group:(s + 1) * group, :] for s in range(ck // group)]
            while len(parts) > 1:
                parts = [a + b for a, b in zip(parts[0::2], parts[1::2])]
            return acc + parts[0].astype(F32)

        acc = lax.fori_loop(0, n_live, body, jnp.zeros((group, tq), F32))
        return jnp.sum(acc, axis=0, keepdims=True)

    def upper_bit(it, lo):
        cand = lo + jnp.left_shift(jnp.int32(1), 15 - it)
        pattern = jnp.where(cand < 0, cand ^ jnp.int32(0x7FFF), cand)
        pattern = jnp.where((pattern > 0) & (pattern < 0x80), jnp.int32(0x80), pattern)
        value = pltpu.bitcast(jnp.left_shift(pattern, 16), F32).astype(BF16)
        return jnp.where(count(top_scr, lambda sc: sc >= value, BF16, BF16_SUBLANES) >= topk, cand, lo)

    upper = lax.fori_loop(0, 16, upper_bit, jnp.full((1, tq), -2**15, jnp.int32))

    def lower_bit(it, lo):
        cand = lo + jnp.left_shift(jnp.int32(1), 15 - it)
        return jnp.where(count(key_scr, lambda kc: kc >= cand, F32, F32_SUBLANES) >= topk, cand, lo)

    thr = lax.fori_loop(0, 16, lower_bit, jnp.left_shift(upper, 16))
    need = topk - count(key_scr, lambda kc: kc > thr, F32, F32_SUBLANES)

    def emit(c, before):
        kc = key_scr[c]
        kpos = c * ck + lax.broadcasted_iota(jnp.int32, (ck, tq), 0)
        equal = kc == thr
        rank = before + jnp.dot(tri_scr[...], jnp.where(equal, 1.0, 0.0).astype(BF16), preferred_element_type=F32)
        take = ((kc > thr) | (equal & (rank <= need))) & (kpos <= tpos)
        o_ref[0, c] = jnp.where(take, 0.0, -jnp.inf).T.astype(BF16)
        return rank[ck - 1:ck, :]

    lax.fori_loop(0, n_live, emit, jnp.zeros((1, tq), F32))

    def blank(c, carry):
        o_ref[0, c] = jnp.full((tq, ck), -jnp.inf, BF16)
        return carry

    lax.fori_loop(n_live, n_chunks, blank, 0)


def _dsa_select(y, aux, qi_off, ki_off, B, S, *, tq=256, ck=512):
    n_chunks = S // ck
    assert S % ck == 0 and S % tq == 0 and qi_off % 2 == 0
    nq = S // tq
    topk = min(DSA_TOPK, S // 4)
    wt = aux[:, :F32_SUBLANES].T
    return pl.pallas_call(
        functools.partial(_dsa_select_kernel, tq=tq, ck=ck, n_chunks=n_chunks, topk=topk),
        grid=(B, nq),
        in_specs=[pl.BlockSpec((2, tq, LANES), lambda b, i: (qi_off // 2, b * nq + i, 0)),
                  pl.BlockSpec((1, S, LANES), lambda b, i: (ki_off, b, 0)),
                  pl.BlockSpec((F32_SUBLANES, tq), lambda b, i: (0, b * nq + i))],
        out_specs=pl.BlockSpec((1, n_chunks, tq, ck), lambda b, i: (b, 0, i, 0)),
        out_shape=jax.ShapeDtypeStruct((B, n_chunks, S, ck), BF16),
        scratch_shapes=[pltpu.VMEM((n_chunks, ck, tq), jnp.int32), pltpu.VMEM((n_chunks, ck, tq), BF16),
                        pltpu.VMEM((ck, ck), BF16)],
        compiler_params=_cparams(("parallel", "parallel"), 48),
        name="dsa_select",
    )(y, y, wt)


def _out_proj_kernel(*refs, n_in):
    o_refs, w_ref, h_ref, out_ref = refs[:n_in], refs[n_in], refs[n_in + 1], refs[n_in + 2]
    slabs = [o[c] for o in o_refs for c in range(o.shape[0])]
    o = jnp.concatenate(slabs, axis=-1)
    out_ref[...] = h_ref[...] + jnp.dot(o, w_ref[...], preferred_element_type=F32)


def _out_proj(os_, w_bf16, h, *, tm=512):
    T, D = h.shape
    width = sum(o.shape[0] for o in os_) * LANES
    assert w_bf16.shape == (width, D)
    in_specs = [pl.BlockSpec((o.shape[0], tm, LANES), lambda i: (0, i, 0)) for o in os_]
    in_specs += [pl.BlockSpec((width, D), lambda i: (0, 0)), pl.BlockSpec((tm, D), lambda i: (i, 0))]
    return pl.pallas_call(
        functools.partial(_out_proj_kernel, n_in=len(os_)),
        grid=(T // tm,),
        in_specs=in_specs,
        out_specs=pl.BlockSpec((tm, D), lambda i: (i, 0)),
        out_shape=jax.ShapeDtypeStruct((T, D), F32),
        compiler_params=_cparams(("parallel",), 32),
        name="out_proj",
    )(*os_, w_bf16, h)


def _top_values(s, k):
    out = []
    for _ in range(k + 1):
        cur = jnp.max(s, axis=0, keepdims=True)
        out.append(cur)
        s = jnp.where(s >= cur, -jnp.inf, s)
    return jnp.concatenate(out[:k], axis=0), out[k]


def _pair_candidates(t1, t2):
    k = PEER_TOPK
    parts = [t1[0:1] + t2]
    parts += [t1[a:a + 1] + t2[0:k // 2] for a in range(1, k // 2)]
    parts.append(t1[k // 2:k] + t2[0:1])
    return jnp.concatenate(parts, axis=0)


def _kth_largest(c, k):
    left = jnp.full((1, c.shape[1]), float(k), F32)
    tau = jnp.zeros((1, c.shape[1]), F32)
    for _ in range(k):
        cur = jnp.max(c, axis=0, keepdims=True)
        hit = c >= cur
        tau = jnp.where(left > 0.0, cur, tau)
        left = left - jnp.sum(jnp.where(hit, 1.0, 0.0), axis=0, keepdims=True)
        c = jnp.where(hit, -jnp.inf, c)
    return tau


def _peer_route_kernel(h_ref, g_ref, wq_ref, sk_ref, xnt_ref, th1_ref, e1_ref, e2_ref):
    xn = _rms(h_ref[...], g_ref[...])
    xnt_ref[...] = xn.T.astype(BF16)
    xn = xn.astype(BF16)
    for hd in range(PEER_HEADS):
        q = jnp.dot(xn, wq_ref[:, hd * 2 * N_KEYS:(hd + 1) * 2 * N_KEYS], preferred_element_type=F32).astype(BF16)
        s1 = _nt_dot(sk_ref[2 * hd], q[:, :N_KEYS])
        s2 = _nt_dot(sk_ref[2 * hd + 1], q[:, N_KEYS:])
        s1 = s1 - jnp.max(s1, axis=0, keepdims=True)
        s2 = s2 - jnp.max(s2, axis=0, keepdims=True)
        (t1, next1), (t2, next2) = _top_values(s1, PEER_TOPK), _top_values(s2, PEER_TOPK)
        cand = _pair_candidates(t1, t2)
        chosen = cand >= _kth_largest(cand, PEER_TOPK)
        log_z = jnp.log(jnp.sum(jnp.where(chosen, jnp.exp(cand), 0.0), axis=0, keepdims=True))
        runner_up = jnp.maximum(jnp.max(jnp.where(chosen, -jnp.inf, cand), axis=0, keepdims=True),
                                jnp.maximum(next1 + t2[0:1], t1[0:1] + next2))
        smallest = jnp.min(jnp.where(chosen, cand, jnp.inf), axis=0, keepdims=True)
        tau = 0.5 * (smallest + runner_up)
        s1 = s1 - log_z
        th1_ref[hd] = jnp.exp((tau - log_z) - s1)
        e1_ref[hd] = jnp.exp(s1)
        e2_ref[hd] = jnp.exp(s2)


def _peer_route(h, g, wq_bf16, sk_bf16, *, tm=256):
    T, D = h.shape
    nq = PEER_HEADS * 2 * N_KEYS
    assert wq_bf16.shape == (D, nq) and sk_bf16.shape == (2 * PEER_HEADS, N_KEYS, N_KEYS)
    return pl.pallas_call(
        _peer_route_kernel,
        grid=(T // tm,),
        in_specs=[pl.BlockSpec((tm, D), lambda i: (i, 0)), pl.BlockSpec((1, D), lambda i: (0, 0)),
                  pl.BlockSpec((D, nq), lambda i: (0, 0)),
                  pl.BlockSpec((2 * PEER_HEADS, N_KEYS, N_KEYS), lambda i: (0, 0, 0))],
        out_specs=[pl.BlockSpec((D, tm), lambda i: (0, i))]
        + [pl.BlockSpec((PEER_HEADS, N_KEYS, tm), lambda i: (0, 0, i))] * 3,
        out_shape=[jax.ShapeDtypeStruct((D, T), BF16)]
        + [jax.ShapeDtypeStruct((PEER_HEADS, N_KEYS, T), F32)] * 3,
        compiler_params=_cparams(("parallel",), 40),
        name="peer_route",
    )(h, g.reshape(1, D), wq_bf16, sk_bf16)


def _gelu_tanh(x):
    return 0.5 * x * (1.0 + jnp.tanh(math.sqrt(2.0 / math.pi) * (x + 0.044715 * (x * x * x))))


def _peer_expert_kernel(xnt_ref, u_ref, vt_ref, th1_ref, e1_ref, e2_ref, h_ref, out_ref,
                        acc_scr, rows_scr, act0_scr, act1_scr, w0_scr, w1_scr, *, eb, cuts, tq):
    e = pl.program_id(1)
    rows, n_sub = eb // N_KEYS, len(cuts) - 1
    sublanes = rows_scr.shape[2]
    act_scr, w_scr = (act0_scr, act1_scr), (w0_scr, w1_scr)

    @pl.when(e == 0)
    def _():
        acc_scr[...] = jnp.zeros_like(acc_scr)

    row0 = pl.multiple_of(e * rows, rows)
    for k, tab in enumerate((th1_ref, e1_ref)):
        for hd in range(PEER_HEADS):
            block = tab[hd, pl.ds(row0, rows), :]
            for r in range(rows):
                rows_scr[k, hd * rows + r] = jnp.broadcast_to(block[r:r + 1, :], (sublanes, tq))

    def activations(sb):
        lo, hi = cuts[sb], cuts[sb + 1]
        a = jnp.dot(u_ref[lo:hi, :], xnt_ref[...], preferred_element_type=F32)
        act_scr[sb % 2][0:hi - lo, :] = _gelu_tanh(a.astype(BF16))

    def weights(sb):
        span = 64
        reps = (span // sublanes, 1)
        for r in range((cuts[sb + 1] - cuts[sb]) // N_KEYS):
            for t0 in range(0, tq, LANES):
                for i2 in range(0, N_KEYS, span):
                    w = jnp.zeros((span, LANES), BF16)
                    for hd in range(PEER_HEADS):
                        k = hd * rows + cuts[sb] // N_KEYS + r
                        th1 = jnp.tile(rows_scr[0, k, :, t0:t0 + LANES], reps)
                        e1 = jnp.tile(rows_scr[1, k, :, t0:t0 + LANES], reps).astype(BF16)
                        e2 = e2_ref[hd, i2:i2 + span, t0:t0 + LANES]
                        w = w + e1 * jnp.where(e2 >= th1, e2, 0.0).astype(BF16)
                    lo = r * N_KEYS + i2
                    w_scr[sb % 2][lo:lo + span, t0:t0 + LANES] = w

    def project(sb):
        lo, hi = cuts[sb], cuts[sb + 1]
        a = act_scr[sb % 2][0:hi - lo, :] * w_scr[sb % 2][0:hi - lo, :]
        acc_scr[...] += jnp.dot(vt_ref[:, lo:hi], a, preferred_element_type=F32)

    weights(0)
    activations(0)
    for sb in range(n_sub):
        if sb + 1 < n_sub:
            weights(sb + 1)
            activations(sb + 1)
        project(sb)

    @pl.when(e == pl.num_programs(1) - 1)
    def _():
        out_ref[...] = h_ref[...] + acc_scr[...].T


def _peer_experts(xn, u_bf16, vt_bf16, tables, h, *, tq=512, eb=2048, cuts=(0, 1024, 2048)):
    T, D = h.shape
    E = u_bf16.shape[0]
    assert E == N_KEYS * N_KEYS and E % eb == 0 and eb % (F32_SUBLANES * N_KEYS) == 0 and T % tq == 0
    assert cuts[0] == 0 and cuts[-1] == eb and all(c % N_KEYS == 0 for c in cuts)
    sub = max(b - a for a, b in zip(cuts[:-1], cuts[1:]))
    table_spec = pl.BlockSpec((PEER_HEADS, N_KEYS, tq), lambda i, e: (0, 0, i))
    return pl.pallas_call(
        functools.partial(_peer_expert_kernel, eb=eb, cuts=tuple(cuts), tq=tq),
        grid=(T // tq, E // eb),
        in_specs=[pl.BlockSpec((D, tq), lambda i, e: (0, i)),
                  pl.BlockSpec((eb, D), lambda i, e: (e, 0)),
                  pl.BlockSpec((D, eb), lambda i, e: (0, e)),
                  table_spec, table_spec, table_spec,
                  pl.BlockSpec((tq, D), lambda i, e: (i, 0))],
        out_specs=pl.BlockSpec((tq, D), lambda i, e: (i, 0)),
        out_shape=jax.ShapeDtypeStruct((T, D), F32),
        scratch_shapes=[pltpu.VMEM((D, tq), F32),
                        pltpu.VMEM((2, PEER_HEADS * (eb // N_KEYS), F32_SUBLANES, tq), F32),
                        pltpu.VMEM((sub, tq), BF16), pltpu.VMEM((sub, tq), BF16),
                        pltpu.VMEM((sub, tq), BF16), pltpu.VMEM((sub, tq), BF16)],
        compiler_params=_cparams(("parallel", "arbitrary"), 56),
        name="peer_experts",
    )(xn, u_bf16, vt_bf16, *tables, h)


def _ple_kernel(h_ref, g_ref, p_ref, wg_ref, wu_ref, out_ref):
    h = h_ref[...]
    gate = jax.nn.sigmoid(jnp.dot(_rms(h, g_ref[...]).astype(BF16), wg_ref[...], preferred_element_type=F32))
    up = jnp.dot(p_ref[...].astype(BF16), wu_ref[...], preferred_element_type=F32)
    out_ref[...] = h + up * gate


def _ple(h, g, p, wg_bf16, wu_bf16, *, tm=512):
    T, D = h.shape
    P = p.shape[1]
    return pl.pallas_call(
        _ple_kernel,
        grid=(T // tm,),
        in_specs=[pl.BlockSpec((tm, D), lambda i: (i, 0)), pl.BlockSpec((1, D), lambda i: (0, 0)),
                  pl.BlockSpec((tm, P), lambda i: (i, 0)), pl.BlockSpec((D, D), lambda i: (0, 0)),
                  pl.BlockSpec((P, D), lambda i: (0, 0))],
        out_specs=pl.BlockSpec((tm, D), lambda i: (i, 0)),
        out_shape=jax.ShapeDtypeStruct((T, D), F32),
        compiler_params=_cparams(("parallel",), 32),
        name="ple",
    )(h, g.reshape(1, D), p, wg_bf16, wu_bf16)


def _tile_gain(g, scale=1.0):
    return jnp.tile(g.astype(F32) * scale, LANES // HEAD_DIM)


def _ab_mixer(h, rope, B, S, g, w_in, w_out, a_q, a_k, b_q, b_k):
    A, Bw = A_HEADS * HEAD_DIM, B_HEADS * HEAD_DIM
    cuts = np.cumsum([A, A, A, Bw, Bw, Bw, IDX_HEADS * HEAD_DIM, HEAD_DIM, IDX_HEADS])
    base, ki0, wi0 = w_in[:, :cuts[6]], w_in[:, cuts[6]:cuts[7]], w_in[:, cuts[7]:cuts[8]]
    w_pad = jnp.concatenate([base, ki0, ki0, wi0, jnp.zeros((D_MODEL, LANES - IDX_HEADS), F32)], axis=1).astype(BF16)
    scale = HEAD_DIM ** -0.5 * LOG2E
    kinds = ["nr"] * 8 + ["p"] * 4 + ["nr"] * 8 + ["p"] * 4 + ["r"] * 3 + ["pa"]
    one = jnp.ones((LANES,), F32)
    gains = jnp.stack([_tile_gain(a_q, scale)] * 4 + [_tile_gain(a_k)] * 4 + [one] * 4
                      + [_tile_gain(b_q, scale)] * 4 + [_tile_gain(b_k)] * 4 + [one] * 8)
    y, aux = _project(h, g, w_pad, gains, rope, kinds)
    kmean = _block_means(y, 4, 4, B, S)
    tq = min(1024, S)
    oa = _flash("moba", y, y, y, 0, 4, 8, 4, B, S, tq=tq, ts=512, tk=tq, extra=kmean)
    bias = _dsa_select(y, aux, 24, 26, B, S)
    ob = _flash("dsa", y, y, y, 12, 16, 20, 4, B, S, tq=tq, ts=512, tk=tq, extra=bias)
    return _out_proj([oa, ob], w_out.astype(BF16), h)


def _c_mixer(h, rope, B, S, g, w_in, w_out, q_norm, k_norm):
    scale = HEAD_DIM ** -0.5 * LOG2E
    kinds = ["nr"] * 16 + ["p"] * 7 + ["pa"]
    one = jnp.ones((LANES,), F32)
    gains = jnp.stack([_tile_gain(q_norm, scale)] * 8 + [_tile_gain(k_norm)] * 8 + [one] * 8)
    y, _ = _project(h, g, w_in.astype(BF16), gains, rope, kinds)
    o = _flash("dilated", y, y, y, 0, 8, 16, 8, B, S, tq=512, tk=512)
    return _out_proj([o], w_out.astype(BF16), h)


def _peer(h, g, w_query, sub_keys, expert_u, expert_v):
    sk = sub_keys.reshape(2 * PEER_HEADS, N_KEYS, N_KEYS).astype(BF16)
    xn, *tables = _peer_route(h, g, w_query.astype(BF16), sk)
    return _peer_experts(xn, expert_u.astype(BF16), expert_v.T.astype(BF16), tables, h)


def kernel(x, p, positions, attn_norm, ffn_norm, ple_norm, ab_w_in, ab_w_out, a_q_norm, a_k_norm, b_q_norm,
           b_k_norm, c_w_in, c_w_out, c_q_norm, c_k_norm, peer_w_query, peer_sub_keys, peer_u, peer_v,
           ple_w_gate, ple_w_up):
    B, S, D = x.shape
    depth = p.shape[0]
    rope = _rope_tables(positions)
    h = x.reshape(B * S, D)
    for i in range(depth):
        j = i // 2
        if i % 2 == 0:
            h = _ab_mixer(h, rope, B, S, attn_norm[i], ab_w_in[j], ab_w_out[j],
                          a_q_norm[j], a_k_norm[j], b_q_norm[j], b_k_norm[j])
        else:
            h = _c_mixer(h, rope, B, S, attn_norm[i], c_w_in[j], c_w_out[j], c_q_norm[j], c_k_norm[j])
        h = _peer(h, ffn_norm[i], peer_w_query[i], peer_sub_keys[i], peer_u[i], peer_v[i])
        h = _ple(h, ple_norm[i], p[i].reshape(B * S, -1), ple_w_gate[i].astype(BF16), ple_w_up[i].astype(BF16))
    return h.reshape(B, S, D)
```

```python
import functools
import math

import numpy as np
import jax
import jax.numpy as jnp
from jax import lax
from jax.experimental import pallas as pl
from jax.experimental.pallas import tpu as pltpu

F32 = jnp.float32
BF16 = jnp.bfloat16

D_MODEL = 1024
HEAD_DIM = 64
ROT_HALF = 8
ROPE_THETA = 500000.0
NORM_EPS = 1e-6
NEG = -1e30
LOG2E = math.log2(math.e)
LANES = 128

A_HEADS = 8
B_HEADS = 8
C_HEADS = 16
IDX_HEADS = 4
MOBA_BLOCK = 256
MOBA_TOPK = 3
DSA_TOPK = 256
DILATED_PAIRS = ((128, 1), (512, 4), (2048, 16))
PEER_HEADS = 8
N_KEYS = 128
PEER_TOPK = 16

V7X_VMEM_BYTES = 64 * 1024 * 1024


def _cparams(sem, vmem_mb):
    assert vmem_mb * 2**20 < V7X_VMEM_BYTES
    return pltpu.CompilerParams(dimension_semantics=sem, vmem_limit_bytes=vmem_mb * 2**20)


def _rms(x, g):
    ms = jnp.mean(x * x, axis=-1, keepdims=True)
    return x * lax.rsqrt(ms + NORM_EPS) * g


def _nt_dot(a, b):
    return lax.dot_general(a, b, (((1,), (1,)), ((), ())), preferred_element_type=F32)


def _split_bf16(x):
    hi = x.astype(BF16)
    lo = (x - hi.astype(F32)).astype(BF16)
    return hi, lo


def _proj_kernel(x_ref, g_ref, w_ref, gains_ref, cos_ref, sa_ref, sb_ref, y_ref, aux_ref, *, kinds, group):
    xn = _rms(x_ref[...], g_ref[...]).astype(BF16)
    row = lax.broadcasted_iota(jnp.int32, (LANES, LANES), 0) // HEAD_DIM
    col = lax.broadcasted_iota(jnp.int32, (LANES, LANES), 1) // HEAD_DIM
    head_avg = jnp.where(row == col, 1.0 / HEAD_DIM, 0.0).astype(BF16)
    n_chunks = len(kinds)
    for g0 in range(0, n_chunks, group):
        g1 = min(g0 + group, n_chunks)
        yg = jnp.dot(xn, w_ref[:, g0 * LANES:g1 * LANES], preferred_element_type=F32)
        for c in range(g0, g1):
            y = yg[:, (c - g0) * LANES:(c - g0 + 1) * LANES]
            kind = kinds[c]
            if "n" in kind:
                hi, lo = _split_bf16(y * y)
                msq = (jnp.dot(hi, head_avg, preferred_element_type=F32)
                       + jnp.dot(lo, head_avg, preferred_element_type=F32))
                y = y * lax.rsqrt(msq + NORM_EPS) * gains_ref[c:c + 1, :]
            if "r" in kind:
                y = (y * cos_ref[...] + pltpu.roll(y, LANES - ROT_HALF, 1) * sa_ref[...]
                     + pltpu.roll(y, ROT_HALF, 1) * sb_ref[...])
            if "a" in kind:
                aux_ref[...] = y
            y_ref[c] = y.astype(BF16)


def _project(h, g, w_bf16, gains, rope, kinds, *, tm=512, group=4):
    T, D = h.shape
    n_chunks = len(kinds)
    assert w_bf16.shape == (D, n_chunks * LANES) and T % tm == 0
    cos, sa, sb = rope
    tok = lambda i: (i, 0)
    fixed = lambda i: (0, 0)
    return pl.pallas_call(
        functools.partial(_proj_kernel, kinds=tuple(kinds), group=group),
        grid=(T // tm,),
        in_specs=[pl.BlockSpec((tm, D), tok), pl.BlockSpec((1, D), fixed),
                  pl.BlockSpec((D, n_chunks * LANES), fixed), pl.BlockSpec((n_chunks, LANES), fixed),
                  pl.BlockSpec((tm, LANES), tok), pl.BlockSpec((tm, LANES), tok), pl.BlockSpec((tm, LANES), tok)],
        out_specs=[pl.BlockSpec((n_chunks, tm, LANES), lambda i: (0, i, 0)), pl.BlockSpec((tm, LANES), tok)],
        out_shape=[jax.ShapeDtypeStruct((n_chunks, T, LANES), BF16), jax.ShapeDtypeStruct((T, LANES), F32)],
        compiler_params=_cparams(("parallel",), 48),
        name="proj",
    )(h, g.reshape(1, D), w_bf16, gains, cos, sa, sb)


def _rope_tables(positions):
    B, S = positions.shape
    inv = ROPE_THETA ** (-jnp.arange(ROT_HALF, dtype=F32) / ROT_HALF)
    ang = positions.astype(F32)[..., None] * inv
    cos, sin = jnp.cos(ang), jnp.sin(ang)
    rest = HEAD_DIM - 2 * ROT_HALF
    one, zero, z8 = jnp.ones((B, S, rest), F32), jnp.zeros((B, S, rest), F32), jnp.zeros((B, S, ROT_HALF), F32)
    tile = lambda parts: jnp.tile(jnp.concatenate(parts, axis=-1), (1, 1, LANES // HEAD_DIM)).reshape(B * S, LANES)
    return tile([cos, cos, one]), tile([-sin, z8, zero]), tile([z8, sin, zero])


def _kmean_kernel(k_ref, o_ref):
    n = pl.program_id(1)

    @pl.when(n == 0)
    def _():
        o_ref[...] = jnp.zeros_like(o_ref)

    o_ref[:, pl.ds(n, 1), :] = jnp.mean(k_ref[...].astype(F32), axis=1, keepdims=True)


def _block_means(y, k_off, pairs, B, S):
    nb = S // MOBA_BLOCK
    assert nb <= LANES and k_off % pairs == 0
    return pl.pallas_call(
        _kmean_kernel,
        grid=(B, nb),
        in_specs=[pl.BlockSpec((pairs, MOBA_BLOCK, LANES), lambda b, n: (k_off // pairs, b * nb + n, 0))],
        out_specs=pl.BlockSpec((pairs, LANES, LANES), lambda b, n: (0, b, 0)),
        out_shape=jax.ShapeDtypeStruct((pairs, B * LANES, LANES), F32),
        compiler_params=_cparams(("parallel", "arbitrary"), 16),
        name="moba_kmean",
    )(y)


MASK_BIG = 32768.0


def _flash_kernel(*refs, mode, pairs, tq, ts, tk, n_steps, back):
    q_ref, k_ref, v_ref, x_ref, o_ref, m_scr, acc_scr, qa_scr = refs[:8]
    s_scr, p_scr, alpha_scr = refs[8:10], refs[10:12], refs[12:14]
    if mode == "dilated":
        b_ref = x_ref
    elif mode == "dsa":
        b_ref, bias_scr = x_ref, refs[14]
    else:
        km_ref, bias_scr, oh_scr = x_ref, refs[14], refs[15]
    i = pl.program_id(1)
    j = pl.program_id(2)
    lane = lax.broadcasted_iota(jnp.int32, (1, LANES), 1)
    halves = (lane < HEAD_DIM, lane >= HEAD_DIM)
    if mode == "dilated":
        kt = i * (tq // tk) - back + j
        active = kt >= 0
    else:
        kt = j
        active = j * tk < (i + 1) * tq
    tpos = i * tq + lax.broadcasted_iota(jnp.int32, (tq, 1), 0)

    n_heads = 2 * pairs
    n_sub = tq // ts
    n_units = n_sub * n_heads

    @pl.when(j == 0)
    def _():
        m_scr[...] = jnp.full_like(m_scr, NEG)
        acc_scr[...] = jnp.zeros_like(acc_scr)
        blk = lax.broadcasted_iota(jnp.int32, (ts, LANES), 1)
        for sub in range(n_sub):
            rows = slice(sub * ts, (sub + 1) * ts)
            qblk = tpos[rows] // MOBA_BLOCK
            for c in range(pairs):
                if mode == "moba":
                    km_hi, km_lo = _split_bf16(km_ref[c])
                for half in range(2):
                    qm = jnp.where(halves[half], q_ref[c, rows, :], jnp.zeros((), BF16))
                    if mode == "moba":
                        gate = jnp.where(blk < qblk, _nt_dot(qm, km_hi) + _nt_dot(qm, km_lo), NEG)
                        rest, thr = gate, None
                        for _ in range(MOBA_TOPK):
                            thr = jnp.max(rest, axis=1, keepdims=True)
                            rest = jnp.where(rest >= thr, NEG, rest)
                        attended = ((gate >= thr) & (blk < qblk)) | (blk == qblk)
                        feat = jnp.where(attended, 0.0, -MASK_BIG)
                        if half == 0:
                            feat = pltpu.roll(feat, HEAD_DIM, 1)
                        qm = jnp.where(halves[half], qm, feat.astype(BF16))
                    qa_scr[sub * n_heads + 2 * c + half] = qm

    def attend(bias):
        if mode == "moba":
            kblk = (kt * tk + lax.broadcasted_iota(jnp.int32, (tk, 1), 0)) // MOBA_BLOCK
            oh_scr[0] = jnp.where(lane == kblk + HEAD_DIM, 1.0, 0.0).astype(BF16)
            oh_scr[1] = jnp.where(lane == kblk, 1.0, 0.0).astype(BF16)

        def scores(u, par):
            kp = k_ref[(u % n_heads) // 2]
            if mode == "moba":
                kp = jnp.where(halves[par], kp, oh_scr[par])
            s_scr[par][...] = _nt_dot(qa_scr[u], kp)

        def bias_rows(u):
            start = u // n_heads * ts
            if isinstance(start, int):
                return bias[start:start + ts, :]
            return bias[pl.ds(pl.multiple_of(start, ts), ts), :]

        def softmax(u, par):
            s = s_scr[par][...]
            if bias is not None:
                s = s + (bias[...] if n_sub == 1 else bias_rows(u))
            m_old = m_scr[u]
            m_new = jnp.maximum(m_old, jnp.max(s, axis=1, keepdims=True))
            p_scr[par][...] = jnp.exp2(s - m_new).astype(BF16)
            alpha_scr[par][...] = jnp.exp2(m_old - m_new)
            m_scr[u] = m_new

        def values(u, par):
            va = jnp.where(halves[par], v_ref[(u % n_heads) // 2], jnp.ones((), BF16))
            acc_scr[u] = alpha_scr[par][...] * acc_scr[u] + jnp.dot(p_scr[par][...], va, preferred_element_type=F32)

        scores(0, 0)
        scores(1, 1)
        softmax(0, 0)

        def body(c, carry):
            scores(2 * c + 2, 0)
            softmax(2 * c + 1, 1)
            values(2 * c, 0)
            scores(2 * c + 3, 1)
            softmax(2 * c + 2, 0)
            values(2 * c + 1, 1)
            return carry

        lax.fori_loop(0, n_units // 2 - 1, body, 0)
        softmax(n_units - 1, 1)
        values(n_units - 2, 0)
        values(n_units - 1, 1)

    if mode == "moba":
        on_diagonal = (kt + 1) * tk > i * tq

        @pl.when(active & on_diagonal)
        def _():
            kpos = kt * tk + lax.broadcasted_iota(jnp.int32, (tq, tk), 1)
            bias_scr[...] = jnp.where(kpos <= tpos, 0.0, -jnp.inf)
            attend(bias_scr)

        @pl.when(active & jnp.logical_not(on_diagonal))
        def _():
            attend(None)
    elif mode == "dsa":
        @pl.when(active)
        def _():
            for part in range(b_ref.shape[1]):
                width = b_ref.shape[3]
                bias_scr[:, part * width:(part + 1) * width] = b_ref[0, part].astype(F32)
            attend(bias_scr)
    else:
        @pl.when(active)
        def _():
            attend(b_ref.at[0])

    @pl.when(j == n_steps - 1)
    def _():
        for sub in range(n_sub):
            for c in range(pairs):
                a0, a1 = acc_scr[sub * n_heads + 2 * c], acc_scr[sub * n_heads + 2 * c + 1]
                num = jnp.where(halves[0], a0, a1)
                den = pltpu.roll(jnp.where(halves[0], a1, a0), HEAD_DIM, 1)
                o_ref[c, sub * ts:(sub + 1) * ts, :] = (num / den).astype(BF16)


def _dilated_bias_table(tq, tk, back, n_steps):
    r = np.arange(tq)[:, None]
    c = np.arange(tk)[None, :]
    tabs = []
    for j in range(n_steps):
        d = (back - j) * tk + r - c
        mult = sum(((d >= 0) & (d <= w) & (d % dil == 0)).astype(np.float64) for w, dil in DILATED_PAIRS)
        tabs.append(np.where(mult > 0, np.log2(np.maximum(mult, 1.0)), -np.inf))
    return jnp.asarray(np.stack(tabs), F32)


def _flash(mode, q_arr, k_arr, v_arr, q_off, k_off, v_off, pairs, B, S, *, tq, tk, ts=None, extra=None):
    ts = ts or tq
    nq, nk = S // tq, S // tk
    n_units = tq // ts * 2 * pairs
    assert S % tq == 0 and S % tk == 0 and tq % tk == 0 and tq % ts == 0
    assert q_off % pairs == 0 and k_off % pairs == 0 and v_off % pairs == 0
    if mode == "dilated":
        max_back = max(w for w, _ in DILATED_PAIRS)
        back = -(-max_back // tk)
        n_steps = back + tq // tk
        kv_tile = lambda i, j: jnp.maximum(i * (tq // tk) - back + j, 0)
    else:
        back = 0
        n_steps = nk
        kv_tile = lambda i, j: jnp.minimum(j, ((i + 1) * tq - 1) // tk)
    in_specs = [pl.BlockSpec((pairs, tq, LANES), lambda b, i, j: (q_off // pairs, b * nq + i, 0)),
                pl.BlockSpec((pairs, tk, LANES), lambda b, i, j: (k_off // pairs, b * nk + kv_tile(i, j), 0)),
                pl.BlockSpec((pairs, tk, LANES), lambda b, i, j: (v_off // pairs, b * nk + kv_tile(i, j), 0))]
    args = [q_arr, k_arr, v_arr]
    scratch = [pltpu.VMEM((n_units, ts, 1), F32), pltpu.VMEM((n_units, ts, LANES), F32),
               pltpu.VMEM((n_units, ts, LANES), BF16), pltpu.VMEM((ts, tk), F32), pltpu.VMEM((ts, tk), F32),
               pltpu.VMEM((ts, tk), BF16), pltpu.VMEM((ts, tk), BF16),
               pltpu.VMEM((ts, 1), F32), pltpu.VMEM((ts, 1), F32)]
    if mode == "moba":
        assert tk % MOBA_BLOCK == 0 and S // MOBA_BLOCK <= HEAD_DIM
        in_specs.append(pl.BlockSpec((pairs, LANES, LANES), lambda b, i, j: (0, b, 0)))
        args.append(extra)
        scratch += [pltpu.VMEM((tq, tk), F32), pltpu.VMEM((2, tk, LANES), BF16)]
    elif mode == "dsa":
        ck = extra.shape[3]
        assert tk % ck == 0
        in_specs.append(pl.BlockSpec((1, tk // ck, tq, ck), lambda b, i, j: (b, kv_tile(i, j), i, 0)))
        args.append(extra)
        scratch.append(pltpu.VMEM((tq, tk), F32))
    else:
        in_specs.append(pl.BlockSpec((1, tq, tk), lambda b, i, j: (j, 0, 0)))
        args.append(_dilated_bias_table(tq, tk, back, n_steps))
    return pl.pallas_call(
        functools.partial(_flash_kernel, mode=mode, pairs=pairs, tq=tq, ts=ts, tk=tk, n_steps=n_steps, back=back),
        grid=(B, nq, n_steps),
        in_specs=in_specs,
        out_specs=pl.BlockSpec((pairs, tq, LANES), lambda b, i, j: (0, b * nq + i, 0)),
        out_shape=jax.ShapeDtypeStruct((pairs, B * S, LANES), BF16),
        scratch_shapes=scratch,
        compiler_params=_cparams(("parallel", "parallel", "arbitrary"), 48),
        name="flash_" + mode,
    )(*args)


F32_SUBLANES = 8
BF16_SUBLANES = 16


def _dsa_select_kernel(qi_ref, ki_ref, wt_ref, o_ref, key_scr, top_scr, tri_scr, *, tq, ck, n_chunks, topk):
    i = pl.program_id(1)
    tri_scr[...] = jnp.where(lax.broadcasted_iota(jnp.int32, (ck, ck), 0) >= lax.broadcasted_iota(jnp.int32, (ck, ck), 1),
                             1.0, 0.0).astype(BF16)
    lane = lax.broadcasted_iota(jnp.int32, (1, LANES), 1)
    halves = (lane < HEAD_DIM, lane >= HEAD_DIM)
    tpos = i * tq + lax.broadcasted_iota(jnp.int32, (1, tq), 1)
    n_live = ((i + 1) * tq + ck - 1) // ck
    wt = wt_ref[...]

    def score_chunk(c, carry):
        kc = ki_ref[0, pl.ds(pl.multiple_of(c * ck, ck), ck), :]
        score = jnp.zeros((ck, tq), F32)
        for h in range(IDX_HEADS):
            qm = jnp.where(halves[h % 2], qi_ref[h // 2], jnp.zeros((), BF16))
            score = score + wt[h:h + 1, :] * jnp.maximum(_nt_dot(kc, qm), 0.0)
        kpos = c * ck + lax.broadcasted_iota(jnp.int32, (ck, tq), 0)
        score = jnp.where(jnp.abs(score) >= jnp.finfo(F32).tiny, score, 0.0)
        score = jnp.where(kpos <= tpos, score, NEG)
        bits = pltpu.bitcast(score, jnp.int32)
        key_scr[c] = jnp.where(bits < 0, bits ^ jnp.int32(0x7FFFFFFF), bits)
        top_scr[c] = pltpu.bitcast(bits & jnp.int32(-65536), F32).astype(BF16)
        return carry

    lax.fori_loop(0, n_live, score_chunk, 0)

    def count(src, pred, dtype, group):
        assert ck // group <= 256

        def body(c, acc):
            hit = jnp.where(pred(src[c]), jnp.ones((), dtype), jnp.zeros((), dtype))
            parts = [hit[s * group:(s + 1) * group, :] for s in range(ck // group)]
            while len(parts) > 1:
                parts = [a + b for a, b in zip(parts[0::2], parts[1::2])]
            return acc + parts[0].astype(F32)

        acc = lax.fori_loop(0, n_live, body, jnp.zeros((group, tq), F32))
        return jnp.sum(acc, axis=0, keepdims=True)

    def upper_bit(it, lo):
        cand = lo + jnp.left_shift(jnp.int32(1), 15 - it)
        pattern = jnp.where(cand < 0, cand ^ jnp.int32(0x7FFF), cand)
        pattern = jnp.where((pattern > 0) & (pattern < 0x80), jnp.int32(0x80), pattern)
        value = pltpu.bitcast(jnp.left_shift(pattern, 16), F32).astype(BF16)
        return jnp.where(count(top_scr, lambda sc: sc >= value, BF16, BF16_SUBLANES) >= topk, cand, lo)

    upper = lax.fori_loop(0, 16, upper_bit, jnp.full((1, tq), -2**15, jnp.int32))

    def lower_bit(it, lo):
        cand = lo + jnp.left_shift(jnp.int32(1), 15 - it)
        return jnp.where(count(key_scr, lambda kc: kc >= cand, F32, F32_SUBLANES) >= topk, cand, lo)

    thr = lax.fori_loop(0, 16, lower_bit, jnp.left_shift(upper, 16))
    need = topk - count(key_scr, lambda kc: kc > thr, F32, F32_SUBLANES)

    def emit(c, before):
        kc = key_scr[c]
        kpos = c * ck + lax.broadcasted_iota(jnp.int32, (ck, tq), 0)
        equal = kc == thr
        rank = before + jnp.dot(tri_scr[...], jnp.where(equal, 1.0, 0.0).astype(BF16), preferred_element_type=F32)
        take = ((kc > thr) | (equal & (rank <= need))) & (kpos <= tpos)
        o_ref[0, c] = jnp.where(take, 0.0, -jnp.inf).T.astype(BF16)
        return rank[ck - 1:ck, :]

    lax.fori_loop(0, n_live, emit, jnp.zeros((1, tq), F32))

    def blank(c, carry):
        o_ref[0, c] = jnp.full((tq, ck), -jnp.inf, BF16)
        return carry

    lax.fori_loop(n_live, n_chunks, blank, 0)


def _dsa_select(y, aux, qi_off, ki_off, B, S, *, tq=256, ck=512):
    n_chunks = S // ck
    assert S % ck == 0 and S % tq == 0 and qi_off % 2 == 0
    nq = S // tq
    topk = min(DSA_TOPK, S // 4)
    wt = aux[:, :F32_SUBLANES].T
    return pl.pallas_call(
        functools.partial(_dsa_select_kernel, tq=tq, ck=ck, n_chunks=n_chunks, topk=topk),
        grid=(B, nq),
        in_specs=[pl.BlockSpec((2, tq, LANES), lambda b, i: (qi_off // 2, b * nq + i, 0)),
                  pl.BlockSpec((1, S, LANES), lambda b, i: (ki_off, b, 0)),
                  pl.BlockSpec((F32_SUBLANES, tq), lambda b, i: (0, b * nq + i))],
        out_specs=pl.BlockSpec((1, n_chunks, tq, ck), lambda b, i: (b, 0, i, 0)),
        out_shape=jax.ShapeDtypeStruct((B, n_chunks, S, ck), BF16),
        scratch_shapes=[pltpu.VMEM((n_chunks, ck, tq), jnp.int32), pltpu.VMEM((n_chunks, ck, tq), BF16),
                        pltpu.VMEM((ck, ck), BF16)],
        compiler_params=_cparams(("parallel", "parallel"), 48),
        name="dsa_select",
    )(y, y, wt)


def _out_proj_kernel(*refs, n_in):
    o_refs, w_ref, h_ref, out_ref = refs[:n_in], refs[n_in], refs[n_in + 1], refs[n_in + 2]
    slabs = [o[c] for o in o_refs for c in range(o.shape[0])]
    o = jnp.concatenate(slabs, axis=-1)
    out_ref[...] = h_ref[...] + jnp.dot(o, w_ref[...], preferred_element_type=F32)


def _out_proj(os_, w_bf16, h, *, tm=512):
    T, D = h.shape
    width = sum(o.shape[0] for o in os_) * LANES
    assert w_bf16.shape == (width, D)
    in_specs = [pl.BlockSpec((o.shape[0], tm, LANES), lambda i: (0, i, 0)) for o in os_]
    in_specs += [pl.BlockSpec((width, D), lambda i: (0, 0)), pl.BlockSpec((tm, D), lambda i: (i, 0))]
    return pl.pallas_call(
        functools.partial(_out_proj_kernel, n_in=len(os_)),
        grid=(T // tm,),
        in_specs=in_specs,
        out_specs=pl.BlockSpec((tm, D), lambda i: (i, 0)),
        out_shape=jax.ShapeDtypeStruct((T, D), F32),
        compiler_params=_cparams(("parallel",), 32),
        name="out_proj",
    )(*os_, w_bf16, h)


def _sort_bitonic_desc(xs, strides):
    xs = list(xs)
    for stride in strides:
        for i in range(len(xs)):
            if not i & stride:
                xs[i], xs[i + stride] = jnp.maximum(xs[i], xs[i + stride]), jnp.minimum(xs[i], xs[i + stride])
    return xs


def _top_values(s, k):
    n, t = s.shape
    assert k & (k - 1) == 0 and n == k * F32_SUBLANES
    xs = [s[i * F32_SUBLANES:(i + 1) * F32_SUBLANES, :] for i in range(k)]
    size = 2
    while size <= k:
        stride = size // 2
        while stride >= 1:
            for i in range(k):
                j = i ^ stride
                if j > i:
                    hi, lo = jnp.maximum(xs[i], xs[j]), jnp.minimum(xs[i], xs[j])
                    xs[i], xs[j] = (hi, lo) if (i & size) == 0 else (lo, hi)
            stride //= 2
        size *= 2
    dropped = None
    shift = F32_SUBLANES // 2
    while shift >= 1:
        other = [pltpu.roll(x, shift, 0) for x in xs]
        lows = [jnp.minimum(xs[i], other[k - 1 - i]) for i in range(k)]
        xs = _sort_bitonic_desc([jnp.maximum(xs[i], other[k - 1 - i]) for i in range(k)],
                                [st for st in (8, 4, 2, 1) if st < k])
        while len(lows) > 1:
            lows = [jnp.maximum(a, b) for a, b in zip(lows[0::2], lows[1::2])]
        dropped = lows[0] if dropped is None else jnp.maximum(jnp.maximum(dropped, pltpu.roll(dropped, shift, 0)), lows[0])
        shift //= 2
    return jnp.concatenate([x[0:1, :] for x in xs], axis=0), dropped[0:1, :]


def _pair_candidates(t1, t2):
    k = PEER_TOPK
    parts = [t1[0:1] + t2]
    parts += [t1[a:a + 1] + t2[0:k // 2] for a in range(1, k // 2)]
    parts.append(t1[k // 2:k] + t2[0:1])
    return jnp.concatenate(parts, axis=0)


def _kth_largest(c, k):
    left = jnp.full((1, c.shape[1]), float(k), F32)
    tau = jnp.zeros((1, c.shape[1]), F32)
    for _ in range(k):
        cur = jnp.max(c, axis=0, keepdims=True)
        hit = c >= cur
        tau = jnp.where(left > 0.0, cur, tau)
        left = left - jnp.sum(jnp.where(hit, 1.0, 0.0), axis=0, keepdims=True)
        c = jnp.where(hit, -jnp.inf, c)
    return tau


def _peer_route_kernel(h_ref, g_ref, wq_ref, sk_ref, xnt_ref, th1_ref, e1_ref, e2_ref):
    xn = _rms(h_ref[...], g_ref[...])
    xnt_ref[...] = xn.T.astype(BF16)
    xn = xn.astype(BF16)
    for hd in range(PEER_HEADS):
        q = jnp.dot(xn, wq_ref[:, hd * 2 * N_KEYS:(hd + 1) * 2 * N_KEYS], preferred_element_type=F32).astype(BF16)
        s1 = _nt_dot(sk_ref[2 * hd], q[:, :N_KEYS])
        s2 = _nt_dot(sk_ref[2 * hd + 1], q[:, N_KEYS:])
        s1 = s1 - jnp.max(s1, axis=0, keepdims=True)
        s2 = s2 - jnp.max(s2, axis=0, keepdims=True)
        (t1, next1), (t2, next2) = _top_values(s1, PEER_TOPK), _top_values(s2, PEER_TOPK)
        cand = _pair_candidates(t1, t2)
        chosen = cand >= _kth_largest(cand, PEER_TOPK)
        log_z = jnp.log(jnp.sum(jnp.where(chosen, jnp.exp(cand), 0.0), axis=0, keepdims=True))
        runner_up = jnp.maximum(jnp.max(jnp.where(chosen, -jnp.inf, cand), axis=0, keepdims=True),
                                jnp.maximum(next1 + t2[0:1], t1[0:1] + next2))
        smallest = jnp.min(jnp.where(chosen, cand, jnp.inf), axis=0, keepdims=True)
        tau = 0.5 * (smallest + runner_up)
        s1 = s1 - log_z
        th1_ref[hd] = jnp.exp((tau - log_z) - s1)
        e1_ref[hd] = jnp.exp(s1)
        e2_ref[hd] = jnp.exp(s2)


def _peer_route(h, g, wq_bf16, sk_bf16, *, tm=256):
    T, D = h.shape
    nq = PEER_HEADS * 2 * N_KEYS
    assert wq_bf16.shape == (D, nq) and sk_bf16.shape == (2 * PEER_HEADS, N_KEYS, N_KEYS)
    return pl.pallas_call(
        _peer_route_kernel,
        grid=(T // tm,),
        in_specs=[pl.BlockSpec((tm, D), lambda i: (i, 0)), pl.BlockSpec((1, D), lambda i: (0, 0)),
                  pl.BlockSpec((D, nq), lambda i: (0, 0)),
                  pl.BlockSpec((2 * PEER_HEADS, N_KEYS, N_KEYS), lambda i: (0, 0, 0))],
        out_specs=[pl.BlockSpec((D, tm), lambda i: (0, i))]
        + [pl.BlockSpec((PEER_HEADS, N_KEYS, tm), lambda i: (0, 0, i))] * 3,
        out_shape=[jax.ShapeDtypeStruct((D, T), BF16)]
        + [jax.ShapeDtypeStruct((PEER_HEADS, N_KEYS, T), F32)] * 3,
        compiler_params=_cparams(("parallel",), 40),
        name="peer_route",
    )(h, g.reshape(1, D), wq_bf16, sk_bf16)


def _gelu_tanh(x):
    return 0.5 * x * (1.0 + jnp.tanh(math.sqrt(2.0 / math.pi) * (x + 0.044715 * (x * x * x))))


def _peer_expert_kernel(xnt_ref, u_ref, vt_ref, th1_ref, e1_ref, e2_ref, h_ref, out_ref,
                        acc_scr, rows_scr, act0_scr, act1_scr, w0_scr, w1_scr, *, eb, cuts, tq):
    e = pl.program_id(1)
    rows, n_sub = eb // N_KEYS, len(cuts) - 1
    sublanes = rows_scr.shape[2]
    act_scr, w_scr = (act0_scr, act1_scr), (w0_scr, w1_scr)

    @pl.when(e == 0)
    def _():
        acc_scr[...] = jnp.zeros_like(acc_scr)

    row0 = pl.multiple_of(e * rows, rows)
    for k, tab in enumerate((th1_ref, e1_ref)):
        for hd in range(PEER_HEADS):
            block = tab[hd, pl.ds(row0, rows), :]
            for r in range(rows):
                rows_scr[k, hd * rows + r] = jnp.broadcast_to(block[r:r + 1, :], (sublanes, tq))

    def activations(sb):
        lo, hi = cuts[sb], cuts[sb + 1]
        a = jnp.dot(u_ref[lo:hi, :], xnt_ref[...], preferred_element_type=F32)
        act_scr[sb % 2][0:hi - lo, :] = _gelu_tanh(a.astype(BF16))

    def weights(sb):
        span = 64
        reps = (span // sublanes, 1)
        for r in range((cuts[sb + 1] - cuts[sb]) // N_KEYS):
            for t0 in range(0, tq, LANES):
                for i2 in range(0, N_KEYS, span):
                    w = jnp.zeros((span, LANES), BF16)
                    for hd in range(PEER_HEADS):
                        k = hd * rows + cuts[sb] // N_KEYS + r
                        th1 = jnp.tile(rows_scr[0, k, :, t0:t0 + LANES], reps)
                        e1 = jnp.tile(rows_scr[1, k, :, t0:t0 + LANES], reps).astype(BF16)
                        e2 = e2_ref[hd, i2:i2 + span, t0:t0 + LANES]
                        w = w + e1 * jnp.where(e2 >= th1, e2, 0.0).astype(BF16)
                    lo = r * N_KEYS + i2
                    w_scr[sb % 2][lo:lo + span, t0:t0 + LANES] = w

    def project(sb):
        lo, hi = cuts[sb], cuts[sb + 1]
        a = act_scr[sb % 2][0:hi - lo, :] * w_scr[sb % 2][0:hi - lo, :]
        acc_scr[...] += jnp.dot(vt_ref[:, lo:hi], a, preferred_element_type=F32)

    weights(0)
    activations(0)
    for sb in range(n_sub):
        if sb + 1 < n_sub:
            weights(sb + 1)
            activations(sb + 1)
        project(sb)

    @pl.when(e == pl.num_programs(1) - 1)
    def _():
        out_ref[...] = h_ref[...] + acc_scr[...].T


def _peer_experts(xn, u_bf16, vt_bf16, tables, h, *, tq=512, eb=2048, cuts=(0, 1024, 2048)):
    T, D = h.shape
    E = u_bf16.shape[0]
    assert E == N_KEYS * N_KEYS and E % eb == 0 and eb % (F32_SUBLANES * N_KEYS) == 0 and T % tq == 0
    assert cuts[0] == 0 and cuts[-1] == eb and all(c % N_KEYS == 0 for c in cuts)
    sub = max(b - a for a, b in zip(cuts[:-1], cuts[1:]))
    table_spec = pl.BlockSpec((PEER_HEADS, N_KEYS, tq), lambda i, e: (0, 0, i))
    return pl.pallas_call(
        functools.partial(_peer_expert_kernel, eb=eb, cuts=tuple(cuts), tq=tq),
        grid=(T // tq, E // eb),
        in_specs=[pl.BlockSpec((D, tq), lambda i, e: (0, i)),
                  pl.BlockSpec((eb, D), lambda i, e: (e, 0)),
                  pl.BlockSpec((D, eb), lambda i, e: (0, e)),
                  table_spec, table_spec, table_spec,
                  pl.BlockSpec((tq, D), lambda i, e: (i, 0))],
        out_specs=pl.BlockSpec((tq, D), lambda i, e: (i, 0)),
        out_shape=jax.ShapeDtypeStruct((T, D), F32),
        scratch_shapes=[pltpu.VMEM((D, tq), F32),
                        pltpu.VMEM((2, PEER_HEADS * (eb // N_KEYS), F32_SUBLANES, tq), F32),
                        pltpu.VMEM((sub, tq), BF16), pltpu.VMEM((sub, tq), BF16),
                        pltpu.VMEM((sub, tq), BF16), pltpu.VMEM((sub, tq), BF16)],
        compiler_params=_cparams(("parallel", "arbitrary"), 56),
        name="peer_experts",
    )(xn, u_bf16, vt_bf16, *tables, h)


def _ple_kernel(h_ref, g_ref, p_ref, wg_ref, wu_ref, out_ref):
    h = h_ref[...]
    gate = jax.nn.sigmoid(jnp.dot(_rms(h, g_ref[...]).astype(BF16), wg_ref[...], preferred_element_type=F32))
    up = jnp.dot(p_ref[...].astype(BF16), wu_ref[...], preferred_element_type=F32)
    out_ref[...] = h + up * gate


def _ple(h, g, p, wg_bf16, wu_bf16, *, tm=512):
    T, D = h.shape
    P = p.shape[1]
    return pl.pallas_call(
        _ple_kernel,
        grid=(T // tm,),
        in_specs=[pl.BlockSpec((tm, D), lambda i: (i, 0)), pl.BlockSpec((1, D), lambda i: (0, 0)),
                  pl.BlockSpec((tm, P), lambda i: (i, 0)), pl.BlockSpec((D, D), lambda i: (0, 0)),
                  pl.BlockSpec((P, D), lambda i: (0, 0))],
        out_specs=pl.BlockSpec((tm, D), lambda i: (i, 0)),
        out_shape=jax.ShapeDtypeStruct((T, D), F32),
        compiler_params=_cparams(("parallel",), 32),
        name="ple",
    )(h, g.reshape(1, D), p, wg_bf16, wu_bf16)


def _tile_gain(g, scale=1.0):
    return jnp.tile(g.astype(F32) * scale, LANES // HEAD_DIM)


def _ab_mixer(h, rope, B, S, g, w_in, w_out, a_q, a_k, b_q, b_k):
    A, Bw = A_HEADS * HEAD_DIM, B_HEADS * HEAD_DIM
    cuts = np.cumsum([A, A, A, Bw, Bw, Bw, IDX_HEADS * HEAD_DIM, HEAD_DIM, IDX_HEADS])
    base, ki0, wi0 = w_in[:, :cuts[6]], w_in[:, cuts[6]:cuts[7]], w_in[:, cuts[7]:cuts[8]]
    w_pad = jnp.concatenate([base, ki0, ki0, wi0, jnp.zeros((D_MODEL, LANES - IDX_HEADS), F32)], axis=1).astype(BF16)
    scale = HEAD_DIM ** -0.5 * LOG2E
    kinds = ["nr"] * 8 + ["p"] * 4 + ["nr"] * 8 + ["p"] * 4 + ["r"] * 3 + ["pa"]
    one = jnp.ones((LANES,), F32)
    gains = jnp.stack([_tile_gain(a_q, scale)] * 4 + [_tile_gain(a_k)] * 4 + [one] * 4
                      + [_tile_gain(b_q, scale)] * 4 + [_tile_gain(b_k)] * 4 + [one] * 8)
    y, aux = _project(h, g, w_pad, gains, rope, kinds)
    kmean = _block_means(y, 4, 4, B, S)
    tq = min(1024, S)
    oa = _flash("moba", y, y, y, 0, 4, 8, 4, B, S, tq=tq, ts=512, tk=tq, extra=kmean)
    bias = _dsa_select(y, aux, 24, 26, B, S)
    ob = _flash("dsa", y, y, y, 12, 16, 20, 4, B, S, tq=tq, ts=512, tk=tq, extra=bias)
    return _out_proj([oa, ob], w_out.astype(BF16), h)


def _c_mixer(h, rope, B, S, g, w_in, w_out, q_norm, k_norm):
    scale = HEAD_DIM ** -0.5 * LOG2E
    kinds = ["nr"] * 16 + ["p"] * 7 + ["pa"]
    one = jnp.ones((LANES,), F32)
    gains = jnp.stack([_tile_gain(q_norm, scale)] * 8 + [_tile_gain(k_norm)] * 8 + [one] * 8)
    y, _ = _project(h, g, w_in.astype(BF16), gains, rope, kinds)
    o = _flash("dilated", y, y, y, 0, 8, 16, 8, B, S, tq=512, tk=512)
    return _out_proj([o], w_out.astype(BF16), h)


def _peer(h, g, w_query, sub_keys, expert_u, expert_v):
    sk = sub_keys.reshape(2 * PEER_HEADS, N_KEYS, N_KEYS).astype(BF16)
    xn, *tables = _peer_route(h, g, w_query.astype(BF16), sk)
    return _peer_experts(xn, expert_u.astype(BF16), expert_v.T.astype(BF16), tables, h)


def kernel(x, p, positions, attn_norm, ffn_norm, ple_norm, ab_w_in, ab_w_out, a_q_norm, a_k_norm, b_q_norm,
           b_k_norm, c_w_in, c_w_out, c_q_norm, c_k_norm, peer_w_query, peer_sub_keys, peer_u, peer_v,
           ple_w_gate, ple_w_up):
    B, S, D = x.shape
    depth = p.shape[0]
    rope = _rope_tables(positions)
    h = x.reshape(B * S, D)
    for i in range(depth):
        j = i // 2
        if i % 2 == 0:
            h = _ab_mixer(h, rope, B, S, attn_norm[i], ab_w_in[j], ab_w_out[j],
                          a_q_norm[j], a_k_norm[j], b_q_norm[j], b_k_norm[j])
        else:
            h = _c_mixer(h, rope, B, S, attn_norm[i], c_w_in[j], c_w_out[j], c_q_norm[j], c_k_norm[j])
        h = _peer(h, ffn_norm[i], peer_w_query[i], peer_sub_keys[i], peer_u[i], peer_v[i])
        h = _ple(h, ple_norm[i], p[i].reshape(B * S, -1), ple_w_gate[i].astype(BF16), ple_w_up[i].astype(BF16))
    return h.reshape(B, S, D)
```

```python
import functools
import math

import numpy as np
import jax
import jax.numpy as jnp
from jax import lax
from jax.experimental import pallas as pl
from jax.experimental.pallas import tpu as pltpu

F32 = jnp.float32
BF16 = jnp.bfloat16

D_MODEL = 1024
HEAD_DIM = 64
ROT_HALF = 8
ROPE_THETA = 500000.0
NORM_EPS = 1e-6
NEG = -1e30
LOG2E = math.log2(math.e)
LANES = 128

A_HEADS = 8
B_HEADS = 8
C_HEADS = 16
IDX_HEADS = 4
MOBA_BLOCK = 256
MOBA_TOPK = 3
DSA_TOPK = 256
DILATED_PAIRS = ((128, 1), (512, 4), (2048, 16))
PEER_HEADS = 8
N_KEYS = 128
PEER_TOPK = 16

V7X_VMEM_BYTES = 64 * 1024 * 1024


def _cparams(sem, vmem_mb):
    assert vmem_mb * 2**20 < V7X_VMEM_BYTES
    return pltpu.CompilerParams(dimension_semantics=sem, vmem_limit_bytes=vmem_mb * 2**20)


def _rms(x, g):
    ms = jnp.mean(x * x, axis=-1, keepdims=True)
    return x * lax.rsqrt(ms + NORM_EPS) * g


def _nt_dot(a, b):
    return lax.dot_general(a, b, (((1,), (1,)), ((), ())), preferred_element_type=F32)


def _split_bf16(x):
    hi = x.astype(BF16)
    lo = (x - hi.astype(F32)).astype(BF16)
    return hi, lo


def _proj_kernel(x_ref, g_ref, w_ref, gains_ref, cos_ref, sa_ref, sb_ref, y_ref, aux_ref, *, kinds, group):
    xn = _rms(x_ref[...], g_ref[...]).astype(BF16)
    row = lax.broadcasted_iota(jnp.int32, (LANES, LANES), 0) // HEAD_DIM
    col = lax.broadcasted_iota(jnp.int32, (LANES, LANES), 1) // HEAD_DIM
    head_avg = jnp.where(row == col, 1.0 / HEAD_DIM, 0.0).astype(BF16)
    n_chunks = len(kinds)
    for g0 in range(0, n_chunks, group):
        g1 = min(g0 + group, n_chunks)
        yg = jnp.dot(xn, w_ref[:, g0 * LANES:g1 * LANES], preferred_element_type=F32)
        for c in range(g0, g1):
            y = yg[:, (c - g0) * LANES:(c - g0 + 1) * LANES]
            kind = kinds[c]
            if "n" in kind:
                hi, lo = _split_bf16(y * y)
                msq = (jnp.dot(hi, head_avg, preferred_element_type=F32)
                       + jnp.dot(lo, head_avg, preferred_element_type=F32))
                y = y * lax.rsqrt(msq + NORM_EPS) * gains_ref[c:c + 1, :]
            if "r" in kind:
                y = (y * cos_ref[...] + pltpu.roll(y, LANES - ROT_HALF, 1) * sa_ref[...]
                     + pltpu.roll(y, ROT_HALF, 1) * sb_ref[...])
            if "a" in kind:
                aux_ref[...] = y
            y_ref[c] = y.astype(BF16)


def _project(h, g, w_bf16, gains, rope, kinds, *, tm=512, group=4):
    T, D = h.shape
    n_chunks = len(kinds)
    assert w_bf16.shape == (D, n_chunks * LANES) and T % tm == 0
    cos, sa, sb = rope
    tok = lambda i: (i, 0)
    fixed = lambda i: (0, 0)
    return pl.pallas_call(
        functools.partial(_proj_kernel, kinds=tuple(kinds), group=group),
        grid=(T // tm,),
        in_specs=[pl.BlockSpec((tm, D), tok), pl.BlockSpec((1, D), fixed),
                  pl.BlockSpec((D, n_chunks * LANES), fixed), pl.BlockSpec((n_chunks, LANES), fixed),
                  pl.BlockSpec((tm, LANES), tok), pl.BlockSpec((tm, LANES), tok), pl.BlockSpec((tm, LANES), tok)],
        out_specs=[pl.BlockSpec((n_chunks, tm, LANES), lambda i: (0, i, 0)), pl.BlockSpec((tm, LANES), tok)],
        out_shape=[jax.ShapeDtypeStruct((n_chunks, T, LANES), BF16), jax.ShapeDtypeStruct((T, LANES), F32)],
        compiler_params=_cparams(("parallel",), 48),
        name="proj",
    )(h, g.reshape(1, D), w_bf16, gains, cos, sa, sb)


def _rope_tables(positions):
    B, S = positions.shape
    inv = ROPE_THETA ** (-jnp.arange(ROT_HALF, dtype=F32) / ROT_HALF)
    ang = positions.astype(F32)[..., None] * inv
    cos, sin = jnp.cos(ang), jnp.sin(ang)
    rest = HEAD_DIM - 2 * ROT_HALF
    one, zero, z8 = jnp.ones((B, S, rest), F32), jnp.zeros((B, S, rest), F32), jnp.zeros((B, S, ROT_HALF), F32)
    tile = lambda parts: jnp.tile(jnp.concatenate(parts, axis=-1), (1, 1, LANES // HEAD_DIM)).reshape(B * S, LANES)
    return tile([cos, cos, one]), tile([-sin, z8, zero]), tile([z8, sin, zero])


def _kmean_kernel(k_ref, o_ref):
    n = pl.program_id(1)

    @pl.when(n == 0)
    def _():
        o_ref[...] = jnp.zeros_like(o_ref)

    o_ref[:, pl.ds(n, 1), :] = jnp.mean(k_ref[...].astype(F32), axis=1, keepdims=True)


def _block_means(y, k_off, pairs, B, S):
    nb = S // MOBA_BLOCK
    assert nb <= LANES and k_off % pairs == 0
    return pl.pallas_call(
        _kmean_kernel,
        grid=(B, nb),
        in_specs=[pl.BlockSpec((pairs, MOBA_BLOCK, LANES), lambda b, n: (k_off // pairs, b * nb + n, 0))],
        out_specs=pl.BlockSpec((pairs, LANES, LANES), lambda b, n: (0, b, 0)),
        out_shape=jax.ShapeDtypeStruct((pairs, B * LANES, LANES), F32),
        compiler_params=_cparams(("parallel", "arbitrary"), 16),
        name="moba_kmean",
    )(y)


MASK_BIG = 32768.0


def _flash_kernel(*refs, mode, pairs, tq, ts, tk, n_steps, back):
    q_ref, k_ref, v_ref, x_ref, o_ref, m_scr, acc_scr, qa_scr = refs[:8]
    s_scr, p_scr, alpha_scr = refs[8:10], refs[10:12], refs[12:14]
    if mode == "dilated":
        b_ref = x_ref
    elif mode == "dsa":
        b_ref, bias_scr = x_ref, refs[14]
    else:
        km_ref, bias_scr, oh_scr = x_ref, refs[14], refs[15]
    i = pl.program_id(1)
    j = pl.program_id(2)
    lane = lax.broadcasted_iota(jnp.int32, (1, LANES), 1)
    halves = (lane < HEAD_DIM, lane >= HEAD_DIM)
    if mode == "dilated":
        kt = i * (tq // tk) - back + j
        active = kt >= 0
    else:
        kt = j
        active = j * tk < (i + 1) * tq
    tpos = i * tq + lax.broadcasted_iota(jnp.int32, (tq, 1), 0)

    n_heads = 2 * pairs
    n_sub = tq // ts
    n_units = n_sub * n_heads

    @pl.when(j == 0)
    def _():
        m_scr[...] = jnp.full_like(m_scr, NEG)
        acc_scr[...] = jnp.zeros_like(acc_scr)
        blk = lax.broadcasted_iota(jnp.int32, (ts, LANES), 1)
        for sub in range(n_sub):
            rows = slice(sub * ts, (sub + 1) * ts)
            qblk = tpos[rows] // MOBA_BLOCK
            for c in range(pairs):
                if mode == "moba":
                    km_hi, km_lo = _split_bf16(km_ref[c])
                for half in range(2):
                    qm = jnp.where(halves[half], q_ref[c, rows, :], jnp.zeros((), BF16))
                    if mode == "moba":
                        gate = jnp.where(blk < qblk, _nt_dot(qm, km_hi) + _nt_dot(qm, km_lo), NEG)
                        rest, thr = gate, None
                        for _ in range(MOBA_TOPK):
                            thr = jnp.max(rest, axis=1, keepdims=True)
                            rest = jnp.where(rest >= thr, NEG, rest)
                        attended = ((gate >= thr) & (blk < qblk)) | (blk == qblk)
                        feat = jnp.where(attended, 0.0, -MASK_BIG)
                        if half == 0:
                            feat = pltpu.roll(feat, HEAD_DIM, 1)
                        qm = jnp.where(halves[half], qm, feat.astype(BF16))
                    qa_scr[sub * n_heads + 2 * c + half] = qm

    def attend(bias):
        if mode == "moba":
            kblk = (kt * tk + lax.broadcasted_iota(jnp.int32, (tk, 1), 0)) // MOBA_BLOCK
            oh_scr[0] = jnp.where(lane == kblk + HEAD_DIM, 1.0, 0.0).astype(BF16)
            oh_scr[1] = jnp.where(lane == kblk, 1.0, 0.0).astype(BF16)

        def scores(u, par):
            kp = k_ref[(u % n_heads) // 2]
            if mode == "moba":
                kp = jnp.where(halves[par], kp, oh_scr[par])
            s_scr[par][...] = _nt_dot(qa_scr[u], kp)

        def bias_rows(u):
            start = u // n_heads * ts
            if isinstance(start, int):
                return bias[start:start + ts, :]
            return bias[pl.ds(pl.multiple_of(start, ts), ts), :]

        def softmax(u, par):
            s = s_scr[par][...]
            if bias is not None:
                s = s + (bias[...] if n_sub == 1 else bias_rows(u))
            m_old = m_scr[u]
            m_new = jnp.maximum(m_old, jnp.max(s, axis=1, keepdims=True))
            p_scr[par][...] = jnp.exp2(s - m_new).astype(BF16)
            alpha_scr[par][...] = jnp.exp2(m_old - m_new)
            m_scr[u] = m_new

        def values(u, par):
            va = jnp.where(halves[par], v_ref[(u % n_heads) // 2], jnp.ones((), BF16))
            acc_scr[u] = alpha_scr[par][...] * acc_scr[u] + jnp.dot(p_scr[par][...], va, preferred_element_type=F32)

        scores(0, 0)
        scores(1, 1)
        softmax(0, 0)

        def body(c, carry):
            scores(2 * c + 2, 0)
            softmax(2 * c + 1, 1)
            values(2 * c, 0)
            scores(2 * c + 3, 1)
            softmax(2 * c + 2, 0)
            values(2 * c + 1, 1)
            return carry

        lax.fori_loop(0, n_units // 2 - 1, body, 0)
        softmax(n_units - 1, 1)
        values(n_units - 2, 0)
        values(n_units - 1, 1)

    if mode == "moba":
        on_diagonal = (kt + 1) * tk > i * tq

        @pl.when(active & on_diagonal)
        def _():
            kpos = kt * tk + lax.broadcasted_iota(jnp.int32, (tq, tk), 1)
            bias_scr[...] = jnp.where(kpos <= tpos, 0.0, -jnp.inf)
            attend(bias_scr)

        @pl.when(active & jnp.logical_not(on_diagonal))
        def _():
            attend(None)
    elif mode == "dsa":
        @pl.when(active)
        def _():
            for part in range(b_ref.shape[1]):
                width = b_ref.shape[3]
                bias_scr[:, part * width:(part + 1) * width] = b_ref[0, part].astype(F32)
            attend(bias_scr)
    else:
        @pl.when(active)
        def _():
            attend(b_ref.at[0])

    @pl.when(j == n_steps - 1)
    def _():
        for sub in range(n_sub):
            for c in range(pairs):
                a0, a1 = acc_scr[sub * n_heads + 2 * c], acc_scr[sub * n_heads + 2 * c + 1]
                num = jnp.where(halves[0], a0, a1)
                den = pltpu.roll(jnp.where(halves[0], a1, a0), HEAD_DIM, 1)
                o_ref[c, sub * ts:(sub + 1) * ts, :] = (num / den).astype(BF16)


def _dilated_bias_table(tq, tk, back, n_steps):
    r = np.arange(tq)[:, None]
    c = np.arange(tk)[None, :]
    tabs = []
    for j in range(n_steps):
        d = (back - j) * tk + r - c
        mult = sum(((d >= 0) & (d <= w) & (d % dil == 0)).astype(np.float64) for w, dil in DILATED_PAIRS)
        tabs.append(np.where(mult > 0, np.log2(np.maximum(mult, 1.0)), -np.inf))
    return jnp.asarray(np.stack(tabs), F32)


def _flash(mode, q_arr, k_arr, v_arr, q_off, k_off, v_off, pairs, B, S, *, tq, tk, ts=None, extra=None):
    ts = ts or tq
    nq, nk = S // tq, S // tk
    n_units = tq // ts * 2 * pairs
    assert S % tq == 0 and S % tk == 0 and tq % tk == 0 and tq % ts == 0
    assert q_off % pairs == 0 and k_off % pairs == 0 and v_off % pairs == 0
    if mode == "dilated":
        max_back = max(w for w, _ in DILATED_PAIRS)
        back = -(-max_back // tk)
        n_steps = back + tq // tk
        kv_tile = lambda i, j: jnp.maximum(i * (tq // tk) - back + j, 0)
    else:
        back = 0
        n_steps = nk
        kv_tile = lambda i, j: jnp.minimum(j, ((i + 1) * tq - 1) // tk)
    in_specs = [pl.BlockSpec((pairs, tq, LANES), lambda b, i, j: (q_off // pairs, b * nq + i, 0)),
                pl.BlockSpec((pairs, tk, LANES), lambda b, i, j: (k_off // pairs, b * nk + kv_tile(i, j), 0)),
                pl.BlockSpec((pairs, tk, LANES), lambda b, i, j: (v_off // pairs, b * nk + kv_tile(i, j), 0))]
    args = [q_arr, k_arr, v_arr]
    scratch = [pltpu.VMEM((n_units, ts, 1), F32), pltpu.VMEM((n_units, ts, LANES), F32),
               pltpu.VMEM((n_units, ts, LANES), BF16), pltpu.VMEM((ts, tk), F32), pltpu.VMEM((ts, tk), F32),
               pltpu.VMEM((ts, tk), BF16), pltpu.VMEM((ts, tk), BF16),
               pltpu.VMEM((ts, 1), F32), pltpu.VMEM((ts, 1), F32)]
    if mode == "moba":
        assert tk % MOBA_BLOCK == 0 and S // MOBA_BLOCK <= HEAD_DIM
        in_specs.append(pl.BlockSpec((pairs, LANES, LANES), lambda b, i, j: (0, b, 0)))
        args.append(extra)
        scratch += [pltpu.VMEM((tq, tk), F32), pltpu.VMEM((2, tk, LANES), BF16)]
    elif mode == "dsa":
        ck = extra.shape[3]
        assert tk % ck == 0
        in_specs.append(pl.BlockSpec((1, tk // ck, tq, ck), lambda b, i, j: (b, kv_tile(i, j), i, 0)))
        args.append(extra)
        scratch.append(pltpu.VMEM((tq, tk), F32))
    else:
        in_specs.append(pl.BlockSpec((1, tq, tk), lambda b, i, j: (j, 0, 0)))
        args.append(_dilated_bias_table(tq, tk, back, n_steps))
    return pl.pallas_call(
        functools.partial(_flash_kernel, mode=mode, pairs=pairs, tq=tq, ts=ts, tk=tk, n_steps=n_steps, back=back),
        grid=(B, nq, n_steps),
        in_specs=in_specs,
        out_specs=pl.BlockSpec((pairs, tq, LANES), lambda b, i, j: (0, b * nq + i, 0)),
        out_shape=jax.ShapeDtypeStruct((pairs, B * S, LANES), BF16),
        scratch_shapes=scratch,
        compiler_params=_cparams(("parallel", "parallel", "arbitrary"), 48),
        name="flash_" + mode,
    )(*args)


F32_SUBLANES = 8
BF16_SUBLANES = 16


def _dsa_select_kernel(qi_ref, ki_ref, wt_ref, o_ref, key_scr, top_scr, tri_scr, *, tq, ck, n_chunks, topk):
    i = pl.program_id(1)
    tri_scr[...] = jnp.where(lax.broadcasted_iota(jnp.int32, (ck, ck), 0) >= lax.broadcasted_iota(jnp.int32, (ck, ck), 1),
                             1.0, 0.0).astype(BF16)
    lane = lax.broadcasted_iota(jnp.int32, (1, LANES), 1)
    halves = (lane < HEAD_DIM, lane >= HEAD_DIM)
    tpos = i * tq + lax.broadcasted_iota(jnp.int32, (1, tq), 1)
    n_live = ((i + 1) * tq + ck - 1) // ck
    wt = wt_ref[...]

    def score_chunk(c, carry):
        kc = ki_ref[0, pl.ds(pl.multiple_of(c * ck, ck), ck), :]
        score = jnp.zeros((ck, tq), F32)
        for h in range(IDX_HEADS):
            qm = jnp.where(halves[h % 2], qi_ref[h // 2], jnp.zeros((), BF16))
            score = score + wt[h:h + 1, :] * jnp.maximum(_nt_dot(kc, qm), 0.0)
        kpos = c * ck + lax.broadcasted_iota(jnp.int32, (ck, tq), 0)
        score = jnp.where(jnp.abs(score) >= jnp.finfo(F32).tiny, score, 0.0)
        score = jnp.where(kpos <= tpos, score, NEG)
        bits = pltpu.bitcast(score, jnp.int32)
        key_scr[c] = jnp.where(bits < 0, bits ^ jnp.int32(0x7FFFFFFF), bits)
        top_scr[c] = pltpu.bitcast(bits & jnp.int32(-65536), F32).astype(BF16)
        return carry

    lax.fori_loop(0, n_live, score_chunk, 0)

    def count(src, pred, dtype, group):
        assert ck // group <= 256

        def body(c, acc):
            hit = jnp.where(pred(src[c]), jnp.ones((), dtype), jnp.zeros((), dtype))
            parts = [hit[s * group:(s + 1) * group, :] for s in range(ck // group)]
            lanes = parts[:8]
            for s in range(8, len(parts)):
                lanes[s % 8] = lanes[s % 8] + parts[s]
            while len(lanes) > 1:
                lanes = [a + b for a, b in zip(lanes[0::2], lanes[1::2])]
            return acc + lanes[0].astype(F32)

        acc = lax.fori_loop(0, n_live, body, jnp.zeros((group, tq), F32))
        return jnp.sum(acc, axis=0, keepdims=True)

    def upper_bit(it, lo):
        cand = lo + jnp.left_shift(jnp.int32(1), 15 - it)
        pattern = jnp.where(cand < 0, cand ^ jnp.int32(0x7FFF), cand)
        pattern = jnp.where((pattern > 0) & (pattern < 0x80), jnp.int32(0x80), pattern)
        value = pltpu.bitcast(jnp.left_shift(pattern, 16), F32).astype(BF16)
        return jnp.where(count(top_scr, lambda sc: sc >= value, BF16, BF16_SUBLANES) >= topk, cand, lo)

    upper = lax.fori_loop(0, 16, upper_bit, jnp.full((1, tq), -2**15, jnp.int32))

    def lower_bit(it, lo):
        cand = lo + jnp.left_shift(jnp.int32(1), 15 - it)
        return jnp.where(count(key_scr, lambda kc: kc >= cand, F32, F32_SUBLANES) >= topk, cand, lo)

    thr = lax.fori_loop(0, 16, lower_bit, jnp.left_shift(upper, 16))
    need = topk - count(key_scr, lambda kc: kc > thr, F32, F32_SUBLANES)

    def emit(c, before):
        kc = key_scr[c]
        kpos = c * ck + lax.broadcasted_iota(jnp.int32, (ck, tq), 0)
        equal = kc == thr
        rank = before + jnp.dot(tri_scr[...], jnp.where(equal, 1.0, 0.0).astype(BF16), preferred_element_type=F32)
        take = ((kc > thr) | (equal & (rank <= need))) & (kpos <= tpos)
        o_ref[0, c] = jnp.where(take, 0.0, -jnp.inf).T.astype(BF16)
        return rank[ck - 1:ck, :]

    lax.fori_loop(0, n_live, emit, jnp.zeros((1, tq), F32))

    def blank(c, carry):
        o_ref[0, c] = jnp.full((tq, ck), -jnp.inf, BF16)
        return carry

    lax.fori_loop(n_live, n_chunks, blank, 0)


def _dsa_select(y, aux, qi_off, ki_off, B, S, *, tq=256, ck=512):
    n_chunks = S // ck
    assert S % ck == 0 and S % tq == 0 and qi_off % 2 == 0
    nq = S // tq
    topk = min(DSA_TOPK, S // 4)
    wt = aux[:, :F32_SUBLANES].T
    return pl.pallas_call(
        functools.partial(_dsa_select_kernel, tq=tq, ck=ck, n_chunks=n_chunks, topk=topk),
        grid=(B, nq),
        in_specs=[pl.BlockSpec((2, tq, LANES), lambda b, i: (qi_off // 2, b * nq + i, 0)),
                  pl.BlockSpec((1, S, LANES), lambda b, i: (ki_off, b, 0)),
                  pl.BlockSpec((F32_SUBLANES, tq), lambda b, i: (0, b * nq + i))],
        out_specs=pl.BlockSpec((1, n_chunks, tq, ck), lambda b, i: (b, 0, i, 0)),
        out_shape=jax.ShapeDtypeStruct((B, n_chunks, S, ck), BF16),
        scratch_shapes=[pltpu.VMEM((n_chunks, ck, tq), jnp.int32), pltpu.VMEM((n_chunks, ck, tq), BF16),
                        pltpu.VMEM((ck, ck), BF16)],
        compiler_params=_cparams(("parallel", "parallel"), 48),
        name="dsa_select",
    )(y, y, wt)


def _out_proj_kernel(*refs, n_in):
    o_refs, w_ref, h_ref, out_ref = refs[:n_in], refs[n_in], refs[n_in + 1], refs[n_in + 2]
    slabs = [o[c] for o in o_refs for c in range(o.shape[0])]
    o = jnp.concatenate(slabs, axis=-1)
    out_ref[...] = h_ref[...] + jnp.dot(o, w_ref[...], preferred_element_type=F32)


def _out_proj(os_, w_bf16, h, *, tm=512):
    T, D = h.shape
    width = sum(o.shape[0] for o in os_) * LANES
    assert w_bf16.shape == (width, D)
    in_specs = [pl.BlockSpec((o.shape[0], tm, LANES), lambda i: (0, i, 0)) for o in os_]
    in_specs += [pl.BlockSpec((width, D), lambda i: (0, 0)), pl.BlockSpec((tm, D), lambda i: (i, 0))]
    return pl.pallas_call(
        functools.partial(_out_proj_kernel, n_in=len(os_)),
        grid=(T // tm,),
        in_specs=in_specs,
        out_specs=pl.BlockSpec((tm, D), lambda i: (i, 0)),
        out_shape=jax.ShapeDtypeStruct((T, D), F32),
        compiler_params=_cparams(("parallel",), 32),
        name="out_proj",
    )(*os_, w_bf16, h)


def _sort_bitonic_desc(xs, strides):
    xs = list(xs)
    for stride in strides:
        for i in range(len(xs)):
            if not i & stride:
                xs[i], xs[i + stride] = jnp.maximum(xs[i], xs[i + stride]), jnp.minimum(xs[i], xs[i + stride])
    return xs


def _top_values(s, k):
    n, t = s.shape
    assert k & (k - 1) == 0 and n == k * F32_SUBLANES
    xs = [s[i * F32_SUBLANES:(i + 1) * F32_SUBLANES, :] for i in range(k)]
    size = 2
    while size <= k:
        stride = size // 2
        while stride >= 1:
            for i in range(k):
                j = i ^ stride
                if j > i:
                    hi, lo = jnp.maximum(xs[i], xs[j]), jnp.minimum(xs[i], xs[j])
                    xs[i], xs[j] = (hi, lo) if (i & size) == 0 else (lo, hi)
            stride //= 2
        size *= 2
    dropped = None
    shift = F32_SUBLANES // 2
    while shift >= 1:
        other = [pltpu.roll(x, shift, 0) for x in xs]
        lows = [jnp.minimum(xs[i], other[k - 1 - i]) for i in range(k)]
        xs = _sort_bitonic_desc([jnp.maximum(xs[i], other[k - 1 - i]) for i in range(k)],
                                [st for st in (8, 4, 2, 1) if st < k])
        while len(lows) > 1:
            lows = [jnp.maximum(a, b) for a, b in zip(lows[0::2], lows[1::2])]
        dropped = lows[0] if dropped is None else jnp.maximum(jnp.maximum(dropped, pltpu.roll(dropped, shift, 0)), lows[0])
        shift //= 2
    return jnp.concatenate([x[0:1, :] for x in xs], axis=0), dropped[0:1, :]


def _pair_candidates(t1, t2):
    k = PEER_TOPK
    parts = [t1[0:1] + t2]
    parts += [t1[a:a + 1] + t2[0:k // 2] for a in range(1, k // 2)]
    parts.append(t1[k // 2:k] + t2[0:1])
    return jnp.concatenate(parts, axis=0)


def _kth_largest(c, k):
    left = jnp.full((1, c.shape[1]), float(k), F32)
    tau = jnp.zeros((1, c.shape[1]), F32)
    for _ in range(k):
        cur = jnp.max(c, axis=0, keepdims=True)
        hit = c >= cur
        tau = jnp.where(left > 0.0, cur, tau)
        left = left - jnp.sum(jnp.where(hit, 1.0, 0.0), axis=0, keepdims=True)
        c = jnp.where(hit, -jnp.inf, c)
    return tau


def _peer_route_kernel(h_ref, g_ref, wq_ref, sk_ref, xnt_ref, th1_ref, e1_ref, e2_ref):
    xn = _rms(h_ref[...], g_ref[...])
    xnt_ref[...] = xn.T.astype(BF16)
    xn = xn.astype(BF16)
    for hd in range(PEER_HEADS):
        q = jnp.dot(xn, wq_ref[:, hd * 2 * N_KEYS:(hd + 1) * 2 * N_KEYS], preferred_element_type=F32).astype(BF16)
        s1 = _nt_dot(sk_ref[2 * hd], q[:, :N_KEYS])
        s2 = _nt_dot(sk_ref[2 * hd + 1], q[:, N_KEYS:])
        s1 = s1 - jnp.max(s1, axis=0, keepdims=True)
        s2 = s2 - jnp.max(s2, axis=0, keepdims=True)
        (t1, next1), (t2, next2) = _top_values(s1, PEER_TOPK), _top_values(s2, PEER_TOPK)
        cand = _pair_candidates(t1, t2)
        chosen = cand >= _kth_largest(cand, PEER_TOPK)
        log_z = jnp.log(jnp.sum(jnp.where(chosen, jnp.exp(cand), 0.0), axis=0, keepdims=True))
        runner_up = jnp.maximum(jnp.max(jnp.where(chosen, -jnp.inf, cand), axis=0, keepdims=True),
                                jnp.maximum(next1 + t2[0:1], t1[0:1] + next2))
        smallest = jnp.min(jnp.where(chosen, cand, jnp.inf), axis=0, keepdims=True)
        tau = 0.5 * (smallest + runner_up)
        s1 = s1 - log_z
        th1_ref[hd] = jnp.exp((tau - log_z) - s1)
        e1_ref[hd] = jnp.exp(s1)
        e2_ref[hd] = jnp.exp(s2)


def _peer_route(h, g, wq_bf16, sk_bf16, *, tm=256):
    T, D = h.shape
    nq = PEER_HEADS * 2 * N_KEYS
    assert wq_bf16.shape == (D, nq) and sk_bf16.shape == (2 * PEER_HEADS, N_KEYS, N_KEYS)
    return pl.pallas_call(
        _peer_route_kernel,
        grid=(T // tm,),
        in_specs=[pl.BlockSpec((tm, D), lambda i: (i, 0)), pl.BlockSpec((1, D), lambda i: (0, 0)),
                  pl.BlockSpec((D, nq), lambda i: (0, 0)),
                  pl.BlockSpec((2 * PEER_HEADS, N_KEYS, N_KEYS), lambda i: (0, 0, 0))],
        out_specs=[pl.BlockSpec((D, tm), lambda i: (0, i))]
        + [pl.BlockSpec((PEER_HEADS, N_KEYS, tm), lambda i: (0, 0, i))] * 3,
        out_shape=[jax.ShapeDtypeStruct((D, T), BF16)]
        + [jax.ShapeDtypeStruct((PEER_HEADS, N_KEYS, T), F32)] * 3,
        compiler_params=_cparams(("parallel",), 40),
        name="peer_route",
    )(h, g.reshape(1, D), wq_bf16, sk_bf16)


def _gelu_tanh(x):
    return 0.5 * x * (1.0 + jnp.tanh(math.sqrt(2.0 / math.pi) * (x + 0.044715 * (x * x * x))))


def _peer_expert_kernel(xnt_ref, u_ref, vt_ref, th1_ref, e1_ref, e2_ref, h_ref, out_ref,
                        acc_scr, rows_scr, act0_scr, act1_scr, w0_scr, w1_scr, *, eb, cuts, tq):
    e = pl.program_id(1)
    rows, n_sub = eb // N_KEYS, len(cuts) - 1
    sublanes = rows_scr.shape[2]
    act_scr, w_scr = (act0_scr, act1_scr), (w0_scr, w1_scr)

    @pl.when(e == 0)
    def _():
        acc_scr[...] = jnp.zeros_like(acc_scr)

    row0 = pl.multiple_of(e * rows, rows)
    for k, tab in enumerate((th1_ref, e1_ref)):
        for hd in range(PEER_HEADS):
            block = tab[hd, pl.ds(row0, rows), :]
            for r in range(rows):
                rows_scr[k, hd * rows + r] = jnp.broadcast_to(block[r:r + 1, :], (sublanes, tq))

    def activations(sb):
        lo, hi = cuts[sb], cuts[sb + 1]
        a = jnp.dot(u_ref[lo:hi, :], xnt_ref[...], preferred_element_type=F32)
        act_scr[sb % 2][0:hi - lo, :] = _gelu_tanh(a.astype(BF16))

    def weights(sb):
        span = 64
        reps = (span // sublanes, 1)
        for r in range((cuts[sb + 1] - cuts[sb]) // N_KEYS):
            for t0 in range(0, tq, LANES):
                for i2 in range(0, N_KEYS, span):
                    w = jnp.zeros((span, LANES), BF16)
                    for hd in range(PEER_HEADS):
                        k = hd * rows + cuts[sb] // N_KEYS + r
                        th1 = jnp.tile(rows_scr[0, k, :, t0:t0 + LANES], reps)
                        e1 = jnp.tile(rows_scr[1, k, :, t0:t0 + LANES], reps).astype(BF16)
                        e2 = e2_ref[hd, i2:i2 + span, t0:t0 + LANES]
                        w = w + e1 * jnp.where(e2 >= th1, e2, 0.0).astype(BF16)
                    lo = r * N_KEYS + i2
                    w_scr[sb % 2][lo:lo + span, t0:t0 + LANES] = w

    def project(sb):
        lo, hi = cuts[sb], cuts[sb + 1]
        a = act_scr[sb % 2][0:hi - lo, :] * w_scr[sb % 2][0:hi - lo, :]
        acc_scr[...] += jnp.dot(vt_ref[:, lo:hi], a, preferred_element_type=F32)

    weights(0)
    activations(0)
    for sb in range(n_sub):
        if sb + 1 < n_sub:
            weights(sb + 1)
            activations(sb + 1)
        project(sb)

    @pl.when(e == pl.num_programs(1) - 1)
    def _():
        out_ref[...] = h_ref[...] + acc_scr[...].T


def _peer_experts(xn, u_bf16, vt_bf16, tables, h, *, tq=512, eb=2048, cuts=(0, 1024, 2048)):
    T, D = h.shape
    E = u_bf16.shape[0]
    assert E == N_KEYS * N_KEYS and E % eb == 0 and eb % (F32_SUBLANES * N_KEYS) == 0 and T % tq == 0
    assert cuts[0] == 0 and cuts[-1] == eb and all(c % N_KEYS == 0 for c in cuts)
    sub = max(b - a for a, b in zip(cuts[:-1], cuts[1:]))
    table_spec = pl.BlockSpec((PEER_HEADS, N_KEYS, tq), lambda i, e: (0, 0, i))
    return pl.pallas_call(
        functools.partial(_peer_expert_kernel, eb=eb, cuts=tuple(cuts), tq=tq),
        grid=(T // tq, E // eb),
        in_specs=[pl.BlockSpec((D, tq), lambda i, e: (0, i)),
                  pl.BlockSpec((eb, D), lambda i, e: (e, 0)),
                  pl.BlockSpec((D, eb), lambda i, e: (0, e)),
                  table_spec, table_spec, table_spec,
                  pl.BlockSpec((tq, D), lambda i, e: (i, 0))],
        out_specs=pl.BlockSpec((tq, D), lambda i, e: (i, 0)),
        out_shape=jax.ShapeDtypeStruct((T, D), F32),
        scratch_shapes=[pltpu.VMEM((D, tq), F32),
                        pltpu.VMEM((2, PEER_HEADS * (eb // N_KEYS), F32_SUBLANES, tq), F32),
                        pltpu.VMEM((sub, tq), BF16), pltpu.VMEM((sub, tq), BF16),
                        pltpu.VMEM((sub, tq), BF16), pltpu.VMEM((sub, tq), BF16)],
        compiler_params=_cparams(("parallel", "arbitrary"), 56),
        name="peer_experts",
    )(xn, u_bf16, vt_bf16, *tables, h)


def _ple_kernel(h_ref, g_ref, p_ref, wg_ref, wu_ref, out_ref):
    h = h_ref[...]
    gate = jax.nn.sigmoid(jnp.dot(_rms(h, g_ref[...]).astype(BF16), wg_ref[...], preferred_element_type=F32))
    up = jnp.dot(p_ref[...].astype(BF16), wu_ref[...], preferred_element_type=F32)
    out_ref[...] = h + up * gate


def _ple(h, g, p, wg_bf16, wu_bf16, *, tm=512):
    T, D = h.shape
    P = p.shape[1]
    return pl.pallas_call(
        _ple_kernel,
        grid=(T // tm,),
        in_specs=[pl.BlockSpec((tm, D), lambda i: (i, 0)), pl.BlockSpec((1, D), lambda i: (0, 0)),
                  pl.BlockSpec((tm, P), lambda i: (i, 0)), pl.BlockSpec((D, D), lambda i: (0, 0)),
                  pl.BlockSpec((P, D), lambda i: (0, 0))],
        out_specs=pl.BlockSpec((tm, D), lambda i: (i, 0)),
        out_shape=jax.ShapeDtypeStruct((T, D), F32),
        compiler_params=_cparams(("parallel",), 32),
        name="ple",
    )(h, g.reshape(1, D), p, wg_bf16, wu_bf16)


def _tile_gain(g, scale=1.0):
    return jnp.tile(g.astype(F32) * scale, LANES // HEAD_DIM)


def _ab_mixer(h, rope, B, S, g, w_in, w_out, a_q, a_k, b_q, b_k):
    A, Bw = A_HEADS * HEAD_DIM, B_HEADS * HEAD_DIM
    cuts = np.cumsum([A, A, A, Bw, Bw, Bw, IDX_HEADS * HEAD_DIM, HEAD_DIM, IDX_HEADS])
    base, ki0, wi0 = w_in[:, :cuts[6]], w_in[:, cuts[6]:cuts[7]], w_in[:, cuts[7]:cuts[8]]
    w_pad = jnp.concatenate([base, ki0, ki0, wi0, jnp.zeros((D_MODEL, LANES - IDX_HEADS), F32)], axis=1).astype(BF16)
    scale = HEAD_DIM ** -0.5 * LOG2E
    kinds = ["nr"] * 8 + ["p"] * 4 + ["nr"] * 8 + ["p"] * 4 + ["r"] * 3 + ["pa"]
    one = jnp.ones((LANES,), F32)
    gains = jnp.stack([_tile_gain(a_q, scale)] * 4 + [_tile_gain(a_k)] * 4 + [one] * 4
                      + [_tile_gain(b_q, scale)] * 4 + [_tile_gain(b_k)] * 4 + [one] * 8)
    y, aux = _project(h, g, w_pad, gains, rope, kinds)
    kmean = _block_means(y, 4, 4, B, S)
    tq = min(1024, S)
    oa = _flash("moba", y, y, y, 0, 4, 8, 4, B, S, tq=tq, ts=512, tk=tq, extra=kmean)
    bias = _dsa_select(y, aux, 24, 26, B, S)
    ob = _flash("dsa", y, y, y, 12, 16, 20, 4, B, S, tq=tq, ts=512, tk=tq, extra=bias)
    return _out_proj([oa, ob], w_out.astype(BF16), h)


def _c_mixer(h, rope, B, S, g, w_in, w_out, q_norm, k_norm):
    scale = HEAD_DIM ** -0.5 * LOG2E
    kinds = ["nr"] * 16 + ["p"] * 7 + ["pa"]
    one = jnp.ones((LANES,), F32)
    gains = jnp.stack([_tile_gain(q_norm, scale)] * 8 + [_tile_gain(k_norm)] * 8 + [one] * 8)
    y, _ = _project(h, g, w_in.astype(BF16), gains, rope, kinds)
    o = _flash("dilated", y, y, y, 0, 8, 16, 8, B, S, tq=512, tk=512)
    return _out_proj([o], w_out.astype(BF16), h)


def _peer(h, g, w_query, sub_keys, expert_u, expert_v):
    sk = sub_keys.reshape(2 * PEER_HEADS, N_KEYS, N_KEYS).astype(BF16)
    xn, *tables = _peer_route(h, g, w_query.astype(BF16), sk)
    return _peer_experts(xn, expert_u.astype(BF16), expert_v.T.astype(BF16), tables, h)


def kernel(x, p, positions, attn_norm, ffn_norm, ple_norm, ab_w_in, ab_w_out, a_q_norm, a_k_norm, b_q_norm,
           b_k_norm, c_w_in, c_w_out, c_q_norm, c_k_norm, peer_w_query, peer_sub_keys, peer_u, peer_v,
           ple_w_gate, ple_w_up):
    B, S, D = x.shape
    depth = p.shape[0]
    rope = _rope_tables(positions)
    h = x.reshape(B * S, D)
    for i in range(depth):
        j = i // 2
        if i % 2 == 0:
            h = _ab_mixer(h, rope, B, S, attn_norm[i], ab_w_in[j], ab_w_out[j],
                          a_q_norm[j], a_k_norm[j], b_q_norm[j], b_k_norm[j])
        else:
            h = _c_mixer(h, rope, B, S, attn_norm[i], c_w_in[j], c_w_out[j], c_q_norm[j], c_k_norm[j])
        h = _peer(h, ffn_norm[i], peer_w_query[i], peer_sub_keys[i], peer_u[i], peer_v[i])
        h = _ple(h, ple_norm[i], p[i].reshape(B * S, -1), ple_w_gate[i].astype(BF16), ple_w_up[i].astype(BF16))
    return h.reshape(B, S, D)
```

```python
import functools
import math

import numpy as np
import jax
import jax.numpy as jnp
from jax import lax
from jax.experimental import pallas as pl
from jax.experimental.pallas import tpu as pltpu

F32 = jnp.float32
BF16 = jnp.bfloat16

D_MODEL = 1024
HEAD_DIM = 64
ROT_HALF = 8
ROPE_THETA = 500000.0
NORM_EPS = 1e-6
NEG = -1e30
LOG2E = math.log2(math.e)
LANES = 128

A_HEADS = 8
B_HEADS = 8
C_HEADS = 16
IDX_HEADS = 4
MOBA_BLOCK = 256
MOBA_TOPK = 3
DSA_TOPK = 256
DILATED_PAIRS = ((128, 1), (512, 4), (2048, 16))
PEER_HEADS = 8
N_KEYS = 128
PEER_TOPK = 16

V7X_VMEM_BYTES = 64 * 1024 * 1024


def _cparams(sem, vmem_mb):
    assert vmem_mb * 2**20 < V7X_VMEM_BYTES
    return pltpu.CompilerParams(dimension_semantics=sem, vmem_limit_bytes=vmem_mb * 2**20)


def _rms(x, g):
    ms = jnp.mean(x * x, axis=-1, keepdims=True)
    return x * lax.rsqrt(ms + NORM_EPS) * g


def _nt_dot(a, b):
    return lax.dot_general(a, b, (((1,), (1,)), ((), ())), preferred_element_type=F32)


def _split_bf16(x):
    hi = x.astype(BF16)
    lo = (x - hi.astype(F32)).astype(BF16)
    return hi, lo


def _proj_kernel(x_ref, g_ref, w_ref, gains_ref, cos_ref, sa_ref, sb_ref, y_ref, aux_ref, *, kinds, group):
    xn = _rms(x_ref[...], g_ref[...]).astype(BF16)
    row = lax.broadcasted_iota(jnp.int32, (LANES, LANES), 0) // HEAD_DIM
    col = lax.broadcasted_iota(jnp.int32, (LANES, LANES), 1) // HEAD_DIM
    head_avg = jnp.where(row == col, 1.0 / HEAD_DIM, 0.0).astype(BF16)
    n_chunks = len(kinds)
    for g0 in range(0, n_chunks, group):
        g1 = min(g0 + group, n_chunks)
        yg = jnp.dot(xn, w_ref[:, g0 * LANES:g1 * LANES], preferred_element_type=F32)
        for c in range(g0, g1):
            y = yg[:, (c - g0) * LANES:(c - g0 + 1) * LANES]
            kind = kinds[c]
            if "n" in kind:
                hi, lo = _split_bf16(y * y)
                msq = (jnp.dot(hi, head_avg, preferred_element_type=F32)
                       + jnp.dot(lo, head_avg, preferred_element_type=F32))
                y = y * lax.rsqrt(msq + NORM_EPS) * gains_ref[c:c + 1, :]
            if "r" in kind:
                y = (y * cos_ref[...] + pltpu.roll(y, LANES - ROT_HALF, 1) * sa_ref[...]
                     + pltpu.roll(y, ROT_HALF, 1) * sb_ref[...])
            if "a" in kind:
                aux_ref[...] = y
            y_ref[c] = y.astype(BF16)


def _project(h, g, w_bf16, gains, rope, kinds, *, tm=512, group=4):
    T, D = h.shape
    n_chunks = len(kinds)
    assert w_bf16.shape == (D, n_chunks * LANES) and T % tm == 0
    cos, sa, sb = rope
    tok = lambda i: (i, 0)
    fixed = lambda i: (0, 0)
    return pl.pallas_call(
        functools.partial(_proj_kernel, kinds=tuple(kinds), group=group),
        grid=(T // tm,),
        in_specs=[pl.BlockSpec((tm, D), tok), pl.BlockSpec((1, D), fixed),
                  pl.BlockSpec((D, n_chunks * LANES), fixed), pl.BlockSpec((n_chunks, LANES), fixed),
                  pl.BlockSpec((tm, LANES), tok), pl.BlockSpec((tm, LANES), tok), pl.BlockSpec((tm, LANES), tok)],
        out_specs=[pl.BlockSpec((n_chunks, tm, LANES), lambda i: (0, i, 0)), pl.BlockSpec((tm, LANES), tok)],
        out_shape=[jax.ShapeDtypeStruct((n_chunks, T, LANES), BF16), jax.ShapeDtypeStruct((T, LANES), F32)],
        compiler_params=_cparams(("parallel",), 48),
        name="proj",
    )(h, g.reshape(1, D), w_bf16, gains, cos, sa, sb)


def _rope_tables(positions):
    B, S = positions.shape
    inv = ROPE_THETA ** (-jnp.arange(ROT_HALF, dtype=F32) / ROT_HALF)
    ang = positions.astype(F32)[..., None] * inv
    cos, sin = jnp.cos(ang), jnp.sin(ang)
    rest = HEAD_DIM - 2 * ROT_HALF
    one, zero, z8 = jnp.ones((B, S, rest), F32), jnp.zeros((B, S, rest), F32), jnp.zeros((B, S, ROT_HALF), F32)
    tile = lambda parts: jnp.tile(jnp.concatenate(parts, axis=-1), (1, 1, LANES // HEAD_DIM)).reshape(B * S, LANES)
    return tile([cos, cos, one]), tile([-sin, z8, zero]), tile([z8, sin, zero])


def _kmean_kernel(k_ref, o_ref):
    n = pl.program_id(1)

    @pl.when(n == 0)
    def _():
        o_ref[...] = jnp.zeros_like(o_ref)

    o_ref[:, pl.ds(n, 1), :] = jnp.mean(k_ref[...].astype(F32), axis=1, keepdims=True)


def _block_means(y, k_off, pairs, B, S):
    nb = S // MOBA_BLOCK
    assert nb <= LANES and k_off % pairs == 0
    return pl.pallas_call(
        _kmean_kernel,
        grid=(B, nb),
        in_specs=[pl.BlockSpec((pairs, MOBA_BLOCK, LANES), lambda b, n: (k_off // pairs, b * nb + n, 0))],
        out_specs=pl.BlockSpec((pairs, LANES, LANES), lambda b, n: (0, b, 0)),
        out_shape=jax.ShapeDtypeStruct((pairs, B * LANES, LANES), F32),
        compiler_params=_cparams(("parallel", "arbitrary"), 16),
        name="moba_kmean",
    )(y)


MASK_BIG = 32768.0


def _flash_kernel(*refs, mode, pairs, tq, ts, tk, n_steps, back):
    q_ref, k_ref, v_ref, x_ref, o_ref, m_scr, acc_scr, qa_scr = refs[:8]
    s_scr, p_scr, alpha_scr = refs[8:10], refs[10:12], refs[12:14]
    if mode == "dilated":
        b_ref = x_ref
    elif mode == "dsa":
        b_ref, bias_scr = x_ref, refs[14]
    else:
        km_ref, bias_scr, oh_scr = x_ref, refs[14], refs[15]
    i = pl.program_id(1)
    j = pl.program_id(2)
    lane = lax.broadcasted_iota(jnp.int32, (1, LANES), 1)
    halves = (lane < HEAD_DIM, lane >= HEAD_DIM)
    if mode == "dilated":
        kt = i * (tq // tk) - back + j
        active = kt >= 0
    else:
        kt = j
        active = j * tk < (i + 1) * tq
    tpos = i * tq + lax.broadcasted_iota(jnp.int32, (tq, 1), 0)

    n_heads = 2 * pairs
    n_sub = tq // ts
    n_units = n_sub * n_heads

    @pl.when(j == 0)
    def _():
        m_scr[...] = jnp.full_like(m_scr, NEG)
        acc_scr[...] = jnp.zeros_like(acc_scr)
        blk = lax.broadcasted_iota(jnp.int32, (ts, LANES), 1)
        for sub in range(n_sub):
            rows = slice(sub * ts, (sub + 1) * ts)
            qblk = tpos[rows] // MOBA_BLOCK
            for c in range(pairs):
                if mode == "moba":
                    km_hi, km_lo = _split_bf16(km_ref[c])
                for half in range(2):
                    qm = jnp.where(halves[half], q_ref[c, rows, :], jnp.zeros((), BF16))
                    if mode == "moba":
                        gate = jnp.where(blk < qblk, _nt_dot(qm, km_hi) + _nt_dot(qm, km_lo), NEG)
                        rest, thr = gate, None
                        for _ in range(MOBA_TOPK):
                            thr = jnp.max(rest, axis=1, keepdims=True)
                            rest = jnp.where(rest >= thr, NEG, rest)
                        attended = ((gate >= thr) & (blk < qblk)) | (blk == qblk)
                        feat = jnp.where(attended, 0.0, -MASK_BIG)
                        if half == 0:
                            feat = pltpu.roll(feat, HEAD_DIM, 1)
                        qm = jnp.where(halves[half], qm, feat.astype(BF16))
                    qa_scr[sub * n_heads + 2 * c + half] = qm

    def attend(bias):
        if mode == "moba":
            kblk = (kt * tk + lax.broadcasted_iota(jnp.int32, (tk, 1), 0)) // MOBA_BLOCK
            oh_scr[0] = jnp.where(lane == kblk + HEAD_DIM, 1.0, 0.0).astype(BF16)
            oh_scr[1] = jnp.where(lane == kblk, 1.0, 0.0).astype(BF16)

        def scores(u, par):
            kp = k_ref[(u % n_heads) // 2]
            if mode == "moba":
                kp = jnp.where(halves[par], kp, oh_scr[par])
            s_scr[par][...] = _nt_dot(qa_scr[u], kp)

        def bias_rows(u):
            start = u // n_heads * ts
            if isinstance(start, int):
                return bias[start:start + ts, :]
            return bias[pl.ds(pl.multiple_of(start, ts), ts), :]

        def softmax(u, par):
            s = s_scr[par][...]
            if bias is not None:
                s = s + (bias[...] if n_sub == 1 else bias_rows(u))
            m_old = m_scr[u]
            m_new = jnp.maximum(m_old, jnp.max(s, axis=1, keepdims=True))
            p_scr[par][...] = jnp.exp2(s - m_new).astype(BF16)
            alpha_scr[par][...] = jnp.exp2(m_old - m_new)
            m_scr[u] = m_new

        def values(u, par):
            va = jnp.where(halves[par], v_ref[(u % n_heads) // 2], jnp.ones((), BF16))
            acc_scr[u] = alpha_scr[par][...] * acc_scr[u] + jnp.dot(p_scr[par][...], va, preferred_element_type=F32)

        scores(0, 0)
        scores(1, 1)
        softmax(0, 0)

        def body(c, carry):
            scores(2 * c + 2, 0)
            softmax(2 * c + 1, 1)
            values(2 * c, 0)
            scores(2 * c + 3, 1)
            softmax(2 * c + 2, 0)
            values(2 * c + 1, 1)
            return carry

        lax.fori_loop(0, n_units // 2 - 1, body, 0)
        softmax(n_units - 1, 1)
        values(n_units - 2, 0)
        values(n_units - 1, 1)

    if mode == "moba":
        on_diagonal = (kt + 1) * tk > i * tq

        @pl.when(active & on_diagonal)
        def _():
            kpos = kt * tk + lax.broadcasted_iota(jnp.int32, (tq, tk), 1)
            bias_scr[...] = jnp.where(kpos <= tpos, 0.0, -jnp.inf)
            attend(bias_scr)

        @pl.when(active & jnp.logical_not(on_diagonal))
        def _():
            attend(None)
    elif mode == "dsa":
        @pl.when(active)
        def _():
            for part in range(b_ref.shape[1]):
                width = b_ref.shape[3]
                bias_scr[:, part * width:(part + 1) * width] = b_ref[0, part].astype(F32)
            attend(bias_scr)
    else:
        @pl.when(active)
        def _():
            attend(b_ref.at[0])

    @pl.when(j == n_steps - 1)
    def _():
        for sub in range(n_sub):
            for c in range(pairs):
                a0, a1 = acc_scr[sub * n_heads + 2 * c], acc_scr[sub * n_heads + 2 * c + 1]
                num = jnp.where(halves[0], a0, a1)
                den = pltpu.roll(jnp.where(halves[0], a1, a0), HEAD_DIM, 1)
                o_ref[c, sub * ts:(sub + 1) * ts, :] = (num / den).astype(BF16)


def _dilated_bias_table(tq, tk, back, n_steps):
    r = np.arange(tq)[:, None]
    c = np.arange(tk)[None, :]
    tabs = []
    for j in range(n_steps):
        d = (back - j) * tk + r - c
        mult = sum(((d >= 0) & (d <= w) & (d % dil == 0)).astype(np.float64) for w, dil in DILATED_PAIRS)
        tabs.append(np.where(mult > 0, np.log2(np.maximum(mult, 1.0)), -np.inf))
    return jnp.asarray(np.stack(tabs), F32)


def _flash(mode, q_arr, k_arr, v_arr, q_off, k_off, v_off, pairs, B, S, *, tq, tk, ts=None, extra=None):
    ts = ts or tq
    nq, nk = S // tq, S // tk
    n_units = tq // ts * 2 * pairs
    assert S % tq == 0 and S % tk == 0 and tq % tk == 0 and tq % ts == 0
    assert q_off % pairs == 0 and k_off % pairs == 0 and v_off % pairs == 0
    if mode == "dilated":
        max_back = max(w for w, _ in DILATED_PAIRS)
        back = -(-max_back // tk)
        n_steps = back + tq // tk
        kv_tile = lambda i, j: jnp.maximum(i * (tq // tk) - back + j, 0)
    else:
        back = 0
        n_steps = nk
        kv_tile = lambda i, j: jnp.minimum(j, ((i + 1) * tq - 1) // tk)
    in_specs = [pl.BlockSpec((pairs, tq, LANES), lambda b, i, j: (q_off // pairs, b * nq + i, 0)),
                pl.BlockSpec((pairs, tk, LANES), lambda b, i, j: (k_off // pairs, b * nk + kv_tile(i, j), 0)),
                pl.BlockSpec((pairs, tk, LANES), lambda b, i, j: (v_off // pairs, b * nk + kv_tile(i, j), 0))]
    args = [q_arr, k_arr, v_arr]
    scratch = [pltpu.VMEM((n_units, ts, 1), F32), pltpu.VMEM((n_units, ts, LANES), F32),
               pltpu.VMEM((n_units, ts, LANES), BF16), pltpu.VMEM((ts, tk), F32), pltpu.VMEM((ts, tk), F32),
               pltpu.VMEM((ts, tk), BF16), pltpu.VMEM((ts, tk), BF16),
               pltpu.VMEM((ts, 1), F32), pltpu.VMEM((ts, 1), F32)]
    if mode == "moba":
        assert tk % MOBA_BLOCK == 0 and S // MOBA_BLOCK <= HEAD_DIM
        in_specs.append(pl.BlockSpec((pairs, LANES, LANES), lambda b, i, j: (0, b, 0)))
        args.append(extra)
        scratch += [pltpu.VMEM((tq, tk), F32), pltpu.VMEM((2, tk, LANES), BF16)]
    elif mode == "dsa":
        ck = extra.shape[3]
        assert tk % ck == 0
        in_specs.append(pl.BlockSpec((1, tk // ck, tq, ck), lambda b, i, j: (b, kv_tile(i, j), i, 0)))
        args.append(extra)
        scratch.append(pltpu.VMEM((tq, tk), F32))
    else:
        in_specs.append(pl.BlockSpec((1, tq, tk), lambda b, i, j: (j, 0, 0)))
        args.append(_dilated_bias_table(tq, tk, back, n_steps))
    return pl.pallas_call(
        functools.partial(_flash_kernel, mode=mode, pairs=pairs, tq=tq, ts=ts, tk=tk, n_steps=n_steps, back=back),
        grid=(B, nq, n_steps),
        in_specs=in_specs,
        out_specs=pl.BlockSpec((pairs, tq, LANES), lambda b, i, j: (0, b * nq + i, 0)),
        out_shape=jax.ShapeDtypeStruct((pairs, B * S, LANES), BF16),
        scratch_shapes=scratch,
        compiler_params=_cparams(("parallel", "parallel", "arbitrary"), 48),
        name="flash_" + mode,
    )(*args)


F32_SUBLANES = 8
BF16_SUBLANES = 16


def _dsa_select_kernel(qi_ref, ki_ref, wt_ref, o_ref, key_scr, top_scr, tri_scr, *, tq, ck, n_chunks, topk):
    i = pl.program_id(1)
    tri_scr[...] = jnp.where(lax.broadcasted_iota(jnp.int32, (ck, ck), 0) >= lax.broadcasted_iota(jnp.int32, (ck, ck), 1),
                             1.0, 0.0).astype(BF16)
    lane = lax.broadcasted_iota(jnp.int32, (1, LANES), 1)
    halves = (lane < HEAD_DIM, lane >= HEAD_DIM)
    tpos = i * tq + lax.broadcasted_iota(jnp.int32, (1, tq), 1)
    n_live = ((i + 1) * tq + ck - 1) // ck
    wt = wt_ref[...]

    def score_chunk(c, carry):
        kc = ki_ref[0, pl.ds(pl.multiple_of(c * ck, ck), ck), :]
        score = jnp.zeros((ck, tq), F32)
        for h in range(IDX_HEADS):
            qm = jnp.where(halves[h % 2], qi_ref[h // 2], jnp.zeros((), BF16))
            score = score + wt[h:h + 1, :] * jnp.maximum(_nt_dot(kc, qm), 0.0)
        kpos = c * ck + lax.broadcasted_iota(jnp.int32, (ck, tq), 0)
        score = jnp.where(jnp.abs(score) >= jnp.finfo(F32).tiny, score, 0.0)
        score = jnp.where(kpos <= tpos, score, NEG)
        bits = pltpu.bitcast(score, jnp.int32)
        key_scr[c] = jnp.where(bits < 0, bits ^ jnp.int32(0x7FFFFFFF), bits)
        top_scr[c] = pltpu.bitcast(bits & jnp.int32(-65536), F32).astype(BF16)
        return carry

    lax.fori_loop(0, n_live, score_chunk, 0)

    def count(src, pred, dtype, group):
        assert ck // group <= 256

        def body(c, acc):
            hit = jnp.where(pred(src[c]), jnp.ones((), dtype), jnp.zeros((), dtype))
            parts = [hit[s * group:(s + 1) * group, :] for s in range(ck // group)]
            lanes = parts[:8]
            for s in range(8, len(parts)):
                lanes[s % 8] = lanes[s % 8] + parts[s]
            while len(lanes) > 1:
                lanes = [a + b for a, b in zip(lanes[0::2], lanes[1::2])]
            return acc + lanes[0].astype(F32)

        acc = lax.fori_loop(0, n_live, body, jnp.zeros((group, tq), F32))
        return jnp.sum(acc, axis=0, keepdims=True)

    def upper_bit(it, lo):
        cand = lo + jnp.left_shift(jnp.int32(1), 15 - it)
        pattern = jnp.where(cand < 0, cand ^ jnp.int32(0x7FFF), cand)
        pattern = jnp.where((pattern > 0) & (pattern < 0x80), jnp.int32(0x80), pattern)
        value = pltpu.bitcast(jnp.left_shift(pattern, 16), F32).astype(BF16)
        return jnp.where(count(top_scr, lambda sc: sc >= value, BF16, BF16_SUBLANES) >= topk, cand, lo)

    upper = lax.fori_loop(0, 16, upper_bit, jnp.full((1, tq), -2**15, jnp.int32))

    def lower_bit(it, lo):
        cand = lo + jnp.left_shift(jnp.int32(1), 15 - it)
        return jnp.where(count(key_scr, lambda kc: kc >= cand, F32, F32_SUBLANES) >= topk, cand, lo)

    thr = lax.fori_loop(0, 16, lower_bit, jnp.left_shift(upper, 16))
    need = topk - count(key_scr, lambda kc: kc > thr, F32, F32_SUBLANES)

    def emit(c, before):
        kc = key_scr[c]
        kpos = c * ck + lax.broadcasted_iota(jnp.int32, (ck, tq), 0)
        equal = kc == thr
        rank = before + jnp.dot(tri_scr[...], jnp.where(equal, 1.0, 0.0).astype(BF16), preferred_element_type=F32)
        take = ((kc > thr) | (equal & (rank <= need))) & (kpos <= tpos)
        o_ref[0, c] = jnp.where(take, 0.0, -jnp.inf).T.astype(BF16)
        return rank[ck - 1:ck, :]

    lax.fori_loop(0, n_live, emit, jnp.zeros((1, tq), F32))

    def blank(c, carry):
        o_ref[0, c] = jnp.full((tq, ck), -jnp.inf, BF16)
        return carry

    lax.fori_loop(n_live, n_chunks, blank, 0)


def _dsa_select(y, aux, qi_off, ki_off, B, S, *, tq=256, ck=512):
    n_chunks = S // ck
    assert S % ck == 0 and S % tq == 0 and qi_off % 2 == 0
    nq = S // tq
    topk = min(DSA_TOPK, S // 4)
    wt = aux[:, :F32_SUBLANES].T
    return pl.pallas_call(
        functools.partial(_dsa_select_kernel, tq=tq, ck=ck, n_chunks=n_chunks, topk=topk),
        grid=(B, nq),
        in_specs=[pl.BlockSpec((2, tq, LANES), lambda b, i: (qi_off // 2, b * nq + i, 0)),
                  pl.BlockSpec((1, S, LANES), lambda b, i: (ki_off, b, 0)),
                  pl.BlockSpec((F32_SUBLANES, tq), lambda b, i: (0, b * nq + i))],
        out_specs=pl.BlockSpec((1, n_chunks, tq, ck), lambda b, i: (b, 0, i, 0)),
        out_shape=jax.ShapeDtypeStruct((B, n_chunks, S, ck), BF16),
        scratch_shapes=[pltpu.VMEM((n_chunks, ck, tq), jnp.int32), pltpu.VMEM((n_chunks, ck, tq), BF16),
                        pltpu.VMEM((ck, ck), BF16)],
        compiler_params=_cparams(("parallel", "parallel"), 48),
        name="dsa_select",
    )(y, y, wt)


def _out_proj_kernel(*refs, n_in):
    o_refs, w_ref, h_ref, out_ref = refs[:n_in], refs[n_in], refs[n_in + 1], refs[n_in + 2]
    slabs = [o[c] for o in o_refs for c in range(o.shape[0])]
    o = jnp.concatenate(slabs, axis=-1)
    out_ref[...] = h_ref[...] + jnp.dot(o, w_ref[...], preferred_element_type=F32)


def _out_proj(os_, w_bf16, h, *, tm=512):
    T, D = h.shape
    width = sum(o.shape[0] for o in os_) * LANES
    assert w_bf16.shape == (width, D)
    in_specs = [pl.BlockSpec((o.shape[0], tm, LANES), lambda i: (0, i, 0)) for o in os_]
    in_specs += [pl.BlockSpec((width, D), lambda i: (0, 0)), pl.BlockSpec((tm, D), lambda i: (i, 0))]
    return pl.pallas_call(
        functools.partial(_out_proj_kernel, n_in=len(os_)),
        grid=(T // tm,),
        in_specs=in_specs,
        out_specs=pl.BlockSpec((tm, D), lambda i: (i, 0)),
        out_shape=jax.ShapeDtypeStruct((T, D), F32),
        compiler_params=_cparams(("parallel",), 32),
        name="out_proj",
    )(*os_, w_bf16, h)


def _sort_bitonic_desc(xs, strides):
    xs = list(xs)
    for stride in strides:
        for i in range(len(xs)):
            if not i & stride:
                xs[i], xs[i + stride] = jnp.maximum(xs[i], xs[i + stride]), jnp.minimum(xs[i], xs[i + stride])
    return xs


def _top_values(s, k):
    n, t = s.shape
    assert k & (k - 1) == 0 and n == k * F32_SUBLANES
    xs = [s[i * F32_SUBLANES:(i + 1) * F32_SUBLANES, :] for i in range(k)]
    size = 2
    while size <= k:
        stride = size // 2
        while stride >= 1:
            for i in range(k):
                j = i ^ stride
                if j > i:
                    hi, lo = jnp.maximum(xs[i], xs[j]), jnp.minimum(xs[i], xs[j])
                    xs[i], xs[j] = (hi, lo) if (i & size) == 0 else (lo, hi)
            stride //= 2
        size *= 2
    dropped = None
    shift = F32_SUBLANES // 2
    while shift >= 1:
        other = [pltpu.roll(x, shift, 0) for x in xs]
        lows = [jnp.minimum(xs[i], other[k - 1 - i]) for i in range(k)]
        xs = _sort_bitonic_desc([jnp.maximum(xs[i], other[k - 1 - i]) for i in range(k)],
                                [st for st in (8, 4, 2, 1) if st < k])
        while len(lows) > 1:
            lows = [jnp.maximum(a, b) for a, b in zip(lows[0::2], lows[1::2])]
        dropped = lows[0] if dropped is None else jnp.maximum(jnp.maximum(dropped, pltpu.roll(dropped, shift, 0)), lows[0])
        shift //= 2
    return jnp.concatenate([x[0:1, :] for x in xs], axis=0), dropped[0:1, :]


def _pair_candidates(t1, t2):
    k = PEER_TOPK
    parts = [t1[0:1] + t2]
    parts += [t1[a:a + 1] + t2[0:k // 2] for a in range(1, k // 2)]
    parts.append(t1[k // 2:k] + t2[0:1])
    return jnp.concatenate(parts, axis=0)


def _kth_largest(c, k):
    left = jnp.full((1, c.shape[1]), float(k), F32)
    tau = jnp.zeros((1, c.shape[1]), F32)
    for _ in range(k):
        cur = jnp.max(c, axis=0, keepdims=True)
        hit = c >= cur
        tau = jnp.where(left > 0.0, cur, tau)
        left = left - jnp.sum(jnp.where(hit, 1.0, 0.0), axis=0, keepdims=True)
        c = jnp.where(hit, -jnp.inf, c)
    return tau


def _peer_route_kernel(h_ref, g_ref, wq_ref, sk_ref, xnt_ref, th1_ref, e1_ref, e2_ref):
    xn = _rms(h_ref[...], g_ref[...])
    xnt_ref[...] = xn.T.astype(BF16)
    xn = xn.astype(BF16)
    for hd in range(PEER_HEADS):
        q = jnp.dot(xn, wq_ref[:, hd * 2 * N_KEYS:(hd + 1) * 2 * N_KEYS], preferred_element_type=F32).astype(BF16)
        s1 = _nt_dot(sk_ref[2 * hd], q[:, :N_KEYS])
        s2 = _nt_dot(sk_ref[2 * hd + 1], q[:, N_KEYS:])
        s1 = s1 - jnp.max(s1, axis=0, keepdims=True)
        s2 = s2 - jnp.max(s2, axis=0, keepdims=True)
        (t1, next1), (t2, next2) = _top_values(s1, PEER_TOPK), _top_values(s2, PEER_TOPK)
        cand = _pair_candidates(t1, t2)
        chosen = cand >= _kth_largest(cand, PEER_TOPK)
        log_z = jnp.log(jnp.sum(jnp.where(chosen, jnp.exp(cand), 0.0), axis=0, keepdims=True))
        runner_up = jnp.maximum(jnp.max(jnp.where(chosen, -jnp.inf, cand), axis=0, keepdims=True),
                                jnp.maximum(next1 + t2[0:1], t1[0:1] + next2))
        smallest = jnp.min(jnp.where(chosen, cand, jnp.inf), axis=0, keepdims=True)
        tau = 0.5 * (smallest + runner_up)
        s1 = s1 - log_z
        th1_ref[hd] = jnp.exp((tau - log_z) - s1)
        e1_ref[hd] = jnp.exp(s1)
        e2_ref[hd] = jnp.exp(s2)


def _peer_route(h, g, wq_bf16, sk_bf16, *, tm=512):
    T, D = h.shape
    nq = PEER_HEADS * 2 * N_KEYS
    assert wq_bf16.shape == (D, nq) and sk_bf16.shape == (2 * PEER_HEADS, N_KEYS, N_KEYS)
    return pl.pallas_call(
        _peer_route_kernel,
        grid=(T // tm,),
        in_specs=[pl.BlockSpec((tm, D), lambda i: (i, 0)), pl.BlockSpec((1, D), lambda i: (0, 0)),
                  pl.BlockSpec((D, nq), lambda i: (0, 0)),
                  pl.BlockSpec((2 * PEER_HEADS, N_KEYS, N_KEYS), lambda i: (0, 0, 0))],
        out_specs=[pl.BlockSpec((D, tm), lambda i: (0, i))]
        + [pl.BlockSpec((PEER_HEADS, N_KEYS, tm), lambda i: (0, 0, i))] * 3,
        out_shape=[jax.ShapeDtypeStruct((D, T), BF16)]
        + [jax.ShapeDtypeStruct((PEER_HEADS, N_KEYS, T), F32)] * 3,
        compiler_params=_cparams(("parallel",), 40),
        name="peer_route",
    )(h, g.reshape(1, D), wq_bf16, sk_bf16)


def _gelu_tanh(x):
    return 0.5 * x * (1.0 + jnp.tanh(math.sqrt(2.0 / math.pi) * (x + 0.044715 * (x * x * x))))


def _peer_expert_kernel(xnt_ref, u_ref, vt_ref, th1_ref, e1_ref, e2_ref, h_ref, out_ref,
                        acc_scr, rows_scr, act0_scr, act1_scr, w0_scr, w1_scr, *, eb, cuts, tq):
    e = pl.program_id(1)
    rows, n_sub = eb // N_KEYS, len(cuts) - 1
    sublanes = rows_scr.shape[2]
    act_scr, w_scr = (act0_scr, act1_scr), (w0_scr, w1_scr)

    @pl.when(e == 0)
    def _():
        acc_scr[...] = jnp.zeros_like(acc_scr)

    row0 = pl.multiple_of(e * rows, rows)
    for k, tab in enumerate((th1_ref, e1_ref)):
        for hd in range(PEER_HEADS):
            block = tab[hd, pl.ds(row0, rows), :]
            for r in range(rows):
                rows_scr[k, hd * rows + r] = jnp.broadcast_to(block[r:r + 1, :], (sublanes, tq))

    def activations(sb):
        lo, hi = cuts[sb], cuts[sb + 1]
        a = jnp.dot(u_ref[lo:hi, :], xnt_ref[...], preferred_element_type=F32)
        act_scr[sb % 2][0:hi - lo, :] = _gelu_tanh(a.astype(BF16))

    def weights(sb):
        span = 64
        reps = (span // sublanes, 1)
        for r in range((cuts[sb + 1] - cuts[sb]) // N_KEYS):
            for t0 in range(0, tq, LANES):
                for i2 in range(0, N_KEYS, span):
                    w = jnp.zeros((span, LANES), BF16)
                    for hd in range(PEER_HEADS):
                        k = hd * rows + cuts[sb] // N_KEYS + r
                        th1 = jnp.tile(rows_scr[0, k, :, t0:t0 + LANES], reps)
                        e1 = jnp.tile(rows_scr[1, k, :, t0:t0 + LANES], reps).astype(BF16)
                        e2 = e2_ref[hd, i2:i2 + span, t0:t0 + LANES]
                        w = w + e1 * jnp.where(e2 >= th1, e2, 0.0).astype(BF16)
                    lo = r * N_KEYS + i2
                    w_scr[sb % 2][lo:lo + span, t0:t0 + LANES] = w

    def project(sb):
        lo, hi = cuts[sb], cuts[sb + 1]
        a = act_scr[sb % 2][0:hi - lo, :] * w_scr[sb % 2][0:hi - lo, :]
        acc_scr[...] += jnp.dot(vt_ref[:, lo:hi], a, preferred_element_type=F32)

    weights(0)
    activations(0)
    for sb in range(n_sub):
        if sb + 1 < n_sub:
            weights(sb + 1)
            activations(sb + 1)
        project(sb)

    @pl.when(e == pl.num_programs(1) - 1)
    def _():
        out_ref[...] = h_ref[...] + acc_scr[...].T


def _peer_experts(xn, u_bf16, vt_bf16, tables, h, *, tq=512, eb=2048, cuts=(0, 1024, 2048)):
    T, D = h.shape
    E = u_bf16.shape[0]
    assert E == N_KEYS * N_KEYS and E % eb == 0 and eb % (F32_SUBLANES * N_KEYS) == 0 and T % tq == 0
    assert cuts[0] == 0 and cuts[-1] == eb and all(c % N_KEYS == 0 for c in cuts)
    sub = max(b - a for a, b in zip(cuts[:-1], cuts[1:]))
    table_spec = pl.BlockSpec((PEER_HEADS, N_KEYS, tq), lambda i, e: (0, 0, i))
    return pl.pallas_call(
        functools.partial(_peer_expert_kernel, eb=eb, cuts=tuple(cuts), tq=tq),
        grid=(T // tq, E // eb),
        in_specs=[pl.BlockSpec((D, tq), lambda i, e: (0, i)),
                  pl.BlockSpec((eb, D), lambda i, e: (e, 0)),
                  pl.BlockSpec((D, eb), lambda i, e: (0, e)),
                  table_spec, table_spec, table_spec,
                  pl.BlockSpec((tq, D), lambda i, e: (i, 0))],
        out_specs=pl.BlockSpec((tq, D), lambda i, e: (i, 0)),
        out_shape=jax.ShapeDtypeStruct((T, D), F32),
        scratch_shapes=[pltpu.VMEM((D, tq), F32),
                        pltpu.VMEM((2, PEER_HEADS * (eb // N_KEYS), F32_SUBLANES, tq), F32),
                        pltpu.VMEM((sub, tq), BF16), pltpu.VMEM((sub, tq), BF16),
                        pltpu.VMEM((sub, tq), BF16), pltpu.VMEM((sub, tq), BF16)],
        compiler_params=_cparams(("parallel", "arbitrary"), 56),
        name="peer_experts",
    )(xn, u_bf16, vt_bf16, *tables, h)


def _ple_kernel(h_ref, g_ref, p_ref, wg_ref, wu_ref, out_ref):
    h = h_ref[...]
    gate = jax.nn.sigmoid(jnp.dot(_rms(h, g_ref[...]).astype(BF16), wg_ref[...], preferred_element_type=F32))
    up = jnp.dot(p_ref[...].astype(BF16), wu_ref[...], preferred_element_type=F32)
    out_ref[...] = h + up * gate


def _ple(h, g, p, wg_bf16, wu_bf16, *, tm=512):
    T, D = h.shape
    P = p.shape[1]
    return pl.pallas_call(
        _ple_kernel,
        grid=(T // tm,),
        in_specs=[pl.BlockSpec((tm, D), lambda i: (i, 0)), pl.BlockSpec((1, D), lambda i: (0, 0)),
                  pl.BlockSpec((tm, P), lambda i: (i, 0)), pl.BlockSpec((D, D), lambda i: (0, 0)),
                  pl.BlockSpec((P, D), lambda i: (0, 0))],
        out_specs=pl.BlockSpec((tm, D), lambda i: (i, 0)),
        out_shape=jax.ShapeDtypeStruct((T, D), F32),
        compiler_params=_cparams(("parallel",), 32),
        name="ple",
    )(h, g.reshape(1, D), p, wg_bf16, wu_bf16)


def _tile_gain(g, scale=1.0):
    return jnp.tile(g.astype(F32) * scale, LANES // HEAD_DIM)


def _ab_mixer(h, rope, B, S, g, w_in, w_out, a_q, a_k, b_q, b_k):
    A, Bw = A_HEADS * HEAD_DIM, B_HEADS * HEAD_DIM
    cuts = np.cumsum([A, A, A, Bw, Bw, Bw, IDX_HEADS * HEAD_DIM, HEAD_DIM, IDX_HEADS])
    base, ki0, wi0 = w_in[:, :cuts[6]], w_in[:, cuts[6]:cuts[7]], w_in[:, cuts[7]:cuts[8]]
    w_pad = jnp.concatenate([base, ki0, ki0, wi0, jnp.zeros((D_MODEL, LANES - IDX_HEADS), F32)], axis=1).astype(BF16)
    scale = HEAD_DIM ** -0.5 * LOG2E
    kinds = ["nr"] * 8 + ["p"] * 4 + ["nr"] * 8 + ["p"] * 4 + ["r"] * 3 + ["pa"]
    one = jnp.ones((LANES,), F32)
    gains = jnp.stack([_tile_gain(a_q, scale)] * 4 + [_tile_gain(a_k)] * 4 + [one] * 4
                      + [_tile_gain(b_q, scale)] * 4 + [_tile_gain(b_k)] * 4 + [one] * 8)
    y, aux = _project(h, g, w_pad, gains, rope, kinds)
    kmean = _block_means(y, 4, 4, B, S)
    tq = min(1024, S)
    oa = _flash("moba", y, y, y, 0, 4, 8, 4, B, S, tq=tq, ts=512, tk=tq, extra=kmean)
    bias = _dsa_select(y, aux, 24, 26, B, S)
    ob = _flash("dsa", y, y, y, 12, 16, 20, 4, B, S, tq=tq, ts=512, tk=tq, extra=bias)
    return _out_proj([oa, ob], w_out.astype(BF16), h)


def _c_mixer(h, rope, B, S, g, w_in, w_out, q_norm, k_norm):
    scale = HEAD_DIM ** -0.5 * LOG2E
    kinds = ["nr"] * 16 + ["p"] * 7 + ["pa"]
    one = jnp.ones((LANES,), F32)
    gains = jnp.stack([_tile_gain(q_norm, scale)] * 8 + [_tile_gain(k_norm)] * 8 + [one] * 8)
    y, _ = _project(h, g, w_in.astype(BF16), gains, rope, kinds)
    o = _flash("dilated", y, y, y, 0, 8, 16, 8, B, S, tq=512, tk=512)
    return _out_proj([o], w_out.astype(BF16), h)


def _peer(h, g, w_query, sub_keys, expert_u, expert_v):
    sk = sub_keys.reshape(2 * PEER_HEADS, N_KEYS, N_KEYS).astype(BF16)
    xn, *tables = _peer_route(h, g, w_query.astype(BF16), sk)
    return _peer_experts(xn, expert_u.astype(BF16), expert_v.T.astype(BF16), tables, h)


def kernel(x, p, positions, attn_norm, ffn_norm, ple_norm, ab_w_in, ab_w_out, a_q_norm, a_k_norm, b_q_norm,
           b_k_norm, c_w_in, c_w_out, c_q_norm, c_k_norm, peer_w_query, peer_sub_keys, peer_u, peer_v,
           ple_w_gate, ple_w_up):
    B, S, D = x.shape
    depth = p.shape[0]
    rope = _rope_tables(positions)
    h = x.reshape(B * S, D)
    for i in range(depth):
        j = i // 2
        if i % 2 == 0:
            h = _ab_mixer(h, rope, B, S, attn_norm[i], ab_w_in[j], ab_w_out[j],
                          a_q_norm[j], a_k_norm[j], b_q_norm[j], b_k_norm[j])
        else:
            h = _c_mixer(h, rope, B, S, attn_norm[i], c_w_in[j], c_w_out[j], c_q_norm[j], c_k_norm[j])
        h = _peer(h, ffn_norm[i], peer_w_query[i], peer_sub_keys[i], peer_u[i], peer_v[i])
        h = _ple(h, ple_norm[i], p[i].reshape(B * S, -1), ple_w_gate[i].astype(BF16), ple_w_up[i].astype(BF16))
    return h.reshape(B, S, D)
```

```python
import functools
import math

import numpy as np
import jax
import jax.numpy as jnp
from jax import lax
from jax.experimental import pallas as pl
from jax.experimental.pallas import tpu as pltpu

F32 = jnp.float32
BF16 = jnp.bfloat16

D_MODEL = 1024
HEAD_DIM = 64
ROT_HALF = 8
ROPE_THETA = 500000.0
NORM_EPS = 1e-6
NEG = -1e30
LOG2E = math.log2(math.e)
LANES = 128

A_HEADS = 8
B_HEADS = 8
C_HEADS = 16
IDX_HEADS = 4
MOBA_BLOCK = 256
MOBA_TOPK = 3
DSA_TOPK = 256
DILATED_PAIRS = ((128, 1), (512, 4), (2048, 16))
PEER_HEADS = 8
N_KEYS = 128
PEER_TOPK = 16

V7X_VMEM_BYTES = 64 * 1024 * 1024


def _cparams(sem, vmem_mb):
    assert vmem_mb * 2**20 < V7X_VMEM_BYTES
    return pltpu.CompilerParams(dimension_semantics=sem, vmem_limit_bytes=vmem_mb * 2**20)


def _rms(x, g):
    ms = jnp.mean(x * x, axis=-1, keepdims=True)
    return x * lax.rsqrt(ms + NORM_EPS) * g


def _nt_dot(a, b):
    return lax.dot_general(a, b, (((1,), (1,)), ((), ())), preferred_element_type=F32)


def _split_bf16(x):
    hi = x.astype(BF16)
    lo = (x - hi.astype(F32)).astype(BF16)
    return hi, lo


def _proj_kernel(x_ref, g_ref, w_ref, gains_ref, cos_ref, sa_ref, sb_ref, y_ref, aux_ref, *, kinds, group):
    xn = _rms(x_ref[...], g_ref[...]).astype(BF16)
    row = lax.broadcasted_iota(jnp.int32, (LANES, LANES), 0) // HEAD_DIM
    col = lax.broadcasted_iota(jnp.int32, (LANES, LANES), 1) // HEAD_DIM
    head_avg = jnp.where(row == col, 1.0 / HEAD_DIM, 0.0).astype(BF16)
    n_chunks = len(kinds)
    for g0 in range(0, n_chunks, group):
        g1 = min(g0 + group, n_chunks)
        yg = jnp.dot(xn, w_ref[:, g0 * LANES:g1 * LANES], preferred_element_type=F32)
        for c in range(g0, g1):
            y = yg[:, (c - g0) * LANES:(c - g0 + 1) * LANES]
            kind = kinds[c]
            if "n" in kind:
                hi, lo = _split_bf16(y * y)
                msq = (jnp.dot(hi, head_avg, preferred_element_type=F32)
                       + jnp.dot(lo, head_avg, preferred_element_type=F32))
                y = y * lax.rsqrt(msq + NORM_EPS) * gains_ref[c:c + 1, :]
            if "r" in kind:
                y = (y * cos_ref[...] + pltpu.roll(y, LANES - ROT_HALF, 1) * sa_ref[...]
                     + pltpu.roll(y, ROT_HALF, 1) * sb_ref[...])
            if "a" in kind:
                aux_ref[...] = y
            y_ref[c] = y.astype(BF16)


def _project(h, g, w_bf16, gains, rope, kinds, *, tm=512, group=4):
    T, D = h.shape
    n_chunks = len(kinds)
    assert w_bf16.shape == (D, n_chunks * LANES) and T % tm == 0
    cos, sa, sb = rope
    tok = lambda i: (i, 0)
    fixed = lambda i: (0, 0)
    return pl.pallas_call(
        functools.partial(_proj_kernel, kinds=tuple(kinds), group=group),
        grid=(T // tm,),
        in_specs=[pl.BlockSpec((tm, D), tok), pl.BlockSpec((1, D), fixed),
                  pl.BlockSpec((D, n_chunks * LANES), fixed), pl.BlockSpec((n_chunks, LANES), fixed),
                  pl.BlockSpec((tm, LANES), tok), pl.BlockSpec((tm, LANES), tok), pl.BlockSpec((tm, LANES), tok)],
        out_specs=[pl.BlockSpec((n_chunks, tm, LANES), lambda i: (0, i, 0)), pl.BlockSpec((tm, LANES), tok)],
        out_shape=[jax.ShapeDtypeStruct((n_chunks, T, LANES), BF16), jax.ShapeDtypeStruct((T, LANES), F32)],
        compiler_params=_cparams(("parallel",), 48),
        name="proj",
    )(h, g.reshape(1, D), w_bf16, gains, cos, sa, sb)


def _rope_tables(positions):
    B, S = positions.shape
    inv = ROPE_THETA ** (-jnp.arange(ROT_HALF, dtype=F32) / ROT_HALF)
    ang = positions.astype(F32)[..., None] * inv
    cos, sin = jnp.cos(ang), jnp.sin(ang)
    rest = HEAD_DIM - 2 * ROT_HALF
    one, zero, z8 = jnp.ones((B, S, rest), F32), jnp.zeros((B, S, rest), F32), jnp.zeros((B, S, ROT_HALF), F32)
    tile = lambda parts: jnp.tile(jnp.concatenate(parts, axis=-1), (1, 1, LANES // HEAD_DIM)).reshape(B * S, LANES)
    return tile([cos, cos, one]), tile([-sin, z8, zero]), tile([z8, sin, zero])


def _kmean_kernel(k_ref, o_ref):
    n = pl.program_id(1)

    @pl.when(n == 0)
    def _():
        o_ref[...] = jnp.zeros_like(o_ref)

    o_ref[:, pl.ds(n, 1), :] = jnp.mean(k_ref[...].astype(F32), axis=1, keepdims=True)


def _block_means(y, k_off, pairs, B, S):
    nb = S // MOBA_BLOCK
    assert nb <= LANES and k_off % pairs == 0
    return pl.pallas_call(
        _kmean_kernel,
        grid=(B, nb),
        in_specs=[pl.BlockSpec((pairs, MOBA_BLOCK, LANES), lambda b, n: (k_off // pairs, b * nb + n, 0))],
        out_specs=pl.BlockSpec((pairs, LANES, LANES), lambda b, n: (0, b, 0)),
        out_shape=jax.ShapeDtypeStruct((pairs, B * LANES, LANES), F32),
        compiler_params=_cparams(("parallel", "arbitrary"), 16),
        name="moba_kmean",
    )(y)


MASK_BIG = 32768.0


def _flash_kernel(*refs, mode, pairs, tq, ts, tk, n_steps, back):
    q_ref, k_ref, v_ref, x_ref, o_ref, m_scr, acc_scr, qa_scr = refs[:8]
    s_scr, p_scr, alpha_scr = refs[8:10], refs[10:12], refs[12:14]
    if mode == "dilated":
        b_ref = x_ref
    elif mode == "dsa":
        b_ref, bias_scr = x_ref, refs[14]
    else:
        km_ref, bias_scr, oh_scr = x_ref, refs[14], refs[15]
    i = pl.program_id(1)
    j = pl.program_id(2)
    lane = lax.broadcasted_iota(jnp.int32, (1, LANES), 1)
    halves = (lane < HEAD_DIM, lane >= HEAD_DIM)
    if mode == "dilated":
        kt = i * (tq // tk) - back + j
        active = kt >= 0
    else:
        kt = j
        active = j * tk < (i + 1) * tq
    tpos = i * tq + lax.broadcasted_iota(jnp.int32, (tq, 1), 0)

    n_heads = 2 * pairs
    n_sub = tq // ts
    n_units = n_sub * n_heads

    @pl.when(j == 0)
    def _():
        m_scr[...] = jnp.full_like(m_scr, NEG)
        acc_scr[...] = jnp.zeros_like(acc_scr)
        blk = lax.broadcasted_iota(jnp.int32, (ts, LANES), 1)
        for sub in range(n_sub):
            rows = slice(sub * ts, (sub + 1) * ts)
            qblk = tpos[rows] // MOBA_BLOCK
            for c in range(pairs):
                if mode == "moba":
                    km_hi, km_lo = _split_bf16(km_ref[c])
                for half in range(2):
                    qm = jnp.where(halves[half], q_ref[c, rows, :], jnp.zeros((), BF16))
                    if mode == "moba":
                        gate = jnp.where(blk < qblk, _nt_dot(qm, km_hi) + _nt_dot(qm, km_lo), NEG)
                        rest, thr = gate, None
                        for _ in range(MOBA_TOPK):
                            thr = jnp.max(rest, axis=1, keepdims=True)
                            rest = jnp.where(rest >= thr, NEG, rest)
                        attended = ((gate >= thr) & (blk < qblk)) | (blk == qblk)
                        feat = jnp.where(attended, 0.0, -MASK_BIG)
                        if half == 0:
                            feat = pltpu.roll(feat, HEAD_DIM, 1)
                        qm = jnp.where(halves[half], qm, feat.astype(BF16))
                    qa_scr[sub * n_heads + 2 * c + half] = qm

    def attend(bias):
        if mode == "moba":
            kblk = (kt * tk + lax.broadcasted_iota(jnp.int32, (tk, 1), 0)) // MOBA_BLOCK
            oh_scr[0] = jnp.where(lane == kblk + HEAD_DIM, 1.0, 0.0).astype(BF16)
            oh_scr[1] = jnp.where(lane == kblk, 1.0, 0.0).astype(BF16)

        def scores(u, par):
            kp = k_ref[(u % n_heads) // 2]
            if mode == "moba":
                kp = jnp.where(halves[par], kp, oh_scr[par])
            s_scr[par][...] = _nt_dot(qa_scr[u], kp)

        def bias_rows(u):
            start = u // n_heads * ts
            if isinstance(start, int):
                return bias[start:start + ts, :]
            return bias[pl.ds(pl.multiple_of(start, ts), ts), :]

        def softmax(u, par):
            s = s_scr[par][...]
            if bias is not None:
                s = s + (bias[...] if n_sub == 1 else bias_rows(u))
            m_old = m_scr[u]
            m_new = jnp.maximum(m_old, jnp.max(s, axis=1, keepdims=True))
            p_scr[par][...] = jnp.exp2(s - m_new).astype(BF16)
            alpha_scr[par][...] = jnp.exp2(m_old - m_new)
            m_scr[u] = m_new

        def values(u, par):
            va = jnp.where(halves[par], v_ref[(u % n_heads) // 2], jnp.ones((), BF16))
            acc_scr[u] = alpha_scr[par][...] * acc_scr[u] + jnp.dot(p_scr[par][...], va, preferred_element_type=F32)

        scores(0, 0)
        scores(1, 1)
        softmax(0, 0)

        def body(c, carry):
            scores(2 * c + 2, 0)
            softmax(2 * c + 1, 1)
            values(2 * c, 0)
            scores(2 * c + 3, 1)
            softmax(2 * c + 2, 0)
            values(2 * c + 1, 1)
            return carry

        lax.fori_loop(0, n_units // 2 - 1, body, 0)
        softmax(n_units - 1, 1)
        values(n_units - 2, 0)
        values(n_units - 1, 1)

    if mode == "moba":
        on_diagonal = (kt + 1) * tk > i * tq

        @pl.when(active & on_diagonal)
        def _():
            kpos = kt * tk + lax.broadcasted_iota(jnp.int32, (tq, tk), 1)
            bias_scr[...] = jnp.where(kpos <= tpos, 0.0, -jnp.inf)
            attend(bias_scr)

        @pl.when(active & jnp.logical_not(on_diagonal))
        def _():
            attend(None)
    elif mode == "dsa":
        @pl.when(active)
        def _():
            for part in range(b_ref.shape[1]):
                width = b_ref.shape[3]
                bias_scr[:, part * width:(part + 1) * width] = b_ref[0, part].astype(F32)
            attend(bias_scr)
    else:
        @pl.when(active)
        def _():
            attend(b_ref.at[0])

    @pl.when(j == n_steps - 1)
    def _():
        for sub in range(n_sub):
            for c in range(pairs):
                a0, a1 = acc_scr[sub * n_heads + 2 * c], acc_scr[sub * n_heads + 2 * c + 1]
                num = jnp.where(halves[0], a0, a1)
                den = pltpu.roll(jnp.where(halves[0], a1, a0), HEAD_DIM, 1)
                o_ref[c, sub * ts:(sub + 1) * ts, :] = (num / den).astype(BF16)


def _dilated_bias_table(tq, tk, back, n_steps):
    r = np.arange(tq)[:, None]
    c = np.arange(tk)[None, :]
    tabs = []
    for j in range(n_steps):
        d = (back - j) * tk + r - c
        mult = sum(((d >= 0) & (d <= w) & (d % dil == 0)).astype(np.float64) for w, dil in DILATED_PAIRS)
        tabs.append(np.where(mult > 0, np.log2(np.maximum(mult, 1.0)), -np.inf))
    return jnp.asarray(np.stack(tabs), F32)


def _flash(mode, q_arr, k_arr, v_arr, q_off, k_off, v_off, pairs, B, S, *, tq, tk, ts=None, extra=None):
    ts = ts or tq
    nq, nk = S // tq, S // tk
    n_units = tq // ts * 2 * pairs
    assert S % tq == 0 and S % tk == 0 and tq % tk == 0 and tq % ts == 0
    assert q_off % pairs == 0 and k_off % pairs == 0 and v_off % pairs == 0
    if mode == "dilated":
        max_back = max(w for w, _ in DILATED_PAIRS)
        back = -(-max_back // tk)
        n_steps = back + tq // tk
        kv_tile = lambda i, j: jnp.maximum(i * (tq // tk) - back + j, 0)
    else:
        back = 0
        n_steps = nk
        kv_tile = lambda i, j: jnp.minimum(j, ((i + 1) * tq - 1) // tk)
    in_specs = [pl.BlockSpec((pairs, tq, LANES), lambda b, i, j: (q_off // pairs, b * nq + i, 0)),
                pl.BlockSpec((pairs, tk, LANES), lambda b, i, j: (k_off // pairs, b * nk + kv_tile(i, j), 0)),
                pl.BlockSpec((pairs, tk, LANES), lambda b, i, j: (v_off // pairs, b * nk + kv_tile(i, j), 0))]
    args = [q_arr, k_arr, v_arr]
    scratch = [pltpu.VMEM((n_units, ts, 1), F32), pltpu.VMEM((n_units, ts, LANES), F32),
               pltpu.VMEM((n_units, ts, LANES), BF16), pltpu.VMEM((ts, tk), F32), pltpu.VMEM((ts, tk), F32),
               pltpu.VMEM((ts, tk), BF16), pltpu.VMEM((ts, tk), BF16),
               pltpu.VMEM((ts, 1), F32), pltpu.VMEM((ts, 1), F32)]
    if mode == "moba":
        assert tk % MOBA_BLOCK == 0 and S // MOBA_BLOCK <= HEAD_DIM
        in_specs.append(pl.BlockSpec((pairs, LANES, LANES), lambda b, i, j: (0, b, 0)))
        args.append(extra)
        scratch += [pltpu.VMEM((tq, tk), F32), pltpu.VMEM((2, tk, LANES), BF16)]
    elif mode == "dsa":
        ck = extra.shape[3]
        assert tk % ck == 0
        in_specs.append(pl.BlockSpec((1, tk // ck, tq, ck), lambda b, i, j: (b, kv_tile(i, j), i, 0)))
        args.append(extra)
        scratch.append(pltpu.VMEM((tq, tk), F32))
    else:
        in_specs.append(pl.BlockSpec((1, tq, tk), lambda b, i, j: (j, 0, 0)))
        args.append(_dilated_bias_table(tq, tk, back, n_steps))
    return pl.pallas_call(
        functools.partial(_flash_kernel, mode=mode, pairs=pairs, tq=tq, ts=ts, tk=tk, n_steps=n_steps, back=back),
        grid=(B, nq, n_steps),
        in_specs=in_specs,
        out_specs=pl.BlockSpec((pairs, tq, LANES), lambda b, i, j: (0, b * nq + i, 0)),
        out_shape=jax.ShapeDtypeStruct((pairs, B * S, LANES), BF16),
        scratch_shapes=scratch,
        compiler_params=_cparams(("parallel", "parallel", "arbitrary"), 48),
        name="flash_" + mode,
    )(*args)


F32_SUBLANES = 8
BF16_SUBLANES = 16


def _dsa_select_kernel(qi_ref, ki_ref, wt_ref, o_ref, key_scr, top_scr, tri_scr, *, tq, ck, n_chunks, topk):
    i = pl.program_id(1)
    tri_scr[...] = jnp.where(lax.broadcasted_iota(jnp.int32, (ck, ck), 0) >= lax.broadcasted_iota(jnp.int32, (ck, ck), 1),
                             1.0, 0.0).astype(BF16)
    lane = lax.broadcasted_iota(jnp.int32, (1, LANES), 1)
    halves = (lane < HEAD_DIM, lane >= HEAD_DIM)
    tpos = i * tq + lax.broadcasted_iota(jnp.int32, (1, tq), 1)
    n_live = ((i + 1) * tq + ck - 1) // ck
    wt = wt_ref[...]

    def score_chunk(c, carry):
        kc = ki_ref[0, pl.ds(pl.multiple_of(c * ck, ck), ck), :]
        score = jnp.zeros((ck, tq), F32)
        for h in range(IDX_HEADS):
            qm = jnp.where(halves[h % 2], qi_ref[h // 2], jnp.zeros((), BF16))
            score = score + wt[h:h + 1, :] * jnp.maximum(_nt_dot(kc, qm), 0.0)
        kpos = c * ck + lax.broadcasted_iota(jnp.int32, (ck, tq), 0)
        score = jnp.where(jnp.abs(score) >= jnp.finfo(F32).tiny, score, 0.0)
        score = jnp.where(kpos <= tpos, score, NEG)
        bits = pltpu.bitcast(score, jnp.int32)
        key_scr[c] = jnp.where(bits < 0, bits ^ jnp.int32(0x7FFFFFFF), bits)
        top_scr[c] = pltpu.bitcast(bits & jnp.int32(-65536), F32).astype(BF16)
        return carry

    lax.fori_loop(0, n_live, score_chunk, 0)

    def count(src, pred, dtype, group):
        assert ck // group <= 256

        def body(c, acc):
            hit = jnp.where(pred(src[c]), jnp.ones((), dtype), jnp.zeros((), dtype))
            parts = [hit[s * group:(s + 1) * group, :] for s in range(ck // group)]
            lanes = parts[:8]
            for s in range(8, len(parts)):
                lanes[s % 8] = lanes[s % 8] + parts[s]
            while len(lanes) > 1:
                lanes = [a + b for a, b in zip(lanes[0::2], lanes[1::2])]
            return acc + lanes[0].astype(F32)

        acc = lax.fori_loop(0, n_live, body, jnp.zeros((group, tq), F32))
        return jnp.sum(acc, axis=0, keepdims=True)

    def upper_bit(it, lo):
        cand = lo + jnp.left_shift(jnp.int32(1), 15 - it)
        pattern = jnp.where(cand < 0, cand ^ jnp.int32(0x7FFF), cand)
        pattern = jnp.where((pattern > 0) & (pattern < 0x80), jnp.int32(0x80), pattern)
        value = pltpu.bitcast(jnp.left_shift(pattern, 16), F32).astype(BF16)
        return jnp.where(count(top_scr, lambda sc: sc >= value, BF16, BF16_SUBLANES) >= topk, cand, lo)

    upper = lax.fori_loop(0, 16, upper_bit, jnp.full((1, tq), -2**15, jnp.int32))

    def lower_bit(it, lo):
        cand = lo + jnp.left_shift(jnp.int32(1), 15 - it)
        return jnp.where(count(key_scr, lambda kc: kc >= cand, F32, F32_SUBLANES) >= topk, cand, lo)

    thr = lax.fori_loop(0, 16, lower_bit, jnp.left_shift(upper, 16))
    need = topk - count(key_scr, lambda kc: kc > thr, F32, F32_SUBLANES)

    def emit(c, before):
        kc = key_scr[c]
        kpos = c * ck + lax.broadcasted_iota(jnp.int32, (ck, tq), 0)
        equal = kc == thr
        rank = before + jnp.dot(tri_scr[...], jnp.where(equal, 1.0, 0.0).astype(BF16), preferred_element_type=F32)
        take = ((kc > thr) | (equal & (rank <= need))) & (kpos <= tpos)
        o_ref[0, c] = jnp.where(take, 0.0, -jnp.inf).T.astype(BF16)
        return rank[ck - 1:ck, :]

    lax.fori_loop(0, n_live, emit, jnp.zeros((1, tq), F32))

    def blank(c, carry):
        o_ref[0, c] = jnp.full((tq, ck), -jnp.inf, BF16)
        return carry

    lax.fori_loop(n_live, n_chunks, blank, 0)


def _dsa_select(y, aux, qi_off, ki_off, B, S, *, tq=256, ck=512):
    n_chunks = S // ck
    assert S % ck == 0 and S % tq == 0 and qi_off % 2 == 0
    nq = S // tq
    topk = min(DSA_TOPK, S // 4)
    wt = aux[:, :F32_SUBLANES].T
    return pl.pallas_call(
        functools.partial(_dsa_select_kernel, tq=tq, ck=ck, n_chunks=n_chunks, topk=topk),
        grid=(B, nq),
        in_specs=[pl.BlockSpec((2, tq, LANES), lambda b, i: (qi_off // 2, b * nq + i, 0)),
                  pl.BlockSpec((1, S, LANES), lambda b, i: (ki_off, b, 0)),
                  pl.BlockSpec((F32_SUBLANES, tq), lambda b, i: (0, b * nq + i))],
        out_specs=pl.BlockSpec((1, n_chunks, tq, ck), lambda b, i: (b, 0, i, 0)),
        out_shape=jax.ShapeDtypeStruct((B, n_chunks, S, ck), BF16),
        scratch_shapes=[pltpu.VMEM((n_chunks, ck, tq), jnp.int32), pltpu.VMEM((n_chunks, ck, tq), BF16),
                        pltpu.VMEM((ck, ck), BF16)],
        compiler_params=_cparams(("parallel", "parallel"), 48),
        name="dsa_select",
    )(y, y, wt)


def _out_proj_kernel(*refs, n_in):
    o_refs, w_ref, h_ref, out_ref = refs[:n_in], refs[n_in], refs[n_in + 1], refs[n_in + 2]
    slabs = [o[c] for o in o_refs for c in range(o.shape[0])]
    o = jnp.concatenate(slabs, axis=-1)
    out_ref[...] = h_ref[...] + jnp.dot(o, w_ref[...], preferred_element_type=F32)


def _out_proj(os_, w_bf16, h, *, tm=512):
    T, D = h.shape
    width = sum(o.shape[0] for o in os_) * LANES
    assert w_bf16.shape == (width, D)
    in_specs = [pl.BlockSpec((o.shape[0], tm, LANES), lambda i: (0, i, 0)) for o in os_]
    in_specs += [pl.BlockSpec((width, D), lambda i: (0, 0)), pl.BlockSpec((tm, D), lambda i: (i, 0))]
    return pl.pallas_call(
        functools.partial(_out_proj_kernel, n_in=len(os_)),
        grid=(T // tm,),
        in_specs=in_specs,
        out_specs=pl.BlockSpec((tm, D), lambda i: (i, 0)),
        out_shape=jax.ShapeDtypeStruct((T, D), F32),
        compiler_params=_cparams(("parallel",), 32),
        name="out_proj",
    )(*os_, w_bf16, h)


def _sort_bitonic_desc(xs, strides):
    xs = list(xs)
    for stride in strides:
        for i in range(len(xs)):
            if not i & stride:
                xs[i], xs[i + stride] = jnp.maximum(xs[i], xs[i + stride]), jnp.minimum(xs[i], xs[i + stride])
    return xs


def _top_values(s, k):
    n, t = s.shape
    assert k & (k - 1) == 0 and n == k * F32_SUBLANES
    xs = [s[i * F32_SUBLANES:(i + 1) * F32_SUBLANES, :] for i in range(k)]
    size = 2
    while size <= k:
        stride = size // 2
        while stride >= 1:
            for i in range(k):
                j = i ^ stride
                if j > i:
                    hi, lo = jnp.maximum(xs[i], xs[j]), jnp.minimum(xs[i], xs[j])
                    xs[i], xs[j] = (hi, lo) if (i & size) == 0 else (lo, hi)
            stride //= 2
        size *= 2
    dropped = None
    shift = F32_SUBLANES // 2
    while shift >= 1:
        other = [pltpu.roll(x, shift, 0) for x in xs]
        lows = [jnp.minimum(xs[i], other[k - 1 - i]) for i in range(k)]
        xs = _sort_bitonic_desc([jnp.maximum(xs[i], other[k - 1 - i]) for i in range(k)],
                                [st for st in (8, 4, 2, 1) if st < k])
        while len(lows) > 1:
            lows = [jnp.maximum(a, b) for a, b in zip(lows[0::2], lows[1::2])]
        dropped = lows[0] if dropped is None else jnp.maximum(jnp.maximum(dropped, pltpu.roll(dropped, shift, 0)), lows[0])
        shift //= 2
    return jnp.concatenate([x[0:1, :] for x in xs], axis=0), dropped[0:1, :]


def _pair_candidates(t1, t2):
    k = PEER_TOPK
    parts = [t1[0:1] + t2]
    parts += [t1[a:a + 1] + t2[0:k // 2] for a in range(1, k // 2)]
    parts.append(t1[k // 2:k] + t2[0:1])
    return jnp.concatenate(parts, axis=0)


def _kth_largest(c, k):
    left = jnp.full((1, c.shape[1]), float(k), F32)
    tau = jnp.zeros((1, c.shape[1]), F32)
    for _ in range(k):
        cur = jnp.max(c, axis=0, keepdims=True)
        hit = c >= cur
        tau = jnp.where(left > 0.0, cur, tau)
        left = left - jnp.sum(jnp.where(hit, 1.0, 0.0), axis=0, keepdims=True)
        c = jnp.where(hit, -jnp.inf, c)
    return tau


def _peer_route_kernel(h_ref, g_ref, wq_ref, sk_ref, xnt_ref, th1_ref, e1_ref, e2_ref):
    xn = _rms(h_ref[...], g_ref[...])
    xnt_ref[...] = xn.T.astype(BF16)
    xn = xn.astype(BF16)
    for hd in range(PEER_HEADS):
        q = jnp.dot(xn, wq_ref[:, hd * 2 * N_KEYS:(hd + 1) * 2 * N_KEYS], preferred_element_type=F32).astype(BF16)
        s1 = _nt_dot(sk_ref[2 * hd], q[:, :N_KEYS])
        s2 = _nt_dot(sk_ref[2 * hd + 1], q[:, N_KEYS:])
        s1 = s1 - jnp.max(s1, axis=0, keepdims=True)
        s2 = s2 - jnp.max(s2, axis=0, keepdims=True)
        (t1, next1), (t2, next2) = _top_values(s1, PEER_TOPK), _top_values(s2, PEER_TOPK)
        cand = _pair_candidates(t1, t2)
        chosen = cand >= _kth_largest(cand, PEER_TOPK)
        log_z = jnp.log(jnp.sum(jnp.where(chosen, jnp.exp(cand), 0.0), axis=0, keepdims=True))
        runner_up = jnp.maximum(jnp.max(jnp.where(chosen, -jnp.inf, cand), axis=0, keepdims=True),
                                jnp.maximum(next1 + t2[0:1], t1[0:1] + next2))
        smallest = jnp.min(jnp.where(chosen, cand, jnp.inf), axis=0, keepdims=True)
        tau = 0.5 * (smallest + runner_up)
        s1 = s1 - log_z
        th1_ref[hd] = jnp.exp((tau - log_z) - s1)
        e1_ref[hd] = jnp.exp(s1)
        e2_ref[hd] = jnp.exp(s2)


def _peer_route(h, g, wq_bf16, sk_bf16, *, tm=256):
    T, D = h.shape
    nq = PEER_HEADS * 2 * N_KEYS
    assert wq_bf16.shape == (D, nq) and sk_bf16.shape == (2 * PEER_HEADS, N_KEYS, N_KEYS)
    return pl.pallas_call(
        _peer_route_kernel,
        grid=(T // tm,),
        in_specs=[pl.BlockSpec((tm, D), lambda i: (i, 0)), pl.BlockSpec((1, D), lambda i: (0, 0)),
                  pl.BlockSpec((D, nq), lambda i: (0, 0)),
                  pl.BlockSpec((2 * PEER_HEADS, N_KEYS, N_KEYS), lambda i: (0, 0, 0))],
        out_specs=[pl.BlockSpec((D, tm), lambda i: (0, i))]
        + [pl.BlockSpec((PEER_HEADS, N_KEYS, tm), lambda i: (0, 0, i))] * 3,
        out_shape=[jax.ShapeDtypeStruct((D, T), BF16)]
        + [jax.ShapeDtypeStruct((PEER_HEADS, N_KEYS, T), F32)] * 3,
        compiler_params=_cparams(("parallel",), 40),
        name="peer_route",
    )(h, g.reshape(1, D), wq_bf16, sk_bf16)


def _gelu_tanh(x):
    return 0.5 * x * (1.0 + jnp.tanh(math.sqrt(2.0 / math.pi) * (x + 0.044715 * (x * x * x))))


def _peer_expert_kernel(xnt_ref, u_ref, vt_ref, th1_ref, e1_ref, e2_ref, h_ref, out_ref,
                        acc_scr, rows_scr, act0_scr, act1_scr, w0_scr, w1_scr, *, eb, cuts, tq):
    e = pl.program_id(1)
    rows, n_sub = eb // N_KEYS, len(cuts) - 1
    sublanes = rows_scr.shape[2]
    act_scr, w_scr = (act0_scr, act1_scr), (w0_scr, w1_scr)

    @pl.when(e == 0)
    def _():
        acc_scr[...] = jnp.zeros_like(acc_scr)

    row0 = pl.multiple_of(e * rows, rows)
    for k, tab in enumerate((th1_ref, e1_ref)):
        for hd in range(PEER_HEADS):
            block = tab[hd, pl.ds(row0, rows), :]
            for r in range(rows):
                rows_scr[k, hd * rows + r] = jnp.broadcast_to(block[r:r + 1, :], (sublanes, tq))

    def activations(sb):
        lo, hi = cuts[sb], cuts[sb + 1]
        a = jnp.dot(u_ref[lo:hi, :], xnt_ref[...], preferred_element_type=F32)
        act_scr[sb % 2][0:hi - lo, :] = _gelu_tanh(a.astype(BF16))

    def weights(sb):
        span = 64
        reps = (span // sublanes, 1)
        for r in range((cuts[sb + 1] - cuts[sb]) // N_KEYS):
            for t0 in range(0, tq, LANES):
                for i2 in range(0, N_KEYS, span):
                    w = jnp.zeros((span, LANES), BF16)
                    for hd in range(PEER_HEADS):
                        k = hd * rows + cuts[sb] // N_KEYS + r
                        th1 = jnp.tile(rows_scr[0, k, :, t0:t0 + LANES], reps)
                        e1 = jnp.tile(rows_scr[1, k, :, t0:t0 + LANES], reps).astype(BF16)
                        e2 = e2_ref[hd, i2:i2 + span, t0:t0 + LANES]
                        w = w + e1 * jnp.where(e2 >= th1, e2, 0.0).astype(BF16)
                    lo = r * N_KEYS + i2
                    w_scr[sb % 2][lo:lo + span, t0:t0 + LANES] = w

    def project(sb):
        lo, hi = cuts[sb], cuts[sb + 1]
        a = act_scr[sb % 2][0:hi - lo, :] * w_scr[sb % 2][0:hi - lo, :]
        acc_scr[...] += lax.dot_general(vt_ref[lo:hi, :], a, (((0,), (0,)), ((), ())), preferred_element_type=F32)

    weights(0)
    activations(0)
    for sb in range(n_sub):
        if sb + 1 < n_sub:
            weights(sb + 1)
            activations(sb + 1)
        project(sb)

    @pl.when(e == pl.num_programs(1) - 1)
    def _():
        out_ref[...] = h_ref[...] + acc_scr[...].T


def _peer_experts(xn, u_bf16, vt_bf16, tables, h, *, tq=512, eb=2048, cuts=(0, 1024, 2048)):
    T, D = h.shape
    E = u_bf16.shape[0]
    assert E == N_KEYS * N_KEYS and E % eb == 0 and eb % (F32_SUBLANES * N_KEYS) == 0 and T % tq == 0
    assert cuts[0] == 0 and cuts[-1] == eb and all(c % N_KEYS == 0 for c in cuts)
    sub = max(b - a for a, b in zip(cuts[:-1], cuts[1:]))
    table_spec = pl.BlockSpec((PEER_HEADS, N_KEYS, tq), lambda i, e: (0, 0, i))
    return pl.pallas_call(
        functools.partial(_peer_expert_kernel, eb=eb, cuts=tuple(cuts), tq=tq),
        grid=(T // tq, E // eb),
        in_specs=[pl.BlockSpec((D, tq), lambda i, e: (0, i)),
                  pl.BlockSpec((eb, D), lambda i, e: (e, 0)),
                  pl.BlockSpec((eb, D), lambda i, e: (e, 0)),
                  table_spec, table_spec, table_spec,
                  pl.BlockSpec((tq, D), lambda i, e: (i, 0))],
        out_specs=pl.BlockSpec((tq, D), lambda i, e: (i, 0)),
        out_shape=jax.ShapeDtypeStruct((T, D), F32),
        scratch_shapes=[pltpu.VMEM((D, tq), F32),
                        pltpu.VMEM((2, PEER_HEADS * (eb // N_KEYS), F32_SUBLANES, tq), F32),
                        pltpu.VMEM((sub, tq), BF16), pltpu.VMEM((sub, tq), BF16),
                        pltpu.VMEM((sub, tq), BF16), pltpu.VMEM((sub, tq), BF16)],
        compiler_params=_cparams(("parallel", "arbitrary"), 56),
        name="peer_experts",
    )(xn, u_bf16, vt_bf16, *tables, h)


def _ple_kernel(h_ref, g_ref, p_ref, wg_ref, wu_ref, out_ref):
    h = h_ref[...]
    gate = jax.nn.sigmoid(jnp.dot(_rms(h, g_ref[...]).astype(BF16), wg_ref[...], preferred_element_type=F32))
    up = jnp.dot(p_ref[...].astype(BF16), wu_ref[...], preferred_element_type=F32)
    out_ref[...] = h + up * gate


def _ple(h, g, p, wg_bf16, wu_bf16, *, tm=512):
    T, D = h.shape
    P = p.shape[1]
    return pl.pallas_call(
        _ple_kernel,
        grid=(T // tm,),
        in_specs=[pl.BlockSpec((tm, D), lambda i: (i, 0)), pl.BlockSpec((1, D), lambda i: (0, 0)),
                  pl.BlockSpec((tm, P), lambda i: (i, 0)), pl.BlockSpec((D, D), lambda i: (0, 0)),
                  pl.BlockSpec((P, D), lambda i: (0, 0))],
        out_specs=pl.BlockSpec((tm, D), lambda i: (i, 0)),
        out_shape=jax.ShapeDtypeStruct((T, D), F32),
        compiler_params=_cparams(("parallel",), 32),
        name="ple",
    )(h, g.reshape(1, D), p, wg_bf16, wu_bf16)


def _tile_gain(g, scale=1.0):
    return jnp.tile(g.astype(F32) * scale, LANES // HEAD_DIM)


def _ab_mixer(h, rope, B, S, g, w_in, w_out, a_q, a_k, b_q, b_k):
    A, Bw = A_HEADS * HEAD_DIM, B_HEADS * HEAD_DIM
    cuts = np.cumsum([A, A, A, Bw, Bw, Bw, IDX_HEADS * HEAD_DIM, HEAD_DIM, IDX_HEADS])
    base, ki0, wi0 = w_in[:, :cuts[6]], w_in[:, cuts[6]:cuts[7]], w_in[:, cuts[7]:cuts[8]]
    w_pad = jnp.concatenate([base, ki0, ki0, wi0, jnp.zeros((D_MODEL, LANES - IDX_HEADS), F32)], axis=1).astype(BF16)
    scale = HEAD_DIM ** -0.5 * LOG2E
    kinds = ["nr"] * 8 + ["p"] * 4 + ["nr"] * 8 + ["p"] * 4 + ["r"] * 3 + ["pa"]
    one = jnp.ones((LANES,), F32)
    gains = jnp.stack([_tile_gain(a_q, scale)] * 4 + [_tile_gain(a_k)] * 4 + [one] * 4
                      + [_tile_gain(b_q, scale)] * 4 + [_tile_gain(b_k)] * 4 + [one] * 8)
    y, aux = _project(h, g, w_pad, gains, rope, kinds)
    kmean = _block_means(y, 4, 4, B, S)
    tq = min(1024, S)
    oa = _flash("moba", y, y, y, 0, 4, 8, 4, B, S, tq=tq, ts=512, tk=tq, extra=kmean)
    bias = _dsa_select(y, aux, 24, 26, B, S)
    ob = _flash("dsa", y, y, y, 12, 16, 20, 4, B, S, tq=tq, ts=512, tk=tq, extra=bias)
    return _out_proj([oa, ob], w_out.astype(BF16), h)


def _c_mixer(h, rope, B, S, g, w_in, w_out, q_norm, k_norm):
    scale = HEAD_DIM ** -0.5 * LOG2E
    kinds = ["nr"] * 16 + ["p"] * 7 + ["pa"]
    one = jnp.ones((LANES,), F32)
    gains = jnp.stack([_tile_gain(q_norm, scale)] * 8 + [_tile_gain(k_norm)] * 8 + [one] * 8)
    y, _ = _project(h, g, w_in.astype(BF16), gains, rope, kinds)
    o = _flash("dilated", y, y, y, 0, 8, 16, 8, B, S, tq=512, tk=512)
    return _out_proj([o], w_out.astype(BF16), h)


def _peer(h, g, w_query, sub_keys, expert_u, expert_v):
    sk = sub_keys.reshape(2 * PEER_HEADS, N_KEYS, N_KEYS).astype(BF16)
    xn, *tables = _peer_route(h, g, w_query.astype(BF16), sk)
    return _peer_experts(xn, expert_u.astype(BF16), expert_v.astype(BF16), tables, h)


def kernel(x, p, positions, attn_norm, ffn_norm, ple_norm, ab_w_in, ab_w_out, a_q_norm, a_k_norm, b_q_norm,
           b_k_norm, c_w_in, c_w_out, c_q_norm, c_k_norm, peer_w_query, peer_sub_keys, peer_u, peer_v,
           ple_w_gate, ple_w_up):
    B, S, D = x.shape
    depth = p.shape[0]
    rope = _rope_tables(positions)
    h = x.reshape(B * S, D)
    for i in range(depth):
        j = i // 2
        if i % 2 == 0:
            h = _ab_mixer(h, rope, B, S, attn_norm[i], ab_w_in[j], ab_w_out[j],
                          a_q_norm[j], a_k_norm[j], b_q_norm[j], b_k_norm[j])
        else:
            h = _c_mixer(h, rope, B, S, attn_norm[i], c_w_in[j], c_w_out[j], c_q_norm[j], c_k_norm[j])
        h = _peer(h, ffn_norm[i], peer_w_query[i], peer_sub_keys[i], peer_u[i], peer_v[i])
        h = _ple(h, ple_norm[i], p[i].reshape(B * S, -1), ple_w_gate[i].astype(BF16), ple_w_up[i].astype(BF16))
    return h.reshape(B, S, D)
```

```python
import functools
import math

import numpy as np
import jax
import jax.numpy as jnp
from jax import lax
from jax.experimental import pallas as pl
from jax.experimental.pallas import tpu as pltpu

F32 = jnp.float32
BF16 = jnp.bfloat16

D_MODEL = 1024
HEAD_DIM = 64
ROT_HALF = 8
ROPE_THETA = 500000.0
NORM_EPS = 1e-6
NEG = -1e30
LOG2E = math.log2(math.e)
LANES = 128

A_HEADS = 8
B_HEADS = 8
C_HEADS = 16
IDX_HEADS = 4
MOBA_BLOCK = 256
MOBA_TOPK = 3
DSA_TOPK = 256
DILATED_PAIRS = ((128, 1), (512, 4), (2048, 16))
PEER_HEADS = 8
N_KEYS = 128
PEER_TOPK = 16

V7X_VMEM_BYTES = 64 * 1024 * 1024


def _cparams(sem, vmem_mb):
    assert vmem_mb * 2**20 < V7X_VMEM_BYTES
    return pltpu.CompilerParams(dimension_semantics=sem, vmem_limit_bytes=vmem_mb * 2**20)


def _rms(x, g):
    ms = jnp.mean(x * x, axis=-1, keepdims=True)
    return x * lax.rsqrt(ms + NORM_EPS) * g


def _nt_dot(a, b):
    return lax.dot_general(a, b, (((1,), (1,)), ((), ())), preferred_element_type=F32)


def _split_bf16(x):
    hi = x.astype(BF16)
    lo = (x - hi.astype(F32)).astype(BF16)
    return hi, lo


def _proj_kernel(x_ref, g_ref, w_ref, gains_ref, cos_ref, sa_ref, sb_ref, y_ref, aux_ref, *, kinds, group):
    xn = _rms(x_ref[...], g_ref[...]).astype(BF16)
    row = lax.broadcasted_iota(jnp.int32, (LANES, LANES), 0) // HEAD_DIM
    col = lax.broadcasted_iota(jnp.int32, (LANES, LANES), 1) // HEAD_DIM
    head_avg = jnp.where(row == col, 1.0 / HEAD_DIM, 0.0).astype(BF16)
    n_chunks = len(kinds)
    for g0 in range(0, n_chunks, group):
        g1 = min(g0 + group, n_chunks)
        yg = jnp.dot(xn, w_ref[:, g0 * LANES:g1 * LANES], preferred_element_type=F32)
        for c in range(g0, g1):
            y = yg[:, (c - g0) * LANES:(c - g0 + 1) * LANES]
            kind = kinds[c]
            if "n" in kind:
                hi, lo = _split_bf16(y * y)
                msq = (jnp.dot(hi, head_avg, preferred_element_type=F32)
                       + jnp.dot(lo, head_avg, preferred_element_type=F32))
                y = y * lax.rsqrt(msq + NORM_EPS) * gains_ref[c:c + 1, :]
            if "r" in kind:
                y = (y * cos_ref[...] + pltpu.roll(y, LANES - ROT_HALF, 1) * sa_ref[...]
                     + pltpu.roll(y, ROT_HALF, 1) * sb_ref[...])
            if "a" in kind:
                aux_ref[...] = y
            y_ref[c] = y.astype(BF16)


def _project(h, g, w_bf16, gains, rope, kinds, *, tm=512, group=4):
    T, D = h.shape
    n_chunks = len(kinds)
    assert w_bf16.shape == (D, n_chunks * LANES) and T % tm == 0
    cos, sa, sb = rope
    tok = lambda i: (i, 0)
    fixed = lambda i: (0, 0)
    return pl.pallas_call(
        functools.partial(_proj_kernel, kinds=tuple(kinds), group=group),
        grid=(T // tm,),
        in_specs=[pl.BlockSpec((tm, D), tok), pl.BlockSpec((1, D), fixed),
                  pl.BlockSpec((D, n_chunks * LANES), fixed), pl.BlockSpec((n_chunks, LANES), fixed),
                  pl.BlockSpec((tm, LANES), tok), pl.BlockSpec((tm, LANES), tok), pl.BlockSpec((tm, LANES), tok)],
        out_specs=[pl.BlockSpec((n_chunks, tm, LANES), lambda i: (0, i, 0)), pl.BlockSpec((tm, LANES), tok)],
        out_shape=[jax.ShapeDtypeStruct((n_chunks, T, LANES), BF16), jax.ShapeDtypeStruct((T, LANES), F32)],
        compiler_params=_cparams(("parallel",), 48),
        name="proj",
    )(h, g.reshape(1, D), w_bf16, gains, cos, sa, sb)


def _rope_tables(positions):
    B, S = positions.shape
    inv = ROPE_THETA ** (-jnp.arange(ROT_HALF, dtype=F32) / ROT_HALF)
    ang = positions.astype(F32)[..., None] * inv
    cos, sin = jnp.cos(ang), jnp.sin(ang)
    rest = HEAD_DIM - 2 * ROT_HALF
    one, zero, z8 = jnp.ones((B, S, rest), F32), jnp.zeros((B, S, rest), F32), jnp.zeros((B, S, ROT_HALF), F32)
    tile = lambda parts: jnp.tile(jnp.concatenate(parts, axis=-1), (1, 1, LANES // HEAD_DIM)).reshape(B * S, LANES)
    return tile([cos, cos, one]), tile([-sin, z8, zero]), tile([z8, sin, zero])


def _kmean_kernel(k_ref, o_ref):
    n = pl.program_id(1)

    @pl.when(n == 0)
    def _():
        o_ref[...] = jnp.zeros_like(o_ref)

    o_ref[:, pl.ds(n, 1), :] = jnp.mean(k_ref[...].astype(F32), axis=1, keepdims=True)


def _block_means(y, k_off, pairs, B, S):
    nb = S // MOBA_BLOCK
    assert nb <= LANES and k_off % pairs == 0
    return pl.pallas_call(
        _kmean_kernel,
        grid=(B, nb),
        in_specs=[pl.BlockSpec((pairs, MOBA_BLOCK, LANES), lambda b, n: (k_off // pairs, b * nb + n, 0))],
        out_specs=pl.BlockSpec((pairs, LANES, LANES), lambda b, n: (0, b, 0)),
        out_shape=jax.ShapeDtypeStruct((pairs, B * LANES, LANES), F32),
        compiler_params=_cparams(("parallel", "arbitrary"), 16),
        name="moba_kmean",
    )(y)


MASK_BIG = 2.0 ** 20


def _flash_kernel(*refs, mode, pairs, tq, ts, tk, n_steps, back):
    q_ref, k_ref, v_ref, x_ref, o_ref, m_scr, acc_scr, qa_scr = refs[:8]
    s_scr, p_scr, alpha_scr = refs[8:10], refs[10:12], refs[12:14]
    if mode == "dilated":
        b_ref = x_ref
    elif mode == "dsa":
        b_ref, bias_scr = x_ref, refs[14]
    else:
        km_ref, bias_scr, oh_scr = x_ref, refs[14], refs[15]
    i = pl.program_id(1)
    j = pl.program_id(2)
    lane = lax.broadcasted_iota(jnp.int32, (1, LANES), 1)
    halves = (lane < HEAD_DIM, lane >= HEAD_DIM)
    if mode == "dilated":
        kt = i * (tq // tk) - back + j
        active = kt >= 0
    else:
        kt = j
        active = j * tk < (i + 1) * tq
    tpos = i * tq + lax.broadcasted_iota(jnp.int32, (tq, 1), 0)

    n_heads = 2 * pairs
    n_sub = tq // ts
    n_units = n_sub * n_heads

    @pl.when(j == 0)
    def _():
        m_scr[...] = jnp.full_like(m_scr, NEG)
        acc_scr[...] = jnp.zeros_like(acc_scr)
        blk = lax.broadcasted_iota(jnp.int32, (ts, LANES), 1)
        for sub in range(n_sub):
            rows = slice(sub * ts, (sub + 1) * ts)
            qblk = tpos[rows] // MOBA_BLOCK
            for c in range(pairs):
                if mode == "moba":
                    km_hi, km_lo = _split_bf16(km_ref[c])
                for half in range(2):
                    qm = jnp.where(halves[half], q_ref[c, rows, :], jnp.zeros((), BF16))
                    if mode == "moba":
                        gate = jnp.where(blk < qblk, _nt_dot(qm, km_hi) + _nt_dot(qm, km_lo), NEG)
                        rest, thr = gate, None
                        for _ in range(MOBA_TOPK):
                            thr = jnp.max(rest, axis=1, keepdims=True)
                            rest = jnp.where(rest >= thr, NEG, rest)
                        attended = ((gate >= thr) & (blk < qblk)) | (blk == qblk)
                        feat = jnp.where(attended, 0.0, -MASK_BIG)
                        if half == 0:
                            feat = pltpu.roll(feat, HEAD_DIM, 1)
                        qm = jnp.where(halves[half], qm, feat.astype(BF16))
                    qa_scr[sub * n_heads + 2 * c + half] = qm

    def attend(bias):
        if mode == "moba":
            kblk = (kt * tk + lax.broadcasted_iota(jnp.int32, (tk, 1), 0)) // MOBA_BLOCK
            oh_scr[0] = jnp.where(lane == kblk + HEAD_DIM, 1.0, 0.0).astype(BF16)
            oh_scr[1] = jnp.where(lane == kblk, 1.0, 0.0).astype(BF16)

        def scores(u, par):
            kp = k_ref[(u % n_heads) // 2]
            if mode == "moba":
                kp = jnp.where(halves[par], kp, oh_scr[par])
            s_scr[par][...] = _nt_dot(qa_scr[u], kp)

        def bias_rows(u):
            start = u // n_heads * ts
            if isinstance(start, int):
                return bias[start:start + ts, :]
            return bias[pl.ds(pl.multiple_of(start, ts), ts), :]

        def softmax(u, par):
            s = s_scr[par][...]
            if bias is not None:
                s = s + (bias[...] if n_sub == 1 else bias_rows(u))
            m_old = m_scr[u]
            m_new = jnp.maximum(m_old, jnp.max(s, axis=1, keepdims=True))
            p_scr[par][...] = jnp.exp2(s - m_new).astype(BF16)
            alpha_scr[par][...] = jnp.exp2(m_old - m_new)
            m_scr[u] = m_new

        def values(u, par):
            va = jnp.where(halves[par], v_ref[(u % n_heads) // 2], jnp.ones((), BF16))
            acc_scr[u] = alpha_scr[par][...] * acc_scr[u] + jnp.dot(p_scr[par][...], va, preferred_element_type=F32)

        scores(0, 0)
        scores(1, 1)
        softmax(0, 0)

        def body(c, carry):
            scores(2 * c + 2, 0)
            softmax(2 * c + 1, 1)
            values(2 * c, 0)
            scores(2 * c + 3, 1)
            softmax(2 * c + 2, 0)
            values(2 * c + 1, 1)
            return carry

        lax.fori_loop(0, n_units // 2 - 1, body, 0)
        softmax(n_units - 1, 1)
        values(n_units - 2, 0)
        values(n_units - 1, 1)

    if mode == "moba":
        on_diagonal = (kt + 1) * tk > i * tq

        @pl.when(active & on_diagonal)
        def _():
            kpos = kt * tk + lax.broadcasted_iota(jnp.int32, (tq, tk), 1)
            bias_scr[...] = jnp.where(kpos <= tpos, 0.0, -jnp.inf)
            attend(bias_scr)

        @pl.when(active & jnp.logical_not(on_diagonal))
        def _():
            attend(None)
    elif mode == "dsa":
        @pl.when(active)
        def _():
            for part in range(b_ref.shape[1]):
                width = b_ref.shape[3]
                bias_scr[:, part * width:(part + 1) * width] = b_ref[0, part].astype(F32)
            attend(bias_scr)
    else:
        @pl.when(active)
        def _():
            attend(b_ref.at[0])

    @pl.when(j == n_steps - 1)
    def _():
        for sub in range(n_sub):
            for c in range(pairs):
                a0, a1 = acc_scr[sub * n_heads + 2 * c], acc_scr[sub * n_heads + 2 * c + 1]
                num = jnp.where(halves[0], a0, a1)
                den = pltpu.roll(jnp.where(halves[0], a1, a0), HEAD_DIM, 1)
                o_ref[c, sub * ts:(sub + 1) * ts, :] = (num / den).astype(BF16)


def _dilated_bias_table(tq, tk, back, n_steps):
    r = np.arange(tq)[:, None]
    c = np.arange(tk)[None, :]
    tabs = []
    for j in range(n_steps):
        d = (back - j) * tk + r - c
        mult = sum(((d >= 0) & (d <= w) & (d % dil == 0)).astype(np.float64) for w, dil in DILATED_PAIRS)
        tabs.append(np.where(mult > 0, np.log2(np.maximum(mult, 1.0)), -np.inf))
    return jnp.asarray(np.stack(tabs), F32)


def _flash(mode, q_arr, k_arr, v_arr, q_off, k_off, v_off, pairs, B, S, *, tq, tk, ts=None, extra=None):
    ts = ts or tq
    nq, nk = S // tq, S // tk
    n_units = tq // ts * 2 * pairs
    assert S % tq == 0 and S % tk == 0 and tq % tk == 0 and tq % ts == 0
    assert q_off % pairs == 0 and k_off % pairs == 0 and v_off % pairs == 0
    if mode == "dilated":
        max_back = max(w for w, _ in DILATED_PAIRS)
        back = -(-max_back // tk)
        n_steps = back + tq // tk
        kv_tile = lambda i, j: jnp.maximum(i * (tq // tk) - back + j, 0)
    else:
        back = 0
        n_steps = nk
        kv_tile = lambda i, j: jnp.minimum(j, ((i + 1) * tq - 1) // tk)
    in_specs = [pl.BlockSpec((pairs, tq, LANES), lambda b, i, j: (q_off // pairs, b * nq + i, 0)),
                pl.BlockSpec((pairs, tk, LANES), lambda b, i, j: (k_off // pairs, b * nk + kv_tile(i, j), 0)),
                pl.BlockSpec((pairs, tk, LANES), lambda b, i, j: (v_off // pairs, b * nk + kv_tile(i, j), 0))]
    args = [q_arr, k_arr, v_arr]
    scratch = [pltpu.VMEM((n_units, ts, 1), F32), pltpu.VMEM((n_units, ts, LANES), F32),
               pltpu.VMEM((n_units, ts, LANES), BF16), pltpu.VMEM((ts, tk), F32), pltpu.VMEM((ts, tk), F32),
               pltpu.VMEM((ts, tk), BF16), pltpu.VMEM((ts, tk), BF16),
               pltpu.VMEM((ts, 1), F32), pltpu.VMEM((ts, 1), F32)]
    if mode == "moba":
        assert tk % MOBA_BLOCK == 0 and S // MOBA_BLOCK <= HEAD_DIM
        in_specs.append(pl.BlockSpec((pairs, LANES, LANES), lambda b, i, j: (0, b, 0)))
        args.append(extra)
        scratch += [pltpu.VMEM((tq, tk), F32), pltpu.VMEM((2, tk, LANES), BF16)]
    elif mode == "dsa":
        ck = extra.shape[3]
        assert tk % ck == 0
        in_specs.append(pl.BlockSpec((1, tk // ck, tq, ck), lambda b, i, j: (b, kv_tile(i, j), i, 0)))
        args.append(extra)
        scratch.append(pltpu.VMEM((tq, tk), F32))
    else:
        in_specs.append(pl.BlockSpec((1, tq, tk), lambda b, i, j: (j, 0, 0)))
        args.append(_dilated_bias_table(tq, tk, back, n_steps))
    return pl.pallas_call(
        functools.partial(_flash_kernel, mode=mode, pairs=pairs, tq=tq, ts=ts, tk=tk, n_steps=n_steps, back=back),
        grid=(B, nq, n_steps),
        in_specs=in_specs,
        out_specs=pl.BlockSpec((pairs, tq, LANES), lambda b, i, j: (0, b * nq + i, 0)),
        out_shape=jax.ShapeDtypeStruct((pairs, B * S, LANES), BF16),
        scratch_shapes=scratch,
        compiler_params=_cparams(("parallel", "parallel", "arbitrary"), 48),
        name="flash_" + mode,
    )(*args)


F32_SUBLANES = 8
BF16_SUBLANES = 16


def _dsa_select_kernel(qi_ref, ki_ref, wt_ref, o_ref, key_scr, top_scr, tri_scr, *, tq, ck, n_chunks, topk):
    i = pl.program_id(1)
    tri_scr[...] = jnp.where(lax.broadcasted_iota(jnp.int32, (ck, ck), 0) >= lax.broadcasted_iota(jnp.int32, (ck, ck), 1),
                             1.0, 0.0).astype(BF16)
    lane = lax.broadcasted_iota(jnp.int32, (1, LANES), 1)
    halves = (lane < HEAD_DIM, lane >= HEAD_DIM)
    tpos = i * tq + lax.broadcasted_iota(jnp.int32, (1, tq), 1)
    n_live = ((i + 1) * tq + ck - 1) // ck
    wt = wt_ref[...]

    def score_chunk(c, carry):
        kc = ki_ref[0, pl.ds(pl.multiple_of(c * ck, ck), ck), :]
        score = jnp.zeros((ck, tq), F32)
        for h in range(IDX_HEADS):
            qm = jnp.where(halves[h % 2], qi_ref[h // 2], jnp.zeros((), BF16))
            score = score + wt[h:h + 1, :] * jnp.maximum(_nt_dot(kc, qm), 0.0)
        kpos = c * ck + lax.broadcasted_iota(jnp.int32, (ck, tq), 0)
        score = jnp.where(jnp.abs(score) >= jnp.finfo(F32).tiny, score, 0.0)
        score = jnp.where(kpos <= tpos, score, NEG)
        bits = pltpu.bitcast(score, jnp.int32)
        key_scr[c] = jnp.where(bits < 0, bits ^ jnp.int32(0x7FFFFFFF), bits)
        top_scr[c] = pltpu.bitcast(bits & jnp.int32(-65536), F32).astype(BF16)
        return carry

    lax.fori_loop(0, n_live, score_chunk, 0)

    def count(src, pred, dtype, group):
        assert ck // group <= 256

        def body(c, acc):
            hit = jnp.where(pred(src[c]), jnp.ones((), dtype), jnp.zeros((), dtype))
            parts = [hit[s * group:(s + 1) * group, :] for s in range(ck // group)]
            lanes = parts[:8]
            for s in range(8, len(parts)):
                lanes[s % 8] = lanes[s % 8] + parts[s]
            while len(lanes) > 1:
                lanes = [a + b for a, b in zip(lanes[0::2], lanes[1::2])]
            return acc + lanes[0].astype(F32)

        acc = lax.fori_loop(0, n_live, body, jnp.zeros((group, tq), F32))
        return jnp.sum(acc, axis=0, keepdims=True)

    def upper_bit(it, lo):
        cand = lo + jnp.left_shift(jnp.int32(1), 15 - it)
        pattern = jnp.where(cand < 0, cand ^ jnp.int32(0x7FFF), cand)
        pattern = jnp.where((pattern > 0) & (pattern < 0x80), jnp.int32(0x80), pattern)
        value = pltpu.bitcast(jnp.left_shift(pattern, 16), F32).astype(BF16)
        return jnp.where(count(top_scr, lambda sc: sc >= value, BF16, BF16_SUBLANES) >= topk, cand, lo)

    upper = lax.fori_loop(0, 16, upper_bit, jnp.full((1, tq), -2**15, jnp.int32))

    def lower_bit(it, lo):
        cand = lo + jnp.left_shift(jnp.int32(1), 15 - it)
        return jnp.where(count(key_scr, lambda kc: kc >= cand, F32, F32_SUBLANES) >= topk, cand, lo)

    thr = lax.fori_loop(0, 16, lower_bit, jnp.left_shift(upper, 16))
    need = topk - count(key_scr, lambda kc: kc > thr, F32, F32_SUBLANES)

    def emit(c, before):
        kc = key_scr[c]
        kpos = c * ck + lax.broadcasted_iota(jnp.int32, (ck, tq), 0)
        equal = kc == thr
        rank = before + jnp.dot(tri_scr[...], jnp.where(equal, 1.0, 0.0).astype(BF16), preferred_element_type=F32)
        take = ((kc > thr) | (equal & (rank <= need))) & (kpos <= tpos)
        o_ref[0, c] = jnp.where(take, 0.0, -jnp.inf).T.astype(BF16)
        return rank[ck - 1:ck, :]

    lax.fori_loop(0, n_live, emit, jnp.zeros((1, tq), F32))

    def blank(c, carry):
        o_ref[0, c] = jnp.full((tq, ck), -jnp.inf, BF16)
        return carry

    lax.fori_loop(n_live, n_chunks, blank, 0)


def _dsa_select(y, aux, qi_off, ki_off, B, S, *, tq=256, ck=512):
    n_chunks = S // ck
    assert S % ck == 0 and S % tq == 0 and qi_off % 2 == 0
    nq = S // tq
    topk = min(DSA_TOPK, S // 4)
    wt = aux[:, :F32_SUBLANES].T
    return pl.pallas_call(
        functools.partial(_dsa_select_kernel, tq=tq, ck=ck, n_chunks=n_chunks, topk=topk),
        grid=(B, nq),
        in_specs=[pl.BlockSpec((2, tq, LANES), lambda b, i: (qi_off // 2, b * nq + i, 0)),
                  pl.BlockSpec((1, S, LANES), lambda b, i: (ki_off, b, 0)),
                  pl.BlockSpec((F32_SUBLANES, tq), lambda b, i: (0, b * nq + i))],
        out_specs=pl.BlockSpec((1, n_chunks, tq, ck), lambda b, i: (b, 0, i, 0)),
        out_shape=jax.ShapeDtypeStruct((B, n_chunks, S, ck), BF16),
        scratch_shapes=[pltpu.VMEM((n_chunks, ck, tq), jnp.int32), pltpu.VMEM((n_chunks, ck, tq), BF16),
                        pltpu.VMEM((ck, ck), BF16)],
        compiler_params=_cparams(("parallel", "parallel"), 48),
        name="dsa_select",
    )(y, y, wt)


def _out_proj_kernel(*refs, n_in):
    o_refs, w_ref, h_ref, out_ref = refs[:n_in], refs[n_in], refs[n_in + 1], refs[n_in + 2]
    slabs = [o[c] for o in o_refs for c in range(o.shape[0])]
    o = jnp.concatenate(slabs, axis=-1)
    out_ref[...] = h_ref[...] + jnp.dot(o, w_ref[...], preferred_element_type=F32)


def _out_proj(os_, w_bf16, h, *, tm=512):
    T, D = h.shape
    width = sum(o.shape[0] for o in os_) * LANES
    assert w_bf16.shape == (width, D)
    in_specs = [pl.BlockSpec((o.shape[0], tm, LANES), lambda i: (0, i, 0)) for o in os_]
    in_specs += [pl.BlockSpec((width, D), lambda i: (0, 0)), pl.BlockSpec((tm, D), lambda i: (i, 0))]
    return pl.pallas_call(
        functools.partial(_out_proj_kernel, n_in=len(os_)),
        grid=(T // tm,),
        in_specs=in_specs,
        out_specs=pl.BlockSpec((tm, D), lambda i: (i, 0)),
        out_shape=jax.ShapeDtypeStruct((T, D), F32),
        compiler_params=_cparams(("parallel",), 32),
        name="out_proj",
    )(*os_, w_bf16, h)


def _sort_bitonic_desc(xs, strides):
    xs = list(xs)
    for stride in strides:
        for i in range(len(xs)):
            if not i & stride:
                xs[i], xs[i + stride] = jnp.maximum(xs[i], xs[i + stride]), jnp.minimum(xs[i], xs[i + stride])
    return xs


def _top_values(s, k):
    n, t = s.shape
    assert k & (k - 1) == 0 and n == k * F32_SUBLANES
    xs = [s[i * F32_SUBLANES:(i + 1) * F32_SUBLANES, :] for i in range(k)]
    size = 2
    while size <= k:
        stride = size // 2
        while stride >= 1:
            for i in range(k):
                j = i ^ stride
                if j > i:
                    hi, lo = jnp.maximum(xs[i], xs[j]), jnp.minimum(xs[i], xs[j])
                    xs[i], xs[j] = (hi, lo) if (i & size) == 0 else (lo, hi)
            stride //= 2
        size *= 2
    dropped = None
    shift = F32_SUBLANES // 2
    while shift >= 1:
        other = [pltpu.roll(x, shift, 0) for x in xs]
        lows = [jnp.minimum(xs[i], other[k - 1 - i]) for i in range(k)]
        xs = _sort_bitonic_desc([jnp.maximum(xs[i], other[k - 1 - i]) for i in range(k)],
                                [st for st in (8, 4, 2, 1) if st < k])
        while len(lows) > 1:
            lows = [jnp.maximum(a, b) for a, b in zip(lows[0::2], lows[1::2])]
        dropped = lows[0] if dropped is None else jnp.maximum(jnp.maximum(dropped, pltpu.roll(dropped, shift, 0)), lows[0])
        shift //= 2
    return jnp.concatenate([x[0:1, :] for x in xs], axis=0), dropped[0:1, :]


def _pair_candidates(t1, t2):
    k = PEER_TOPK
    parts = [t1[0:1] + t2]
    parts += [t1[a:a + 1] + t2[0:k // 2] for a in range(1, k // 2)]
    parts.append(t1[k // 2:k] + t2[0:1])
    return jnp.concatenate(parts, axis=0)


def _kth_largest(c, k):
    left = jnp.full((1, c.shape[1]), float(k), F32)
    tau = jnp.zeros((1, c.shape[1]), F32)
    for _ in range(k):
        cur = jnp.max(c, axis=0, keepdims=True)
        hit = c >= cur
        tau = jnp.where(left > 0.0, cur, tau)
        left = left - jnp.sum(jnp.where(hit, 1.0, 0.0), axis=0, keepdims=True)
        c = jnp.where(hit, -jnp.inf, c)
    return tau


def _peer_route_kernel(*refs, n_in):
    o_refs, (wo_ref, h_ref, g_ref, wq_ref, sk_ref, h1_ref, xnt_ref, th1_ref, e1_ref, e2_ref) = refs[:n_in], refs[n_in:]
    o = jnp.concatenate([r[c] for r in o_refs for c in range(r.shape[0])], axis=-1)
    h1 = h_ref[...] + jnp.dot(o, wo_ref[...], preferred_element_type=F32)
    h1_ref[...] = h1
    xn = _rms(h1, g_ref[...])
    xnt_ref[...] = xn.T.astype(BF16)
    xn = xn.astype(BF16)
    for hd in range(PEER_HEADS):
        q = jnp.dot(xn, wq_ref[:, hd * 2 * N_KEYS:(hd + 1) * 2 * N_KEYS], preferred_element_type=F32).astype(BF16)
        s1 = _nt_dot(sk_ref[2 * hd], q[:, :N_KEYS])
        s2 = _nt_dot(sk_ref[2 * hd + 1], q[:, N_KEYS:])
        s1 = s1 - jnp.max(s1, axis=0, keepdims=True)
        s2 = s2 - jnp.max(s2, axis=0, keepdims=True)
        (t1, next1), (t2, next2) = _top_values(s1, PEER_TOPK), _top_values(s2, PEER_TOPK)
        cand = _pair_candidates(t1, t2)
        chosen = cand >= _kth_largest(cand, PEER_TOPK)
        log_z = jnp.log(jnp.sum(jnp.where(chosen, jnp.exp(cand), 0.0), axis=0, keepdims=True))
        runner_up = jnp.maximum(jnp.max(jnp.where(chosen, -jnp.inf, cand), axis=0, keepdims=True),
                                jnp.maximum(next1 + t2[0:1], t1[0:1] + next2))
        smallest = jnp.min(jnp.where(chosen, cand, jnp.inf), axis=0, keepdims=True)
        tau = 0.5 * (smallest + runner_up)
        s1 = s1 - log_z
        th1_ref[hd] = jnp.exp((tau - log_z) - s1)
        e1_ref[hd] = jnp.exp(s1)
        e2_ref[hd] = jnp.exp(s2)


def _peer_route(os_, wo_bf16, h, g, wq_bf16, sk_bf16, *, tm=256):
    T, D = h.shape
    nq = PEER_HEADS * 2 * N_KEYS
    width = sum(o.shape[0] for o in os_) * LANES
    assert wq_bf16.shape == (D, nq) and sk_bf16.shape == (2 * PEER_HEADS, N_KEYS, N_KEYS) and wo_bf16.shape == (width, D)
    return pl.pallas_call(
        functools.partial(_peer_route_kernel, n_in=len(os_)),
        grid=(T // tm,),
        in_specs=[pl.BlockSpec((o.shape[0], tm, LANES), lambda i: (0, i, 0)) for o in os_]
        + [pl.BlockSpec((width, D), lambda i: (0, 0)),
           pl.BlockSpec((tm, D), lambda i: (i, 0)), pl.BlockSpec((1, D), lambda i: (0, 0)),
           pl.BlockSpec((D, nq), lambda i: (0, 0)),
           pl.BlockSpec((2 * PEER_HEADS, N_KEYS, N_KEYS), lambda i: (0, 0, 0))],
        out_specs=[pl.BlockSpec((tm, D), lambda i: (i, 0)), pl.BlockSpec((D, tm), lambda i: (0, i))]
        + [pl.BlockSpec((PEER_HEADS, N_KEYS, tm), lambda i: (0, 0, i))] * 3,
        out_shape=[jax.ShapeDtypeStruct((T, D), F32), jax.ShapeDtypeStruct((D, T), BF16)]
        + [jax.ShapeDtypeStruct((PEER_HEADS, N_KEYS, T), F32)] * 3,
        compiler_params=_cparams(("parallel",), 44),
        name="peer_route",
    )(*os_, wo_bf16, h, g.reshape(1, D), wq_bf16, sk_bf16)


def _gelu_tanh(x):
    return 0.5 * x * (1.0 + jnp.tanh(math.sqrt(2.0 / math.pi) * (x + 0.044715 * (x * x * x))))


def _peer_expert_kernel(xnt_ref, u_ref, vt_ref, th1_ref, e1_ref, e2_ref, h_ref, out_ref,
                        acc_scr, rows_scr, act0_scr, act1_scr, w0_scr, w1_scr, *, eb, cuts, tq):
    e = pl.program_id(1)
    rows, n_sub = eb // N_KEYS, len(cuts) - 1
    sublanes = rows_scr.shape[2]
    act_scr, w_scr = (act0_scr, act1_scr), (w0_scr, w1_scr)

    @pl.when(e == 0)
    def _():
        acc_scr[...] = jnp.zeros_like(acc_scr)

    row0 = pl.multiple_of(e * rows, rows)
    for k, tab in enumerate((th1_ref, e1_ref)):
        for hd in range(PEER_HEADS):
            block = tab[hd, pl.ds(row0, rows), :]
            for r in range(rows):
                rows_scr[k, hd * rows + r] = jnp.broadcast_to(block[r:r + 1, :], (sublanes, tq))

    def activations(sb):
        lo, hi = cuts[sb], cuts[sb + 1]
        a = jnp.dot(u_ref[lo:hi, :], xnt_ref[...], preferred_element_type=F32)
        act_scr[sb % 2][0:hi - lo, :] = _gelu_tanh(a.astype(BF16))

    def weights(sb):
        span = 64
        reps = (span // sublanes, 1)
        for r in range((cuts[sb + 1] - cuts[sb]) // N_KEYS):
            for t0 in range(0, tq, LANES):
                for i2 in range(0, N_KEYS, span):
                    w = jnp.zeros((span, LANES), BF16)
                    for hd in range(PEER_HEADS):
                        k = hd * rows + cuts[sb] // N_KEYS + r
                        th1 = jnp.tile(rows_scr[0, k, :, t0:t0 + LANES], reps)
                        e1 = jnp.tile(rows_scr[1, k, :, t0:t0 + LANES], reps).astype(BF16)
                        e2 = e2_ref[hd, i2:i2 + span, t0:t0 + LANES]
                        w = w + e1 * jnp.where(e2 >= th1, e2, 0.0).astype(BF16)
                    lo = r * N_KEYS + i2
                    w_scr[sb % 2][lo:lo + span, t0:t0 + LANES] = w

    def project(sb):
        lo, hi = cuts[sb], cuts[sb + 1]
        a = act_scr[sb % 2][0:hi - lo, :] * w_scr[sb % 2][0:hi - lo, :]
        acc_scr[...] += jnp.dot(vt_ref[:, lo:hi], a, preferred_element_type=F32)

    weights(0)
    activations(0)
    for sb in range(n_sub):
        if sb + 1 < n_sub:
            weights(sb + 1)
            activations(sb + 1)
        project(sb)

    @pl.when(e == pl.num_programs(1) - 1)
    def _():
        out_ref[...] = h_ref[...] + acc_scr[...].T


def _peer_experts(xn, u_bf16, vt_bf16, tables, h, *, tq=512, eb=2048, cuts=(0, 1024, 2048)):
    T, D = h.shape
    E = u_bf16.shape[0]
    assert E == N_KEYS * N_KEYS and E % eb == 0 and eb % (F32_SUBLANES * N_KEYS) == 0 and T % tq == 0
    assert cuts[0] == 0 and cuts[-1] == eb and all(c % N_KEYS == 0 for c in cuts)
    sub = max(b - a for a, b in zip(cuts[:-1], cuts[1:]))
    table_spec = pl.BlockSpec((PEER_HEADS, N_KEYS, tq), lambda i, e: (0, 0, i))
    return pl.pallas_call(
        functools.partial(_peer_expert_kernel, eb=eb, cuts=tuple(cuts), tq=tq),
        grid=(T // tq, E // eb),
        in_specs=[pl.BlockSpec((D, tq), lambda i, e: (0, i)),
                  pl.BlockSpec((eb, D), lambda i, e: (e, 0)),
                  pl.BlockSpec((D, eb), lambda i, e: (0, e)),
                  table_spec, table_spec, table_spec,
                  pl.BlockSpec((tq, D), lambda i, e: (i, 0))],
        out_specs=pl.BlockSpec((tq, D), lambda i, e: (i, 0)),
        out_shape=jax.ShapeDtypeStruct((T, D), F32),
        scratch_shapes=[pltpu.VMEM((D, tq), F32),
                        pltpu.VMEM((2, PEER_HEADS * (eb // N_KEYS), F32_SUBLANES, tq), F32),
                        pltpu.VMEM((sub, tq), BF16), pltpu.VMEM((sub, tq), BF16),
                        pltpu.VMEM((sub, tq), BF16), pltpu.VMEM((sub, tq), BF16)],
        compiler_params=_cparams(("parallel", "arbitrary"), 56),
        name="peer_experts",
    )(xn, u_bf16, vt_bf16, *tables, h)


def _ple_kernel(h_ref, g_ref, p_ref, wg_ref, wu_ref, out_ref):
    h = h_ref[...]
    gate = jax.nn.sigmoid(jnp.dot(_rms(h, g_ref[...]).astype(BF16), wg_ref[...], preferred_element_type=F32))
    up = jnp.dot(p_ref[...].astype(BF16), wu_ref[...], preferred_element_type=F32)
    out_ref[...] = h + up * gate


def _ple(h, g, p, wg_bf16, wu_bf16, *, tm=512):
    T, D = h.shape
    P = p.shape[1]
    return pl.pallas_call(
        _ple_kernel,
        grid=(T // tm,),
        in_specs=[pl.BlockSpec((tm, D), lambda i: (i, 0)), pl.BlockSpec((1, D), lambda i: (0, 0)),
                  pl.BlockSpec((tm, P), lambda i: (i, 0)), pl.BlockSpec((D, D), lambda i: (0, 0)),
                  pl.BlockSpec((P, D), lambda i: (0, 0))],
        out_specs=pl.BlockSpec((tm, D), lambda i: (i, 0)),
        out_shape=jax.ShapeDtypeStruct((T, D), F32),
        compiler_params=_cparams(("parallel",), 32),
        name="ple",
    )(h, g.reshape(1, D), p, wg_bf16, wu_bf16)


def _tile_gain(g, scale=1.0):
    return jnp.tile(g.astype(F32) * scale, LANES // HEAD_DIM)


def _ab_mixer(h, rope, B, S, g, w_in, w_out, a_q, a_k, b_q, b_k):
    A, Bw = A_HEADS * HEAD_DIM, B_HEADS * HEAD_DIM
    cuts = np.cumsum([A, A, A, Bw, Bw, Bw, IDX_HEADS * HEAD_DIM, HEAD_DIM, IDX_HEADS])
    base, ki0, wi0 = w_in[:, :cuts[6]], w_in[:, cuts[6]:cuts[7]], w_in[:, cuts[7]:cuts[8]]
    w_pad = jnp.concatenate([base, ki0, ki0, wi0, jnp.zeros((D_MODEL, LANES - IDX_HEADS), F32)], axis=1).astype(BF16)
    scale = HEAD_DIM ** -0.5 * LOG2E
    kinds = ["nr"] * 8 + ["p"] * 4 + ["nr"] * 8 + ["p"] * 4 + ["r"] * 3 + ["pa"]
    one = jnp.ones((LANES,), F32)
    gains = jnp.stack([_tile_gain(a_q, scale)] * 4 + [_tile_gain(a_k)] * 4 + [one] * 4
                      + [_tile_gain(b_q, scale)] * 4 + [_tile_gain(b_k)] * 4 + [one] * 8)
    y, aux = _project(h, g, w_pad, gains, rope, kinds)
    kmean = _block_means(y, 4, 4, B, S)
    tq = min(1024, S)
    oa = _flash("moba", y, y, y, 0, 4, 8, 4, B, S, tq=tq, ts=512, tk=tq, extra=kmean)
    bias = _dsa_select(y, aux, 24, 26, B, S)
    ob = _flash("dsa", y, y, y, 12, 16, 20, 4, B, S, tq=tq, ts=512, tk=tq, extra=bias)
    return [oa, ob], w_out.astype(BF16)


def _c_mixer(h, rope, B, S, g, w_in, w_out, q_norm, k_norm):
    scale = HEAD_DIM ** -0.5 * LOG2E
    kinds = ["nr"] * 16 + ["p"] * 7 + ["pa"]
    one = jnp.ones((LANES,), F32)
    gains = jnp.stack([_tile_gain(q_norm, scale)] * 8 + [_tile_gain(k_norm)] * 8 + [one] * 8)
    y, _ = _project(h, g, w_in.astype(BF16), gains, rope, kinds)
    o = _flash("dilated", y, y, y, 0, 8, 16, 8, B, S, tq=512, tk=512)
    return [o], w_out.astype(BF16)


def _peer(mixed, h, g, w_query, sub_keys, expert_u, expert_v):
    sk = sub_keys.reshape(2 * PEER_HEADS, N_KEYS, N_KEYS).astype(BF16)
    h1, xn, *tables = _peer_route(mixed[0], mixed[1], h, g, w_query.astype(BF16), sk)
    return _peer_experts(xn, expert_u.astype(BF16), expert_v.T.astype(BF16), tables, h1)


def kernel(x, p, positions, attn_norm, ffn_norm, ple_norm, ab_w_in, ab_w_out, a_q_norm, a_k_norm, b_q_norm,
           b_k_norm, c_w_in, c_w_out, c_q_norm, c_k_norm, peer_w_query, peer_sub_keys, peer_u, peer_v,
           ple_w_gate, ple_w_up):
    B, S, D = x.shape
    depth = p.shape[0]
    rope = _rope_tables(positions)
    h = x.reshape(B * S, D)
    for i in range(depth):
        j = i // 2
        if i % 2 == 0:
            mixed = _ab_mixer(h, rope, B, S, attn_norm[i], ab_w_in[j], ab_w_out[j],
                              a_q_norm[j], a_k_norm[j], b_q_norm[j], b_k_norm[j])
        else:
            mixed = _c_mixer(h, rope, B, S, attn_norm[i], c_w_in[j], c_w_out[j], c_q_norm[j], c_k_norm[j])
        h = _peer(mixed, h, ffn_norm[i], peer_w_query[i], peer_sub_keys[i], peer_u[i], peer_v[i])
        h = _ple(h, ple_norm[i], p[i].reshape(B * S, -1), ple_w_gate[i].astype(BF16), ple_w_up[i].astype(BF16))
    return h.reshape(B, S, D)
```
